```python
import jax, jax.numpy as jnp
from jax import lax
import numpy as np

D_MODEL = 2048
BATCH = 2
SEQ = 8192
DEPTH = 1

CHUNK = 64
N_META = 16
Q_BLOCK = 128
SB_HEADS = 8
SB_HEAD_DIM = 128
SB_WIDTH = SB_HEADS * SB_HEAD_DIM
SB_SCALE = SB_HEAD_DIM ** -0.5
MLA_HEADS = 16
MLA_NOPE_DIM = 128
MLA_ROPE_DIM = 64
MLA_V_DIM = 128
MLA_KV_RANK = 512
MLA_QK_DIM = MLA_NOPE_DIM + MLA_ROPE_DIM
MLA_SCALE = MLA_QK_DIM ** -0.5
ROPE_THETA = 10000.0
IN_SPLITS = (SB_WIDTH, SB_WIDTH, SB_WIDTH, MLA_HEADS * MLA_QK_DIM, MLA_KV_RANK, MLA_ROPE_DIM, D_MODEL, D_MODEL)
IN_WIDTH = SB_WIDTH * 3 + MLA_HEADS * MLA_QK_DIM + MLA_KV_RANK + MLA_ROPE_DIM + 2 * D_MODEL
N_GROUPS = 4
EXPERTS_PER_GROUP = 8
N_EXPERTS = N_GROUPS * EXPERTS_PER_GROUP
TOP_K = 2
EXPERT_FF = 1024
ROUTE_BLOCK = 256
RMS_EPS = 1e-6

kernel_name = "hybrid_stickbreak_mla_hmoe_block"


def rms_norm(x, g):
    xf = x.astype(jnp.float32)
    y = xf * lax.rsqrt(jnp.mean(xf * xf, axis=-1, keepdims=True) + RMS_EPS)
    return (y * g.astype(jnp.float32)).astype(x.dtype)


def rope(x, pos):
    half = MLA_ROPE_DIM // 2
    inv = ROPE_THETA ** (-jnp.arange(half, dtype=jnp.float32) / half)
    ang = pos.astype(jnp.float32)[:, None] * inv[None, :]
    cos = jnp.cos(ang).astype(x.dtype)
    sin = jnp.sin(ang).astype(x.dtype)
    x1, x2 = x[..., :half], x[..., half:]
    return jnp.concatenate([x1 * cos - x2 * sin, x1 * sin + x2 * cos], axis=-1)


def to_blocks(t):
    b, h, lp, d = t.shape
    return t.reshape(b, h, lp // Q_BLOCK, Q_BLOCK, d).transpose(2, 0, 1, 3, 4)


def from_blocks(t):
    nb, b, h, qb, d = t.shape
    return t.transpose(1, 2, 0, 3, 4).reshape(b, h, nb * qb, d)


def stick_breaking_attention(q, k, v, idx):
    nb = q.shape[2] // Q_BLOCK
    def one(args):
        qb, qpos = args
        z = jnp.einsum('bhqd,bhkd->bhqk', qb, k).astype(jnp.float32) * SB_SCALE
        causal = idx[None, :] < qpos[:, None]
        log_1m = jnp.where(causal, jax.nn.log_sigmoid(-z), 0.0)
        log_w = jax.nn.log_sigmoid(z) + lax.cumsum(log_1m, axis=3, reverse=True) - log_1m
        a = jnp.where(causal, jnp.exp(log_w), 0.0)
        return jnp.einsum('bhqk,bhkd->bhqd', a.astype(v.dtype), v)
    o = lax.map(one, (to_blocks(q), idx.reshape(nb, Q_BLOCK)))
    return from_blocks(o)


def latent_attention(q_nope, q_rope, k_nope, k_rope, v, chunk):
    nb = q_nope.shape[2] // Q_BLOCK
    def one(args):
        qn, qr, qc = args
        s = (jnp.einsum('bhqd,bhkd->bhqk', qn, k_nope)
             + jnp.einsum('bhqr,bkr->bhqk', qr, k_rope)).astype(jnp.float32) * MLA_SCALE
        mask = chunk[None, :] <= qc[:, None]
        p = jax.nn.softmax(jnp.where(mask, s, -jnp.inf), axis=-1)
        return jnp.einsum('bhqk,bhkd->bhqd', p.astype(v.dtype), v)
    o = lax.map(one, (to_blocks(q_nope), to_blocks(q_rope), chunk.reshape(nb, Q_BLOCK)))
    return from_blocks(o)


def hybrid_mixer(hn, w_in, b_gate, kv_norm_g, w_uk, w_uv, w_proj_a, w_proj_b, w_out):
    b, l, _ = hn.shape
    lp = -(-l // Q_BLOCK) * Q_BLOCK
    hp = jnp.pad(hn, ((0, 0), (0, lp - l), (0, 0)))
    idx = jnp.arange(lp)
    chunk = jnp.where(idx < N_META, 0, 1 + (idx - N_META) // CHUNK)
    offs = [sum(IN_SPLITS[:i + 1]) for i in range(len(IN_SPLITS) - 1)]
    sb_q, sb_k, sb_v, mla_q, c_kv, k_rope, g_a, g_b = jnp.split(hp @ w_in, offs, axis=-1)

    def heads(t, h):
        return t.reshape(b, lp, h, -1).transpose(0, 2, 1, 3)

    o_a = stick_breaking_attention(heads(sb_q, SB_HEADS), heads(sb_k, SB_HEADS), heads(sb_v, SB_HEADS), idx)

    q = heads(mla_q, MLA_HEADS)
    q_nope, q_rope = q[..., :MLA_NOPE_DIM], rope(q[..., MLA_NOPE_DIM:], idx)
    k_rope = rope(k_rope, idx)
    c = rms_norm(c_kv, kv_norm_g)
    k_nope = heads(c @ w_uk, MLA_HEADS)
    v = heads(c @ w_uv, MLA_HEADS)
    o_b = latent_attention(q_nope, q_rope, k_nope, k_rope, v, chunk)

    def flat(o):
        return o.transpose(0, 2, 1, 3).reshape(b, lp, -1)[:, :l]

    bg_a, bg_b = jnp.split(b_gate, 2)
    y = (jax.nn.sigmoid(g_a[:, :l] + bg_a) * (flat(o_a) @ w_proj_a)
         + jax.nn.sigmoid(g_b[:, :l] + bg_b) * (flat(o_b) @ w_proj_b))
    return y @ w_out


def hierarchical_moe(hn, w_route_group, b_route_group, w_route_expert, b_route_expert, w1, w3, w2):
    n, d = hn.shape
    g_logits = (hn @ w_route_group).astype(jnp.float32) + b_route_group.astype(jnp.float32)
    g_prob = jax.nn.softmax(g_logits, axis=-1)
    g_sel = jnp.argmax(g_logits, axis=-1)
    p_g = jnp.take_along_axis(g_prob, g_sel[:, None], axis=-1)
    e_logits = ((hn @ w_route_expert).astype(jnp.float32)
                + b_route_expert.astype(jnp.float32)).reshape(n, N_GROUPS, EXPERTS_PER_GROUP)
    e_in_group = jnp.take_along_axis(e_logits, g_sel[:, None, None], axis=1)[:, 0]
    top_v, top_i = lax.top_k(jax.nn.softmax(e_in_group, axis=-1), TOP_K)
    weights = p_g * top_v / jnp.sum(top_v, axis=-1, keepdims=True)
    expert_id = g_sel[:, None] * EXPERTS_PER_GROUP + top_i

    m = n * TOP_K
    flat_e = expert_id.reshape(m)
    flat_w = weights.reshape(m).astype(hn.dtype)
    flat_t = jnp.arange(m) // TOP_K
    order = jnp.argsort(flat_e)
    se, st, sw = flat_e[order], flat_t[order], flat_w[order]
    counts = jnp.bincount(flat_e, length=N_EXPERTS)
    padded = (counts + ROUTE_BLOCK - 1) // ROUTE_BLOCK * ROUTE_BLOCK
    starts = jnp.cumsum(counts) - counts
    pends = jnp.cumsum(padded)
    pstarts = pends - padded
    dest = pstarts[se] + jnp.arange(m) - starts[se]
    n_blocks = (m + N_EXPERTS * (ROUTE_BLOCK - 1) + ROUTE_BLOCK - 1) // ROUTE_BLOCK
    p_rows = n_blocks * ROUTE_BLOCK
    buf_t = jnp.zeros((p_rows,), jnp.int32).at[dest].set(st.astype(jnp.int32))
    buf_w = jnp.zeros((p_rows,), hn.dtype).at[dest].set(sw)
    block_e = jnp.clip(jnp.searchsorted(pends, jnp.arange(n_blocks) * ROUTE_BLOCK, side='right'), 0, N_EXPERTS - 1)

    def expert_block(args):
        tok, w, e = args
        xb = hn[tok]
        hid = jax.nn.silu(xb @ w1[e]) * (xb @ w3[e])
        return (hid @ w2[e]) * w[:, None]
    y = lax.map(expert_block, (buf_t.reshape(n_blocks, ROUTE_BLOCK), buf_w.reshape(n_blocks, ROUTE_BLOCK), block_e))
    return jnp.zeros_like(hn).at[buf_t].add(y.reshape(p_rows, d))


def setup_inputs(seed: int = 0) -> dict:
    key = jax.random.key(seed)
    ks = jax.random.split(key, 20)
    f32 = jnp.float32

    def nrm(k, shape, fan_in):
        return jax.random.normal(k, shape, f32) * (fan_in ** -0.5)

    def gain(k, shape):
        return 1.0 + 0.02 * jax.random.normal(k, shape, f32)

    return {
        "x": jax.random.normal(ks[0], (BATCH, SEQ, D_MODEL), f32),
        "meta_tokens": jax.random.normal(ks[1], (N_META, D_MODEL), f32),
        "norm1_g": gain(ks[2], (DEPTH, D_MODEL)),
        "w_in": nrm(ks[3], (DEPTH, D_MODEL, IN_WIDTH), D_MODEL),
        "b_gate": 0.02 * jax.random.normal(ks[4], (DEPTH, 2 * D_MODEL), f32),
        "kv_norm_g": gain(ks[5], (DEPTH, MLA_KV_RANK)),
        "w_uk": nrm(ks[6], (DEPTH, MLA_KV_RANK, MLA_HEADS * MLA_NOPE_DIM), MLA_KV_RANK),
        "w_uv": nrm(ks[7], (DEPTH, MLA_KV_RANK, MLA_HEADS * MLA_V_DIM), MLA_KV_RANK),
        "w_proj_a": nrm(ks[8], (DEPTH, SB_WIDTH, D_MODEL), SB_WIDTH),
        "w_proj_b": nrm(ks[9], (DEPTH, MLA_HEADS * MLA_V_DIM, D_MODEL), MLA_HEADS * MLA_V_DIM),
        "w_out": nrm(ks[10], (DEPTH, D_MODEL, D_MODEL), D_MODEL),
        "norm2_g": gain(ks[11], (DEPTH, D_MODEL)),
        "w_route_group": nrm(ks[12], (DEPTH, D_MODEL, N_GROUPS), D_MODEL),
        "b_route_group": 0.01 * jax.random.normal(ks[13], (DEPTH, N_GROUPS), f32),
        "w_route_expert": nrm(ks[14], (DEPTH, D_MODEL, N_EXPERTS), D_MODEL),
        "b_route_expert": 0.01 * jax.random.normal(ks[15], (DEPTH, N_EXPERTS), f32),
        "w1": nrm(ks[16], (DEPTH, N_EXPERTS, D_MODEL, EXPERT_FF), D_MODEL),
        "w3": nrm(ks[17], (DEPTH, N_EXPERTS, D_MODEL, EXPERT_FF), D_MODEL),
        "w2": nrm(ks[18], (DEPTH, N_EXPERTS, EXPERT_FF, D_MODEL), EXPERT_FF),
        "final_g": gain(ks[19], (D_MODEL,)),
    }


def reference(x, meta_tokens, norm1_g, w_in, b_gate, kv_norm_g, w_uk, w_uv, w_proj_a, w_proj_b, w_out,
              norm2_g, w_route_group, b_route_group, w_route_expert, b_route_expert, w1, w3, w2, final_g):
    b = x.shape[0]
    meta = jnp.broadcast_to(meta_tokens.astype(x.dtype)[None], (b, N_META, x.shape[-1]))
    h = jnp.concatenate([meta, x], axis=1)
    for l in range(DEPTH):
        h = h + hybrid_mixer(rms_norm(h, norm1_g[l]), w_in[l], b_gate[l], kv_norm_g[l], w_uk[l], w_uv[l],
                             w_proj_a[l], w_proj_b[l], w_out[l])
        hn = rms_norm(h, norm2_g[l])
        ffn = hierarchical_moe(hn.reshape(-1, hn.shape[-1]), w_route_group[l], b_route_group[l],
                               w_route_expert[l], b_route_expert[l], w1[l], w3[l], w2[l])
        h = h + ffn.reshape(h.shape)
    return rms_norm(h, final_g)[:, N_META:]
```

```python
import functools

import jax
import jax.numpy as jnp
from jax import lax
from jax.experimental import pallas as pl
from jax.experimental.pallas import tpu as pltpu

N_META = 16
CHUNK = 64
SB_HEADS = 8
SB_HEAD_DIM = 128
MLA_HEADS = 16
MLA_NOPE_DIM = 128
MLA_ROPE_DIM = 64
MLA_V_DIM = 128
MLA_KV_RANK = 512
ROPE_THETA = 10000.0
N_GROUPS = 4
EXPERTS_PER_GROUP = 8
TOP_K = 2
ROUTE_BLOCK = 256
RMS_EPS = 1e-6

TILE = 256
LANES = 128
MLA_QK_PAD = 256
EXPERT_LANE0 = 8
VMEM_LIMIT = 56 * 1024 * 1024
NEG_BIG = -1e30

F32 = jnp.float32
BF16 = jnp.bfloat16
LOG2E = 1.4426950408889634


def _cparams(sem):
    return pltpu.CompilerParams(dimension_semantics=sem, vmem_limit_bytes=VMEM_LIMIT)


def _rms(v, g):
    ms = jnp.mean(v * v, axis=-1, keepdims=True)
    return v * lax.rsqrt(ms + RMS_EPS) * g


def _row_tile(lp):
    for t in (768, 512, 256):
        if lp % t == 0:
            return t
    raise ValueError(lp)


def _norm1_kernel(x_ref, head_ref, g_ref, o_ref):
    i = pl.program_id(1)

    @pl.when(i == 0)
    def _():
        o_ref[0] = _rms(head_ref[...], g_ref[...]).astype(BF16)

    @pl.when(i > 0)
    def _():
        o_ref[0] = _rms(x_ref[0], g_ref[...]).astype(BF16)


def _norm1(x, head, g):
    b, s, d = x.shape
    nt = s // TILE + 1
    return pl.pallas_call(
        _norm1_kernel,
        out_shape=jax.ShapeDtypeStruct((b, nt * TILE, d), BF16),
        grid=(b, nt),
        in_specs=[
            pl.BlockSpec((1, TILE, d), lambda bi, i: (bi, jnp.maximum(i - 1, 0), 0)),
            pl.BlockSpec((TILE, d), lambda bi, i: (0, 0)),
            pl.BlockSpec((1, d), lambda bi, i: (0, 0)),
        ],
        out_specs=pl.BlockSpec((1, TILE, d), lambda bi, i: (bi, i, 0)),
        compiler_params=_cparams(("parallel", "parallel")),
        name="norm1",
    )(x, head, g.reshape(1, d))


def _rope_rows(r, cos_t, sin_a, sin_b):
    return r * cos_t + pltpu.roll(r, 96, 1) * sin_a + pltpu.roll(r, 32, 1) * sin_b


def _proj_scale_kernel(x_ref, w_ref, s_ref, o_ref):
    acc = jnp.dot(x_ref[...], w_ref[...], preferred_element_type=F32)
    o_ref[...] = (acc * s_ref[...]).astype(o_ref.dtype)


def _proj_gate_kernel(x_ref, w_ref, b_ref, o_ref):
    acc = jnp.dot(x_ref[...], w_ref[...], preferred_element_type=F32) + b_ref[...]
    o_ref[...] = 1.0 / (1.0 + jnp.exp(-acc))


def _proj_mlaq_kernel(x_ref, w_ref, cos_ref, sa_ref, sb_ref, o_ref, *, scale):
    acc = jnp.dot(x_ref[...], w_ref[...], preferred_element_type=F32)
    cos_t, sin_a, sin_b = cos_ref[...], sa_ref[...], sb_ref[...]
    for hh in range(acc.shape[1] // MLA_QK_PAD):
        c0 = hh * MLA_QK_PAD
        nope = acc[:, c0:c0 + LANES] * scale
        rope = _rope_rows(acc[:, c0 + LANES:c0 + 2 * LANES], cos_t, sin_a, sin_b) * scale
        o_ref[:, c0:c0 + LANES] = nope.astype(o_ref.dtype)
        o_ref[:, c0 + LANES:c0 + 2 * LANES] = rope.astype(o_ref.dtype)


def _proj(kernel, hn2d, w, extras, extra_specs, out_dtype, tm, tn):
    m, k = hn2d.shape
    n = w.shape[1]
    return pl.pallas_call(
        kernel,
        out_shape=jax.ShapeDtypeStruct((m, n), out_dtype),
        grid=(n // tn, m // tm),
        in_specs=[
            pl.BlockSpec((tm, k), lambda j, i: (i, 0)),
            pl.BlockSpec((k, tn), lambda j, i: (0, j)),
        ] + extra_specs,
        out_specs=pl.BlockSpec((tm, tn), lambda j, i: (i, j)),
        compiler_params=_cparams(("parallel", "parallel")),
        name=getattr(kernel, "__name__", None) or kernel.func.__name__,
    )(hn2d, w, *extras)


def _kvup_kernel(c_ref, g_ref, wk_ref, wv_ref, cos_ref, sa_ref, sb_ref, k_ref, v_ref, *, rank):
    ckr = c_ref[...]
    cn = _rms(ckr[:, :rank], g_ref[...]).astype(BF16)
    kn = jnp.dot(cn, wk_ref[...], preferred_element_type=F32)
    vv = jnp.dot(cn, wv_ref[...], preferred_element_type=F32)
    rope = _rope_rows(ckr[:, rank:rank + LANES], cos_ref[...], sa_ref[...], sb_ref[...]).astype(BF16)
    for h in range(kn.shape[1] // MLA_NOPE_DIM):
        k_ref[:, h * MLA_QK_PAD:h * MLA_QK_PAD + LANES] = kn[:, h * LANES:(h + 1) * LANES].astype(BF16)
        k_ref[:, h * MLA_QK_PAD + LANES:(h + 1) * MLA_QK_PAD] = rope
    v_ref[...] = vv.astype(BF16)


def _kvup(ckr, g, wk, wv, tables, tm, tiles_per_batch):
    m, cw = ckr.shape
    rank = wk.shape[0]
    nk = wk.shape[1]
    heads = nk // MLA_NOPE_DIM
    tab_spec = pl.BlockSpec((tm, LANES), lambda i: (i % tiles_per_batch, 0))
    return pl.pallas_call(
        functools.partial(_kvup_kernel, rank=rank),
        out_shape=(jax.ShapeDtypeStruct((m, heads * MLA_QK_PAD), BF16),
                   jax.ShapeDtypeStruct((m, wv.shape[1]), BF16)),
        grid=(m // tm,),
        in_specs=[
            pl.BlockSpec((tm, cw), lambda i: (i, 0)),
            pl.BlockSpec((1, rank), lambda i: (0, 0)),
            pl.BlockSpec(wk.shape, lambda i: (0, 0)),
            pl.BlockSpec(wv.shape, lambda i: (0, 0)),
            tab_spec, tab_spec, tab_spec,
        ],
        out_specs=(pl.BlockSpec((tm, heads * MLA_QK_PAD), lambda i: (i, 0)),
                   pl.BlockSpec((tm, wv.shape[1]), lambda i: (i, 0))),
        compiler_params=_cparams(("parallel",)),
        name="kvup",
    )(ckr, g.reshape(1, rank), wk, wv, *tables)


def _softplus(z):
    return jnp.maximum(z, 0.0) + jnp.log(1.0 + jnp.exp(-jnp.abs(z)))


def _sb_kernel(q_ref, k_ref, v_ref, o_ref, *, blk, first_valid):
    qi = pl.program_id(2) + 1
    q = q_ref[0]
    row = lax.broadcasted_iota(jnp.int32, (blk, blk), 0)
    col = lax.broadcasted_iota(jnp.int32, (blk, blk), 1)
    tri = (row >= col).astype(BF16)

    def block(j, carry, acc, mask):
        start = pl.multiple_of(j * blk, blk)
        kb = k_ref[0, pl.ds(start, blk), :]
        vb = v_ref[0, pl.ds(start, blk), :]
        z = lax.dot_general(q, kb, (((1,), (1,)), ((), ())), preferred_element_type=F32)
        sp = _softplus(z)
        if mask is not None:
            sp = jnp.where(mask, sp, 0.0)
        hi = sp.astype(BF16)
        lo = (sp - hi.astype(F32)).astype(BF16)
        cs = (jnp.dot(hi, tri, preferred_element_type=F32)
              + jnp.dot(lo, tri, preferred_element_type=F32))
        a = jnp.exp(z - cs - carry)
        if mask is not None:
            a = jnp.where(mask, a, 0.0)
        acc = acc + jnp.dot(a.astype(BF16), vb, preferred_element_type=F32)
        carry = carry + jnp.sum(sp, axis=1, keepdims=True)
        return carry, acc

    carry = jnp.zeros((blk, 1), F32)
    acc = jnp.zeros((blk, SB_HEAD_DIM), F32)
    carry, acc = block(qi, carry, acc, col < row)
    carry, acc = lax.fori_loop(
        0, qi - 1, lambda t, c: block(qi - 1 - t, c[0], c[1], None), (carry, acc))
    carry, acc = block(0, carry, acc, col >= first_valid)
    o_ref[0] = acc.astype(o_ref.dtype)


def _sb_attention(qkv, seq):
    b, lp, w3 = qkv.shape
    heads = w3 // (3 * SB_HEAD_DIM)
    nq = seq // TILE
    kv_spec = lambda off: pl.BlockSpec((1, lp, SB_HEAD_DIM), lambda bi, h, i: (bi, 0, off + h))
    return pl.pallas_call(
        functools.partial(_sb_kernel, blk=TILE, first_valid=TILE - N_META),
        out_shape=jax.ShapeDtypeStruct((b, seq, heads * SB_HEAD_DIM), BF16),
        grid=(b, heads, nq),
        in_specs=[
            pl.BlockSpec((1, TILE, SB_HEAD_DIM), lambda bi, h, i: (bi, i + 1, h)),
            kv_spec(heads),
            kv_spec(2 * heads),
        ],
        out_specs=pl.BlockSpec((1, TILE, SB_HEAD_DIM), lambda bi, h, i: (bi, i, h)),
        compiler_params=_cparams(("parallel", "parallel", "arbitrary")),
        name="sb_attention",
    )(qkv, qkv, qkv)


def _mla_kernel(q_ref, k_ref, v_ref, o_ref, *, blk, first_valid):
    qi = pl.program_id(2) + 1
    q = q_ref[0]
    row = lax.broadcasted_iota(jnp.int32, (blk, blk), 0)
    col = lax.broadcasted_iota(jnp.int32, (blk, blk), 1)

    def block(j, m, l, acc, mask):
        start = pl.multiple_of(j * blk, blk)
        kb = k_ref[0, pl.ds(start, blk), :]
        vb = v_ref[0, pl.ds(start, blk), :]
        s = lax.dot_general(q, kb, (((1,), (1,)), ((), ())), preferred_element_type=F32)
        if mask is not None:
            s = jnp.where(mask, s, NEG_BIG)
        m_new = jnp.maximum(m, jnp.max(s, axis=1, keepdims=True))
        alpha = jnp.exp2(m - m_new)
        p = jnp.exp2(s - m_new)
        l = alpha * l + jnp.sum(p, axis=1, keepdims=True)
        acc = alpha * acc + jnp.dot(p.astype(BF16), vb, preferred_element_type=F32)
        return m_new, l, acc

    m = jnp.full((blk, 1), NEG_BIG, F32)
    l = jnp.zeros((blk, 1), F32)
    acc = jnp.zeros((blk, MLA_V_DIM), F32)
    m, l, acc = block(0, m, l, acc, col >= first_valid)
    m, l, acc = lax.fori_loop(
        1, qi, lambda j, c: block(j, c[0], c[1], c[2], None), (m, l, acc))
    shift = CHUNK.bit_length() - 1
    chunk_mask = (col >> shift) <= (row >> shift)
    m, l, acc = block(qi, m, l, acc, chunk_mask)
    o_ref[0] = (acc / l).astype(o_ref.dtype)


def _mla_attention(q, k, v, seq):
    b, lp, _ = q.shape
    heads = v.shape[2] // MLA_V_DIM
    nq = seq // TILE
    return pl.pallas_call(
        functools.partial(_mla_kernel, blk=TILE, first_valid=TILE - N_META),
        out_shape=jax.ShapeDtypeStruct((b, seq, heads * MLA_V_DIM), BF16),
        grid=(b, heads, nq),
        in_specs=[
            pl.BlockSpec((1, TILE, MLA_QK_PAD), lambda bi, h, i: (bi, i + 1, h)),
            pl.BlockSpec((1, lp, MLA_QK_PAD), lambda bi, h, i: (bi, 0, h)),
            pl.BlockSpec((1, lp, MLA_V_DIM), lambda bi, h, i: (bi, 0, h)),
        ],
        out_specs=pl.BlockSpec((1, TILE, MLA_V_DIM), lambda bi, h, i: (bi, i, h)),
        compiler_params=_cparams(("parallel", "parallel", "arbitrary")),
        name="mla_attention",
    )(q, k, v)


def _merge_kernel(oa_ref, ob_ref, ga_ref, gb_ref, x_ref, wpa_ref, wpb_ref, wo_ref, o_ref):
    pa = jnp.dot(oa_ref[0], wpa_ref[...], preferred_element_type=F32)
    pb = jnp.dot(ob_ref[0], wpb_ref[...], preferred_element_type=F32)
    y = ga_ref[0] * pa + gb_ref[0] * pb
    o_ref[0] = x_ref[0] + jnp.dot(y.astype(BF16), wo_ref[...], preferred_element_type=F32)


def _merge(oa, ob, gates, x, wpa, wpb, wo):
    b, s, d = x.shape
    nt = s // TILE
    resident = lambda w: pl.BlockSpec(w.shape, lambda bi, i: (0, 0), pipeline_mode=pl.Buffered(1))
    return pl.pallas_call(
        _merge_kernel,
        out_shape=jax.ShapeDtypeStruct((b, s, d), F32),
        grid=(b, nt),
        in_specs=[
            pl.BlockSpec((1, TILE, oa.shape[2]), lambda bi, i: (bi, i, 0)),
            pl.BlockSpec((1, TILE, ob.shape[2]), lambda bi, i: (bi, i, 0)),
            pl.BlockSpec((1, TILE, d), lambda bi, i: (bi, i + 1, 0)),
            pl.BlockSpec((1, TILE, d), lambda bi, i: (bi, i + 1, 1)),
            pl.BlockSpec((1, TILE, d), lambda bi, i: (bi, i, 0)),
            resident(wpa), resident(wpb), resident(wo),
        ],
        out_specs=pl.BlockSpec((1, TILE, d), lambda bi, i: (bi, i, 0)),
        compiler_params=_cparams(("parallel", "parallel")),
        name="merge",
    )(oa, ob, gates, gates, x, wpa, wpb, wo)


def _split3(a):
    hi = a.astype(BF16)
    r1 = a - hi.astype(F32)
    mid = r1.astype(BF16)
    lo = (r1 - mid.astype(F32)).astype(BF16)
    return hi, mid, lo


def _route_kernel(h_ref, g_ref, wr_ref, br_ref, hp_ref, r_ref, cnt_ref, carry_ref, *, tm, half):
    i = pl.program_id(0)

    @pl.when(i == 0)
    def _():
        carry_ref[...] = jnp.zeros_like(carry_ref)

    hn = _rms(h_ref[...], g_ref[...])

    lo_bits = pltpu.bitcast(hn[:, :half].astype(BF16).astype(F32), jnp.uint32)
    hi_bits = pltpu.bitcast(hn[:, half:].astype(BF16).astype(F32), jnp.uint32)
    hp_ref[...] = (hi_bits & jnp.uint32(0xFFFF0000)) | (lo_bits >> 16)

    a_hi, a_mid, a_lo = _split3(hn)
    w_hi, w_mid, w_lo = wr_ref[0], wr_ref[1], wr_ref[2]
    dot = lambda a, w: jnp.dot(a, w, preferred_element_type=F32)
    lg = (dot(a_lo, w_hi) + dot(a_hi, w_lo) + dot(a_mid, w_mid)
          + dot(a_mid, w_hi) + dot(a_hi, w_mid) + dot(a_hi, w_hi)) + br_ref[...]

    lane = lax.broadcasted_iota(jnp.int32, lg.shape, 1)
    rmax = lambda v: jnp.max(v, axis=1, keepdims=True)
    rmin = lambda v: jnp.min(v, axis=1, keepdims=True)
    rsum = lambda v: jnp.sum(v, axis=1, keepdims=True)

    gmask = lane < N_GROUPS
    gl = jnp.where(gmask, lg, -jnp.inf)
    gmax = rmax(gl)
    gsel = rmin(jnp.where(gl == gmax, lane, LANES))
    p_g = 1.0 / rsum(jnp.where(gmask, jnp.exp(lg - gmax), 0.0))

    e_lo = EXPERT_LANE0 + gsel * EXPERTS_PER_GROUP
    emask = (lane >= e_lo) & (lane < e_lo + EXPERTS_PER_GROUP)
    emax = rmax(jnp.where(emask, lg, -jnp.inf))
    ex = jnp.where(emask, jnp.exp(lg - emax), 0.0)
    prob = jnp.where(emask, ex / rsum(ex), -1.0)
    top1 = rmax(prob)
    i1 = rmin(jnp.where(prob == top1, lane, LANES))
    prob2 = jnp.where(lane == i1, -1.0, prob)
    top2 = rmax(prob2)
    i2 = rmin(jnp.where(prob2 == top2, lane, LANES))
    denom = top1 + top2
    w1 = p_g * top1 / denom
    w2 = p_g * top2 / denom

    sel = ((lane == i1) | (lane == i2))
    row = lax.broadcasted_iota(jnp.int32, (tm, tm), 0)
    col = lax.broadcasted_iota(jnp.int32, (tm, tm), 1)
    before = (col < row).astype(BF16)
    prefix = dot(before, jnp.where(sel, 1.0, 0.0).astype(BF16)) + carry_ref[...]
    rank1 = rsum(jnp.where(lane == i1, prefix, 0.0))
    rank2 = rsum(jnp.where(lane == i2, prefix, 0.0))
    carry_ref[...] = carry_ref[...] + jnp.sum(jnp.where(sel, 1.0, 0.0), axis=0, keepdims=True)
    cnt_ref[...] = carry_ref[...]

    e1 = (i1 - EXPERT_LANE0).astype(F32)
    e2 = (i2 - EXPERT_LANE0).astype(F32)
    out = jnp.zeros(lg.shape, F32)
    for k, val in enumerate((e1, e2, w1, w2, rank1, rank2)):
        out = jnp.where(lane == k, val, out)
    r_ref[...] = out


def _route(h1, g, wr3, br):
    n, d = h1.shape
    tm = TILE
    return pl.pallas_call(
        functools.partial(_route_kernel, tm=tm, half=d // 2),
        out_shape=(jax.ShapeDtypeStruct((n, d // 2), jnp.uint32),
                   jax.ShapeDtypeStruct((n, LANES), F32),
                   jax.ShapeDtypeStruct((1, LANES), F32)),
        grid=(n // tm,),
        in_specs=[
            pl.BlockSpec((tm, d), lambda i: (i, 0)),
            pl.BlockSpec((1, d), lambda i: (0, 0)),
            pl.BlockSpec(wr3.shape, lambda i: (0, 0, 0)),
            pl.BlockSpec((1, LANES), lambda i: (0, 0)),
        ],
        out_specs=(pl.BlockSpec((tm, d // 2), lambda i: (i, 0)),
                   pl.BlockSpec((tm, LANES), lambda i: (i, 0)),
                   pl.BlockSpec((1, LANES), lambda i: (0, 0))),
        scratch_shapes=[pltpu.VMEM((1, LANES), F32)],
        compiler_params=_cparams(("arbitrary",)),
        name="route",
    )(h1, g.reshape(1, d), wr3, br)


def _dispatch_kernel(d0_ref, d1_ref, src_ref, init_ref, dst_ref, sem, *, tm):
    del init_ref
    base = pl.program_id(0) * tm

    def copy(t, dest_ref, s):
        return pltpu.make_async_copy(src_ref.at[pl.ds(t, 1)], dst_ref.at[pl.ds(dest_ref[t], 1)], sem.at[s])

    def start(r, c):
        copy(base + r, d0_ref, 0).start()
        copy(base + r, d1_ref, 1).start()
        return c

    def wait(r, c):
        copy(base + r, d0_ref, 0).wait()
        copy(base + r, d1_ref, 1).wait()
        return c

    lax.fori_loop(0, tm, start, 0)
    lax.fori_loop(0, tm, wait, 0)


def _dispatch(dest0, dest1, hp, p_rows):
    n, w = hp.shape
    tm = TILE
    init = jnp.zeros((p_rows, w), hp.dtype)
    return pl.pallas_call(
        functools.partial(_dispatch_kernel, tm=tm),
        out_shape=jax.ShapeDtypeStruct((p_rows, w), hp.dtype),
        grid_spec=pltpu.PrefetchScalarGridSpec(
            num_scalar_prefetch=2,
            grid=(n // tm,),
            in_specs=[pl.BlockSpec(memory_space=pl.ANY), pl.BlockSpec(memory_space=pl.ANY)],
            out_specs=pl.BlockSpec(memory_space=pl.ANY),
            scratch_shapes=[pltpu.SemaphoreType.DMA((2,))],
        ),
        input_output_aliases={3: 0},
        compiler_params=_cparams(("arbitrary",)),
        name="dispatch",
    )(dest0, dest1, hp, init)


def _expert_kernel(be_ref, nu_ref, x_ref, w1_ref, w3_ref, w2_ref, y_ref):
    del be_ref

    @pl.when(pl.program_id(0) < nu_ref[0])
    def _():
        xw = x_ref[...]
        lo = pltpu.bitcast(xw << 16, F32).astype(BF16)
        hi = pltpu.bitcast(xw & jnp.uint32(0xFFFF0000), F32).astype(BF16)
        xb = jnp.concatenate([lo, hi], axis=1)
        a = jnp.dot(xb, w1_ref[0], preferred_element_type=F32)
        g = jnp.dot(xb, w3_ref[0], preferred_element_type=F32)
        hid = (a * (1.0 / (1.0 + jnp.exp(-a))) * g).astype(BF16)
        y_ref[...] = jnp.dot(hid, w2_ref[0], preferred_element_type=F32)

    @pl.when(pl.program_id(0) >= nu_ref[0])
    def _():
        y_ref[...] = jnp.zeros_like(y_ref)


def _experts(block_e, n_used, xs, w1, w3, w2):
    p_rows, half = xs.shape
    _, d, ff = w1.shape
    nb = p_rows // ROUTE_BLOCK
    last = lambda j, nu: jnp.minimum(j, nu[0] - 1)
    return pl.pallas_call(
        _expert_kernel,
        out_shape=jax.ShapeDtypeStruct((p_rows, d), F32),
        grid_spec=pltpu.PrefetchScalarGridSpec(
            num_scalar_prefetch=2,
            grid=(nb,),
            in_specs=[
                pl.BlockSpec((ROUTE_BLOCK, half), lambda j, be, nu: (last(j, nu), 0)),
                pl.BlockSpec((1, d, ff), lambda j, be, nu: (be[last(j, nu)], 0, 0)),
                pl.BlockSpec((1, d, ff), lambda j, be, nu: (be[last(j, nu)], 0, 0)),
                pl.BlockSpec((1, ff, d), lambda j, be, nu: (be[last(j, nu)], 0, 0)),
            ],
            out_specs=pl.BlockSpec((ROUTE_BLOCK, d), lambda j, be, nu: (j, 0)),
        ),
        compiler_params=_cparams(("arbitrary",)),
        name="experts",
    )(block_e, n_used, xs, w1, w3, w2)


def _combine_kernel(d0_ref, d1_ref, h_ref, r_ref, g_ref, y_ref, o_ref, ya, yb, sem, *, tm):
    base = pl.program_id(0) * tm

    def copy(r, dest_ref, buf, s):
        return pltpu.make_async_copy(y_ref.at[pl.ds(dest_ref[base + r], 1)], buf.at[pl.ds(r, 1)], sem.at[s])

    def start(r, c):
        copy(r, d0_ref, ya, 0).start()
        copy(r, d1_ref, yb, 1).start()
        return c

    def wait(r, c):
        copy(r, d0_ref, ya, 0).wait()
        copy(r, d1_ref, yb, 1).wait()
        return c

    lax.fori_loop(0, tm, start, 0)
    lax.fori_loop(0, tm, wait, 0)

    route = r_ref[...]
    lane = lax.broadcasted_iota(jnp.int32, route.shape, 1)
    w1 = jnp.sum(jnp.where(lane == 2, route, 0.0), axis=1, keepdims=True)
    w2 = jnp.sum(jnp.where(lane == 3, route, 0.0), axis=1, keepdims=True)
    h2 = h_ref[...] + (ya[...] * w1 + yb[...] * w2)
    o_ref[...] = _rms(h2, g_ref[...])


def _combine(dest0, dest1, h1, route, g, y):
    n, d = h1.shape
    tm = TILE
    return pl.pallas_call(
        functools.partial(_combine_kernel, tm=tm),
        out_shape=jax.ShapeDtypeStruct((n, d), F32),
        grid_spec=pltpu.PrefetchScalarGridSpec(
            num_scalar_prefetch=2,
            grid=(n // tm,),
            in_specs=[
                pl.BlockSpec((tm, d), lambda i, a, b: (i, 0)),
                pl.BlockSpec((tm, LANES), lambda i, a, b: (i, 0)),
                pl.BlockSpec((1, d), lambda i, a, b: (0, 0)),
                pl.BlockSpec(memory_space=pl.ANY),
            ],
            out_specs=pl.BlockSpec((tm, d), lambda i, a, b: (i, 0)),
            scratch_shapes=[pltpu.VMEM((tm, d), F32), pltpu.VMEM((tm, d), F32),
                            pltpu.SemaphoreType.DMA((2,))],
        ),
        compiler_params=_cparams(("arbitrary",)),
        name="combine",
    )(dest0, dest1, h1, route, g.reshape(1, d), y)


def _rope_tables(lp):
    half = MLA_ROPE_DIM // 2
    inv = ROPE_THETA ** (-jnp.arange(half, dtype=F32) / half)
    pos = (jnp.arange(lp) - (TILE - N_META)).astype(F32)
    ang = pos[:, None] * inv[None, :]
    cos, sin = jnp.cos(ang), jnp.sin(ang)
    z32 = jnp.zeros((lp, half), F32)
    z64 = jnp.zeros((lp, LANES - MLA_ROPE_DIM), F32)
    return (jnp.concatenate([cos, cos, z64], axis=1),
            jnp.concatenate([-sin, z32, z64], axis=1),
            jnp.concatenate([z32, sin, z64], axis=1))


def kernel(x, meta_tokens, norm1_g, w_in, b_gate, kv_norm_g, w_uk, w_uv, w_proj_a, w_proj_b, w_out,
           norm2_g, w_route_group, b_route_group, w_route_expert, b_route_expert, w1, w3, w2, final_g):
    b, seq, d = x.shape
    assert seq % TILE == 0 and TILE % CHUNK == 0 and N_META <= TILE
    lp = TILE + seq
    n_tok = b * lp
    n_real = b * seq
    n_exp = N_GROUPS * EXPERTS_PER_GROUP
    sb_w = SB_HEADS * SB_HEAD_DIM
    qk_dim = MLA_NOPE_DIM + MLA_ROPE_DIM
    mq_w = MLA_HEADS * qk_dim
    sb_scale = SB_HEAD_DIM ** -0.5
    mla_scale = qk_dim ** -0.5

    wi = w_in[0]
    o_q = 3 * sb_w
    o_c = o_q + mq_w
    o_r = o_c + MLA_KV_RANK
    o_g = o_r + MLA_ROPE_DIM
    w_sb = wi[:, :o_q].astype(BF16)
    w_mq = jnp.pad(wi[:, o_q:o_c].reshape(d, MLA_HEADS, qk_dim),
                   ((0, 0), (0, 0), (0, MLA_QK_PAD - qk_dim))).reshape(d, MLA_HEADS * MLA_QK_PAD).astype(BF16)
    w_c = jnp.pad(wi[:, o_c:o_g], ((0, 0), (0, LANES - MLA_ROPE_DIM))).astype(BF16)
    w_g = wi[:, o_g:].astype(BF16)
    sb_colscale = jnp.concatenate([jnp.full((1, sb_w), sb_scale, F32), jnp.ones((1, 2 * sb_w), F32)], axis=1)
    wr = jnp.zeros((d, LANES), F32)
    wr = wr.at[:, :N_GROUPS].set(w_route_group[0]).at[:, EXPERT_LANE0:EXPERT_LANE0 + n_exp].set(w_route_expert[0])
    wr_hi = wr.astype(BF16)
    wr_mid = (wr - wr_hi.astype(F32)).astype(BF16)
    wr_lo = (wr - wr_hi.astype(F32) - wr_mid.astype(F32)).astype(BF16)
    wr3 = jnp.stack([wr_hi, wr_mid, wr_lo])
    br = jnp.zeros((1, LANES), F32)
    br = br.at[0, :N_GROUPS].set(b_route_group[0]).at[0, EXPERT_LANE0:EXPERT_LANE0 + n_exp].set(b_route_expert[0])

    head = jnp.concatenate([jnp.zeros((TILE - N_META, d), F32), meta_tokens.astype(F32)], axis=0)
    hn = _norm1(x, head, norm1_g[0]).reshape(n_tok, d)
    tm = _row_tile(lp)
    tpb = lp // tm
    tables = _rope_tables(lp)
    tab_spec = pl.BlockSpec((tm, LANES), lambda j, i: (i % tpb, 0))
    row_spec = lambda tn: pl.BlockSpec((1, tn), lambda j, i: (0, j))

    sbqkv = _proj(_proj_scale_kernel, hn, w_sb, [sb_colscale], [row_spec(sb_w)], BF16, tm, sb_w)
    q_mla = _proj(functools.partial(_proj_mlaq_kernel, scale=mla_scale * LOG2E), hn, w_mq,
                  list(tables), [tab_spec] * 3, BF16, tm, min(4, MLA_HEADS) * MLA_QK_PAD)
    ckr = _proj(_proj_scale_kernel, hn, w_c, [jnp.ones((1, w_c.shape[1]), F32)],
                [row_spec(w_c.shape[1])], F32, tm, w_c.shape[1])
    gates = _proj(_proj_gate_kernel, hn, w_g, [b_gate[0].reshape(1, 2 * d)], [row_spec(d)], F32, tm, d)
    k_mla, v_mla = _kvup(ckr, kv_norm_g[0], w_uk[0].astype(BF16), w_uv[0].astype(BF16), tables, tm, tpb)

    o_a = _sb_attention(sbqkv.reshape(b, lp, 3 * sb_w), seq)
    o_b = _mla_attention(q_mla.reshape(b, lp, -1), k_mla.reshape(b, lp, -1), v_mla.reshape(b, lp, -1), seq)
    h1 = _merge(o_a, o_b, gates.reshape(b, lp, 2 * d), x,
                w_proj_a[0].astype(BF16), w_proj_b[0].astype(BF16), w_out[0].astype(BF16)).reshape(n_real, d)

    hp, route, counts = _route(h1, norm2_g[0], wr3, br)
    cnt = counts[0, EXPERT_LANE0:EXPERT_LANE0 + n_exp].astype(jnp.int32)
    padded = (cnt + ROUTE_BLOCK - 1) // ROUTE_BLOCK * ROUTE_BLOCK
    pends = jnp.cumsum(padded)
    pstarts = pends - padded
    n_blocks = (n_real * TOP_K + n_exp * (ROUTE_BLOCK - 1) + ROUTE_BLOCK - 1) // ROUTE_BLOCK
    n_used = (pends[-1] // ROUTE_BLOCK).astype(jnp.int32).reshape(1)
    blk_start = jnp.minimum(jnp.arange(n_blocks), n_used[0] - 1) * ROUTE_BLOCK
    block_e = jnp.clip(jnp.searchsorted(pends, blk_start, side='right'), 0, n_exp - 1).astype(jnp.int32)
    e1 = route[:, 0].astype(jnp.int32)
    e2 = route[:, 1].astype(jnp.int32)
    dest0 = pstarts[e1] + route[:, 4].astype(jnp.int32)
    dest1 = pstarts[e2] + route[:, 5].astype(jnp.int32)

    xs = _dispatch(dest0, dest1, hp, n_blocks * ROUTE_BLOCK)
    y = _experts(block_e, n_used, xs, w1[0].astype(BF16), w3[0].astype(BF16), w2[0].astype(BF16))
    out = _combine(dest0, dest1, h1, route, final_g, y)
    return out.reshape(b, seq, d)
```

```python
import functools

import jax
import jax.numpy as jnp
from jax import lax
from jax.experimental import pallas as pl
from jax.experimental.pallas import tpu as pltpu

N_META = 16
CHUNK = 64
SB_HEADS = 8
SB_HEAD_DIM = 128
MLA_HEADS = 16
MLA_NOPE_DIM = 128
MLA_ROPE_DIM = 64
MLA_V_DIM = 128
MLA_KV_RANK = 512
ROPE_THETA = 10000.0
N_GROUPS = 4
EXPERTS_PER_GROUP = 8
TOP_K = 2
ROUTE_BLOCK = 256
RMS_EPS = 1e-6

TILE = 256
LANES = 128
MLA_QK_PAD = 256
ATTN_Q_TILES = 4
EXPERT_LANE0 = 8
VMEM_LIMIT = 56 * 1024 * 1024
NEG_BIG = -1e30

F32 = jnp.float32
BF16 = jnp.bfloat16
LOG2E = 1.4426950408889634


def _cparams(sem):
    return pltpu.CompilerParams(dimension_semantics=sem, vmem_limit_bytes=VMEM_LIMIT)


def _rms(v, g):
    ms = jnp.mean(v * v, axis=-1, keepdims=True)
    return v * lax.rsqrt(ms + RMS_EPS) * g


def _row_tile(lp):
    for t in (768, 512, 256):
        if lp % t == 0:
            return t
    raise ValueError(lp)


def _norm1_kernel(x_ref, head_ref, g_ref, o_ref):
    i = pl.program_id(1)

    @pl.when(i == 0)
    def _():
        o_ref[0] = _rms(head_ref[...], g_ref[...]).astype(BF16)

    @pl.when(i > 0)
    def _():
        o_ref[0] = _rms(x_ref[0], g_ref[...]).astype(BF16)


def _norm1(x, head, g):
    b, s, d = x.shape
    nt = s // TILE + 1
    return pl.pallas_call(
        _norm1_kernel,
        out_shape=jax.ShapeDtypeStruct((b, nt * TILE, d), BF16),
        grid=(b, nt),
        in_specs=[
            pl.BlockSpec((1, TILE, d), lambda bi, i: (bi, jnp.maximum(i - 1, 0), 0)),
            pl.BlockSpec((TILE, d), lambda bi, i: (0, 0)),
            pl.BlockSpec((1, d), lambda bi, i: (0, 0)),
        ],
        out_specs=pl.BlockSpec((1, TILE, d), lambda bi, i: (bi, i, 0)),
        compiler_params=_cparams(("parallel", "parallel")),
        name="norm1",
    )(x, head, g.reshape(1, d))


def _rope_rows(r, cos_t, sin_a, sin_b):
    return r * cos_t + pltpu.roll(r, 96, 1) * sin_a + pltpu.roll(r, 32, 1) * sin_b


def _proj_scale_kernel(x_ref, w_ref, s_ref, o_ref):
    acc = jnp.dot(x_ref[...], w_ref[...], preferred_element_type=F32)
    o_ref[...] = (acc * s_ref[...]).astype(o_ref.dtype)


def _proj_gate_kernel(x_ref, w_ref, b_ref, o_ref):
    acc = jnp.dot(x_ref[...], w_ref[...], preferred_element_type=F32) + b_ref[...]
    o_ref[...] = 1.0 / (1.0 + jnp.exp(-acc))


def _proj_mlaq_kernel(x_ref, w_ref, cos_ref, sa_ref, sb_ref, o_ref, *, scale):
    acc = jnp.dot(x_ref[...], w_ref[...], preferred_element_type=F32)
    cos_t, sin_a, sin_b = cos_ref[...], sa_ref[...], sb_ref[...]
    for hh in range(acc.shape[1] // MLA_QK_PAD):
        c0 = hh * MLA_QK_PAD
        nope = acc[:, c0:c0 + LANES] * scale
        rope = _rope_rows(acc[:, c0 + LANES:c0 + 2 * LANES], cos_t, sin_a, sin_b) * scale
        o_ref[:, c0:c0 + LANES] = nope.astype(o_ref.dtype)
        o_ref[:, c0 + LANES:c0 + 2 * LANES] = rope.astype(o_ref.dtype)


def _proj(kernel, hn2d, w, extras, extra_specs, out_dtype, tm, tn):
    m, k = hn2d.shape
    n = w.shape[1]
    return pl.pallas_call(
        kernel,
        out_shape=jax.ShapeDtypeStruct((m, n), out_dtype),
        grid=(n // tn, m // tm),
        in_specs=[
            pl.BlockSpec((tm, k), lambda j, i: (i, 0)),
            pl.BlockSpec((k, tn), lambda j, i: (0, j)),
        ] + extra_specs,
        out_specs=pl.BlockSpec((tm, tn), lambda j, i: (i, j)),
        compiler_params=_cparams(("parallel", "parallel")),
        name=getattr(kernel, "__name__", None) or kernel.func.__name__,
    )(hn2d, w, *extras)


def _kvup_kernel(c_ref, g_ref, wk_ref, wv_ref, cos_ref, sa_ref, sb_ref, k_ref, v_ref, *, rank):
    ckr = c_ref[...]
    cn = _rms(ckr[:, :rank], g_ref[...]).astype(BF16)
    kn = jnp.dot(cn, wk_ref[...], preferred_element_type=F32)
    vv = jnp.dot(cn, wv_ref[...], preferred_element_type=F32)
    rope = _rope_rows(ckr[:, rank:rank + LANES], cos_ref[...], sa_ref[...], sb_ref[...]).astype(BF16)
    for h in range(kn.shape[1] // MLA_NOPE_DIM):
        k_ref[:, h * MLA_QK_PAD:h * MLA_QK_PAD + LANES] = kn[:, h * LANES:(h + 1) * LANES].astype(BF16)
        k_ref[:, h * MLA_QK_PAD + LANES:(h + 1) * MLA_QK_PAD] = rope
    v_ref[...] = vv.astype(BF16)


def _kvup(ckr, g, wk, wv, tables, tm, tiles_per_batch):
    m, cw = ckr.shape
    rank = wk.shape[0]
    nk = wk.shape[1]
    heads = nk // MLA_NOPE_DIM
    tab_spec = pl.BlockSpec((tm, LANES), lambda i: (i % tiles_per_batch, 0))
    return pl.pallas_call(
        functools.partial(_kvup_kernel, rank=rank),
        out_shape=(jax.ShapeDtypeStruct((m, heads * MLA_QK_PAD), BF16),
                   jax.ShapeDtypeStruct((m, wv.shape[1]), BF16)),
        grid=(m // tm,),
        in_specs=[
            pl.BlockSpec((tm, cw), lambda i: (i, 0)),
            pl.BlockSpec((1, rank), lambda i: (0, 0)),
            pl.BlockSpec(wk.shape, lambda i: (0, 0)),
            pl.BlockSpec(wv.shape, lambda i: (0, 0)),
            tab_spec, tab_spec, tab_spec,
        ],
        out_specs=(pl.BlockSpec((tm, heads * MLA_QK_PAD), lambda i: (i, 0)),
                   pl.BlockSpec((tm, wv.shape[1]), lambda i: (i, 0))),
        compiler_params=_cparams(("parallel",)),
        name="kvup",
    )(ckr, g.reshape(1, rank), wk, wv, *tables)


def _softplus(z):
    return jnp.maximum(z, 0.0) + jnp.log(1.0 + jnp.exp(-jnp.abs(z)))


def _sb_kernel(*refs, nsub, first_valid):
    q_refs = refs[:nsub]
    k_ref, v_ref, o_ref, carry_ref, acc_ref = refs[nsub:]
    qs = pl.program_id(2)
    first_tile = 1 + nsub * qs
    carry_ref[...] = jnp.zeros_like(carry_ref)
    acc_ref[...] = jnp.zeros_like(acc_ref)
    trow = lax.broadcasted_iota(jnp.int32, (TILE, TILE), 0)
    tcol = lax.broadcasted_iota(jnp.int32, (TILE, TILE), 1)
    tri = jnp.where(trow >= tcol, 1.0, 0.0).astype(BF16)

    def update(c0, kstart, mask):
        r0 = c0 * TILE
        q = jnp.concatenate([q_refs[c][0] for c in range(c0, nsub)], axis=0)
        kb = k_ref[0, pl.ds(kstart, TILE), :]
        vb = v_ref[0, pl.ds(kstart, TILE), :]
        z = lax.dot_general(q, kb, (((1,), (1,)), ((), ())), preferred_element_type=F32)
        sp = _softplus(z)
        if mask is not None:
            sp = jnp.where(mask, sp, 0.0)
        hi = sp.astype(BF16)
        lo = (sp - hi.astype(F32)).astype(BF16)
        cs = (jnp.dot(hi, tri, preferred_element_type=F32)
              + jnp.dot(lo, tri, preferred_element_type=F32))
        carry = carry_ref[r0:, :]
        a = jnp.exp(z - cs - carry)
        if mask is not None:
            a = jnp.where(mask, a, 0.0)
        acc_ref[r0:, :] = acc_ref[r0:, :] + jnp.dot(a.astype(BF16), vb, preferred_element_type=F32)
        carry_ref[r0:, :] = carry + jnp.sum(sp, axis=1, keepdims=True)

    for c in reversed(range(nsub)):
        rows = (nsub - c) * TILE
        rowi = lax.broadcasted_iota(jnp.int32, (rows, TILE), 0)
        coli = lax.broadcasted_iota(jnp.int32, (rows, TILE), 1)
        update(c, pl.multiple_of((first_tile + c) * TILE, TILE), (rowi >= TILE) | (coli < rowi))

    def body(t, carry):
        j = first_tile - 1 - 2 * t
        update(0, pl.multiple_of(j * TILE, TILE), None)
        update(0, pl.multiple_of((j - 1) * TILE, TILE), None)
        return carry

    lax.fori_loop(0, (nsub * qs) // 2, body, 0)
    col0 = lax.broadcasted_iota(jnp.int32, (nsub * TILE, TILE), 1)
    update(0, 0, col0 >= first_valid)
    o_ref[0] = acc_ref[...].astype(o_ref.dtype)


def _q_specs(nsub, width):
    return [pl.BlockSpec((1, TILE, width), functools.partial(lambda bi, h, i, c: (bi, nsub * i + 1 + c, h), c=c))
            for c in range(nsub)]


def _sb_attention(qkv, seq):
    b, lp, w3 = qkv.shape
    heads = w3 // (3 * SB_HEAD_DIM)
    nsub = ATTN_Q_TILES
    rows = nsub * TILE
    kv_spec = lambda off: pl.BlockSpec((1, lp, SB_HEAD_DIM), lambda bi, h, i: (bi, 0, off + h))
    return pl.pallas_call(
        functools.partial(_sb_kernel, nsub=nsub, first_valid=TILE - N_META),
        out_shape=jax.ShapeDtypeStruct((b, seq, heads * SB_HEAD_DIM), BF16),
        grid=(b, heads, seq // rows),
        in_specs=_q_specs(nsub, SB_HEAD_DIM) + [kv_spec(heads), kv_spec(2 * heads)],
        out_specs=pl.BlockSpec((1, rows, SB_HEAD_DIM), lambda bi, h, i: (bi, i, h)),
        scratch_shapes=[pltpu.VMEM((rows, 1), F32), pltpu.VMEM((rows, SB_HEAD_DIM), F32)],
        compiler_params=_cparams(("parallel", "parallel", "arbitrary")),
        name="sb_attention",
    )(*([qkv] * (nsub + 2)))


def _mla_kernel(*refs, nsub, first_valid):
    q_refs = refs[:nsub]
    k_ref, v_ref, o_ref, m_ref, l_ref, acc_ref = refs[nsub:]
    qs = pl.program_id(2)
    first_tile = 1 + nsub * qs
    m_ref[...] = jnp.full(m_ref.shape, NEG_BIG, F32)
    l_ref[...] = jnp.zeros_like(l_ref)
    acc_ref[...] = jnp.zeros_like(acc_ref)
    shift = CHUNK.bit_length() - 1

    def update(c0, kstart, kw, mask):
        r0 = c0 * TILE
        q = jnp.concatenate([q_refs[c][0] for c in range(c0, nsub)], axis=0)
        kb = k_ref[0, pl.ds(kstart, kw), :]
        vb = v_ref[0, pl.ds(kstart, kw), :]
        s = lax.dot_general(q, kb, (((1,), (1,)), ((), ())), preferred_element_type=F32)
        if mask is not None:
            s = jnp.where(mask, s, NEG_BIG)
        m_old = m_ref[r0:, :]
        m_new = jnp.maximum(m_old, jnp.max(s, axis=1, keepdims=True))
        alpha = jnp.exp2(m_old - m_new)
        p = jnp.exp2(s - m_new)
        l_ref[r0:, :] = alpha * l_ref[r0:, :] + jnp.sum(p, axis=1, keepdims=True)
        acc_ref[r0:, :] = alpha * acc_ref[r0:, :] + jnp.dot(p.astype(BF16), vb, preferred_element_type=F32)
        m_ref[r0:, :] = m_new

    col0 = lax.broadcasted_iota(jnp.int32, (nsub * TILE, TILE), 1)
    update(0, 0, TILE, col0 >= first_valid)

    def body(t, carry):
        update(0, pl.multiple_of(TILE + t * 2 * TILE, TILE), 2 * TILE, None)
        return carry

    lax.fori_loop(0, (nsub * qs) // 2, body, 0)
    for c in range(nsub):
        rows = (nsub - c) * TILE
        rowi = lax.broadcasted_iota(jnp.int32, (rows, TILE), 0)
        coli = lax.broadcasted_iota(jnp.int32, (rows, TILE), 1)
        mask = (rowi >= TILE) | ((coli >> shift) <= (rowi >> shift))
        update(c, pl.multiple_of((first_tile + c) * TILE, TILE), TILE, mask)
    o_ref[0] = (acc_ref[...] / l_ref[...]).astype(o_ref.dtype)


def _mla_attention(q, k, v, seq):
    b, lp, _ = q.shape
    heads = v.shape[2] // MLA_V_DIM
    nsub = ATTN_Q_TILES
    rows = nsub * TILE
    return pl.pallas_call(
        functools.partial(_mla_kernel, nsub=nsub, first_valid=TILE - N_META),
        out_shape=jax.ShapeDtypeStruct((b, seq, heads * MLA_V_DIM), BF16),
        grid=(b, heads, seq // rows),
        in_specs=_q_specs(nsub, MLA_QK_PAD) + [
            pl.BlockSpec((1, lp, MLA_QK_PAD), lambda bi, h, i: (bi, 0, h)),
            pl.BlockSpec((1, lp, MLA_V_DIM), lambda bi, h, i: (bi, 0, h)),
        ],
        out_specs=pl.BlockSpec((1, rows, MLA_V_DIM), lambda bi, h, i: (bi, i, h)),
        scratch_shapes=[pltpu.VMEM((rows, 1), F32), pltpu.VMEM((rows, 1), F32),
                        pltpu.VMEM((rows, MLA_V_DIM), F32)],
        compiler_params=_cparams(("parallel", "parallel", "arbitrary")),
        name="mla_attention",
    )(*([q] * nsub), k, v)


def _merge_kernel(oa_ref, ob_ref, ga_ref, gb_ref, x_ref, wpa_ref, wpb_ref, wo_ref, o_ref):
    pa = jnp.dot(oa_ref[0], wpa_ref[...], preferred_element_type=F32)
    pb = jnp.dot(ob_ref[0], wpb_ref[...], preferred_element_type=F32)
    y = ga_ref[0] * pa + gb_ref[0] * pb
    o_ref[0] = x_ref[0] + jnp.dot(y.astype(BF16), wo_ref[...], preferred_element_type=F32)


def _merge(oa, ob, gates, x, wpa, wpb, wo):
    b, s, d = x.shape
    nt = s // TILE
    resident = lambda w: pl.BlockSpec(w.shape, lambda bi, i: (0, 0), pipeline_mode=pl.Buffered(1))
    return pl.pallas_call(
        _merge_kernel,
        out_shape=jax.ShapeDtypeStruct((b, s, d), F32),
        grid=(b, nt),
        in_specs=[
            pl.BlockSpec((1, TILE, oa.shape[2]), lambda bi, i: (bi, i, 0)),
            pl.BlockSpec((1, TILE, ob.shape[2]), lambda bi, i: (bi, i, 0)),
            pl.BlockSpec((1, TILE, d), lambda bi, i: (bi, i + 1, 0)),
            pl.BlockSpec((1, TILE, d), lambda bi, i: (bi, i + 1, 1)),
            pl.BlockSpec((1, TILE, d), lambda bi, i: (bi, i, 0)),
            resident(wpa), resident(wpb), resident(wo),
        ],
        out_specs=pl.BlockSpec((1, TILE, d), lambda bi, i: (bi, i, 0)),
        compiler_params=_cparams(("parallel", "parallel")),
        name="merge",
    )(oa, ob, gates, gates, x, wpa, wpb, wo)


def _split3(a):
    hi = a.astype(BF16)
    r1 = a - hi.astype(F32)
    mid = r1.astype(BF16)
    lo = (r1 - mid.astype(F32)).astype(BF16)
    return hi, mid, lo


def _route_kernel(h_ref, g_ref, wr_ref, br_ref, hp_ref, r_ref, cnt_ref, carry_ref, *, tm, half):
    i = pl.program_id(0)

    @pl.when(i == 0)
    def _():
        carry_ref[...] = jnp.zeros_like(carry_ref)

    hn = _rms(h_ref[...], g_ref[...])

    lo_bits = pltpu.bitcast(hn[:, :half].astype(BF16).astype(F32), jnp.uint32)
    hi_bits = pltpu.bitcast(hn[:, half:].astype(BF16).astype(F32), jnp.uint32)
    hp_ref[...] = (hi_bits & jnp.uint32(0xFFFF0000)) | (lo_bits >> 16)

    a_hi, a_mid, a_lo = _split3(hn)
    w_hi, w_mid, w_lo = wr_ref[0], wr_ref[1], wr_ref[2]
    dot = lambda a, w: jnp.dot(a, w, preferred_element_type=F32)
    lg = (dot(a_lo, w_hi) + dot(a_hi, w_lo) + dot(a_mid, w_mid)
          + dot(a_mid, w_hi) + dot(a_hi, w_mid) + dot(a_hi, w_hi)) + br_ref[...]

    lane = lax.broadcasted_iota(jnp.int32, lg.shape, 1)
    rmax = lambda v: jnp.max(v, axis=1, keepdims=True)
    rmin = lambda v: jnp.min(v, axis=1, keepdims=True)
    rsum = lambda v: jnp.sum(v, axis=1, keepdims=True)

    gmask = lane < N_GROUPS
    gl = jnp.where(gmask, lg, -jnp.inf)
    gmax = rmax(gl)
    gsel = rmin(jnp.where(gl == gmax, lane, LANES))
    p_g = 1.0 / rsum(jnp.where(gmask, jnp.exp(lg - gmax), 0.0))

    e_lo = EXPERT_LANE0 + gsel * EXPERTS_PER_GROUP
    emask = (lane >= e_lo) & (lane < e_lo + EXPERTS_PER_GROUP)
    emax = rmax(jnp.where(emask, lg, -jnp.inf))
    ex = jnp.where(emask, jnp.exp(lg - emax), 0.0)
    prob = jnp.where(emask, ex / rsum(ex), -1.0)
    top1 = rmax(prob)
    i1 = rmin(jnp.where(prob == top1, lane, LANES))
    prob2 = jnp.where(lane == i1, -1.0, prob)
    top2 = rmax(prob2)
    i2 = rmin(jnp.where(prob2 == top2, lane, LANES))
    denom = top1 + top2
    w1 = p_g * top1 / denom
    w2 = p_g * top2 / denom

    sel = ((lane == i1) | (lane == i2))
    row = lax.broadcasted_iota(jnp.int32, (tm, tm), 0)
    col = lax.broadcasted_iota(jnp.int32, (tm, tm), 1)
    before = (col < row).astype(BF16)
    prefix = dot(before, jnp.where(sel, 1.0, 0.0).astype(BF16)) + carry_ref[...]
    rank1 = rsum(jnp.where(lane == i1, prefix, 0.0))
    rank2 = rsum(jnp.where(lane == i2, prefix, 0.0))
    carry_ref[...] = carry_ref[...] + jnp.sum(jnp.where(sel, 1.0, 0.0), axis=0, keepdims=True)
    cnt_ref[...] = carry_ref[...]

    e1 = (i1 - EXPERT_LANE0).astype(F32)
    e2 = (i2 - EXPERT_LANE0).astype(F32)
    out = jnp.zeros(lg.shape, F32)
    for k, val in enumerate((e1, e2, w1, w2, rank1, rank2)):
        out = jnp.where(lane == k, val, out)
    r_ref[...] = out


def _route(h1, g, wr3, br):
    n, d = h1.shape
    tm = TILE
    return pl.pallas_call(
        functools.partial(_route_kernel, tm=tm, half=d // 2),
        out_shape=(jax.ShapeDtypeStruct((n, d // 2), jnp.uint32),
                   jax.ShapeDtypeStruct((n, LANES), F32),
                   jax.ShapeDtypeStruct((1, LANES), F32)),
        grid=(n // tm,),
        in_specs=[
            pl.BlockSpec((tm, d), lambda i: (i, 0)),
            pl.BlockSpec((1, d), lambda i: (0, 0)),
            pl.BlockSpec(wr3.shape, lambda i: (0, 0, 0)),
            pl.BlockSpec((1, LANES), lambda i: (0, 0)),
        ],
        out_specs=(pl.BlockSpec((tm, d // 2), lambda i: (i, 0)),
                   pl.BlockSpec((tm, LANES), lambda i: (i, 0)),
                   pl.BlockSpec((1, LANES), lambda i: (0, 0))),
        scratch_shapes=[pltpu.VMEM((1, LANES), F32)],
        compiler_params=_cparams(("arbitrary",)),
        name="route",
    )(h1, g.reshape(1, d), wr3, br)


def _dispatch_kernel(d0_ref, d1_ref, src_ref, init_ref, dst_ref, sem, *, tm):
    del init_ref
    base = pl.program_id(0) * tm

    def copy(r, dest_ref, s):
        return pltpu.make_async_copy(src_ref.at[pl.ds(r, 1)], dst_ref.at[pl.ds(dest_ref[base + r], 1)], sem.at[s])

    def start(r, c):
        copy(r, d0_ref, 0).start()
        copy(r, d1_ref, 1).start()
        return c

    def wait(r, c):
        copy(r, d0_ref, 0).wait()
        copy(r, d1_ref, 1).wait()
        return c

    lax.fori_loop(0, tm, start, 0)
    lax.fori_loop(0, tm, wait, 0)


def _dispatch(dest0, dest1, hp, p_rows):
    n, w = hp.shape
    tm = TILE
    init = jnp.zeros((p_rows, w), hp.dtype)
    return pl.pallas_call(
        functools.partial(_dispatch_kernel, tm=tm),
        out_shape=jax.ShapeDtypeStruct((p_rows, w), hp.dtype),
        grid_spec=pltpu.PrefetchScalarGridSpec(
            num_scalar_prefetch=2,
            grid=(n // tm,),
            in_specs=[pl.BlockSpec((tm, w), lambda i, a, b: (i, 0)), pl.BlockSpec(memory_space=pl.ANY)],
            out_specs=pl.BlockSpec(memory_space=pl.ANY),
            scratch_shapes=[pltpu.SemaphoreType.DMA((2,))],
        ),
        input_output_aliases={3: 0},
        compiler_params=_cparams(("arbitrary",)),
        name="dispatch",
    )(dest0, dest1, hp, init)


def _expert_kernel(be_ref, nu_ref, x_ref, w1_ref, w3_ref, w2_ref, y_ref):
    del be_ref

    @pl.when(pl.program_id(0) < nu_ref[0])
    def _():
        xw = x_ref[...]
        lo = pltpu.bitcast(xw << 16, F32).astype(BF16)
        hi = pltpu.bitcast(xw & jnp.uint32(0xFFFF0000), F32).astype(BF16)
        xb = jnp.concatenate([lo, hi], axis=1)
        a = jnp.dot(xb, w1_ref[0], preferred_element_type=F32)
        g = jnp.dot(xb, w3_ref[0], preferred_element_type=F32)
        hid = (a * (1.0 / (1.0 + jnp.exp(-a))) * g).astype(BF16)
        y_ref[...] = jnp.dot(hid, w2_ref[0], preferred_element_type=F32)

    @pl.when(pl.program_id(0) >= nu_ref[0])
    def _():
        y_ref[...] = jnp.zeros_like(y_ref)


def _experts(block_e, n_used, xs, w1, w3, w2):
    p_rows, half = xs.shape
    _, d, ff = w1.shape
    nb = p_rows // ROUTE_BLOCK
    last = lambda j, nu: jnp.minimum(j, nu[0] - 1)
    return pl.pallas_call(
        _expert_kernel,
        out_shape=jax.ShapeDtypeStruct((p_rows, d), F32),
        grid_spec=pltpu.PrefetchScalarGridSpec(
            num_scalar_prefetch=2,
            grid=(nb,),
            in_specs=[
                pl.BlockSpec((ROUTE_BLOCK, half), lambda j, be, nu: (last(j, nu), 0)),
                pl.BlockSpec((1, d, ff), lambda j, be, nu: (be[last(j, nu)], 0, 0)),
                pl.BlockSpec((1, d, ff), lambda j, be, nu: (be[last(j, nu)], 0, 0)),
                pl.BlockSpec((1, ff, d), lambda j, be, nu: (be[last(j, nu)], 0, 0)),
            ],
            out_specs=pl.BlockSpec((ROUTE_BLOCK, d), lambda j, be, nu: (j, 0)),
        ),
        compiler_params=_cparams(("arbitrary",)),
        name="experts",
    )(block_e, n_used, xs, w1, w3, w2)


def _combine_kernel(d0_ref, d1_ref, h_ref, r_ref, g_ref, y_ref, o_ref, ya, yb, sem, *, tm):
    base = pl.program_id(0) * tm

    def copy(r, dest_ref, buf, s):
        return pltpu.make_async_copy(y_ref.at[pl.ds(dest_ref[base + r], 1)], buf.at[pl.ds(r, 1)], sem.at[s])

    def start(r, c):
        copy(r, d0_ref, ya, 0).start()
        copy(r, d1_ref, yb, 1).start()
        return c

    def wait(r, c):
        copy(r, d0_ref, ya, 0).wait()
        copy(r, d1_ref, yb, 1).wait()
        return c

    lax.fori_loop(0, tm, start, 0)
    lax.fori_loop(0, tm, wait, 0)

    route = r_ref[...]
    lane = lax.broadcasted_iota(jnp.int32, route.shape, 1)
    w1 = jnp.sum(jnp.where(lane == 2, route, 0.0), axis=1, keepdims=True)
    w2 = jnp.sum(jnp.where(lane == 3, route, 0.0), axis=1, keepdims=True)
    h2 = h_ref[...] + (ya[...] * w1 + yb[...] * w2)
    o_ref[...] = _rms(h2, g_ref[...])


def _combine(dest0, dest1, h1, route, g, y):
    n, d = h1.shape
    tm = TILE
    return pl.pallas_call(
        functools.partial(_combine_kernel, tm=tm),
        out_shape=jax.ShapeDtypeStruct((n, d), F32),
        grid_spec=pltpu.PrefetchScalarGridSpec(
            num_scalar_prefetch=2,
            grid=(n // tm,),
            in_specs=[
                pl.BlockSpec((tm, d), lambda i, a, b: (i, 0)),
                pl.BlockSpec((tm, LANES), lambda i, a, b: (i, 0)),
                pl.BlockSpec((1, d), lambda i, a, b: (0, 0)),
                pl.BlockSpec(memory_space=pl.ANY),
            ],
            out_specs=pl.BlockSpec((tm, d), lambda i, a, b: (i, 0)),
            scratch_shapes=[pltpu.VMEM((tm, d), F32), pltpu.VMEM((tm, d), F32),
                            pltpu.SemaphoreType.DMA((2,))],
        ),
        compiler_params=_cparams(("arbitrary",)),
        name="combine",
    )(dest0, dest1, h1, route, g.reshape(1, d), y)


def _rope_tables(lp):
    half = MLA_ROPE_DIM // 2
    inv = ROPE_THETA ** (-jnp.arange(half, dtype=F32) / half)
    pos = (jnp.arange(lp) - (TILE - N_META)).astype(F32)
    ang = pos[:, None] * inv[None, :]
    cos, sin = jnp.cos(ang), jnp.sin(ang)
    z32 = jnp.zeros((lp, half), F32)
    z64 = jnp.zeros((lp, LANES - MLA_ROPE_DIM), F32)
    return (jnp.concatenate([cos, cos, z64], axis=1),
            jnp.concatenate([-sin, z32, z64], axis=1),
            jnp.concatenate([z32, sin, z64], axis=1))


def kernel(x, meta_tokens, norm1_g, w_in, b_gate, kv_norm_g, w_uk, w_uv, w_proj_a, w_proj_b, w_out,
           norm2_g, w_route_group, b_route_group, w_route_expert, b_route_expert, w1, w3, w2, final_g):
    b, seq, d = x.shape
    assert seq % TILE == 0 and TILE % CHUNK == 0 and N_META <= TILE
    lp = TILE + seq
    n_tok = b * lp
    n_real = b * seq
    n_exp = N_GROUPS * EXPERTS_PER_GROUP
    sb_w = SB_HEADS * SB_HEAD_DIM
    qk_dim = MLA_NOPE_DIM + MLA_ROPE_DIM
    mq_w = MLA_HEADS * qk_dim
    sb_scale = SB_HEAD_DIM ** -0.5
    mla_scale = qk_dim ** -0.5

    wi = w_in[0]
    o_q = 3 * sb_w
    o_c = o_q + mq_w
    o_r = o_c + MLA_KV_RANK
    o_g = o_r + MLA_ROPE_DIM
    w_sb = wi[:, :o_q].astype(BF16)
    w_mq = jnp.pad(wi[:, o_q:o_c].reshape(d, MLA_HEADS, qk_dim),
                   ((0, 0), (0, 0), (0, MLA_QK_PAD - qk_dim))).reshape(d, MLA_HEADS * MLA_QK_PAD).astype(BF16)
    w_c = jnp.pad(wi[:, o_c:o_g], ((0, 0), (0, LANES - MLA_ROPE_DIM))).astype(BF16)
    w_g = wi[:, o_g:].astype(BF16)
    sb_colscale = jnp.concatenate([jnp.full((1, sb_w), sb_scale, F32), jnp.ones((1, 2 * sb_w), F32)], axis=1)
    wr = jnp.zeros((d, LANES), F32)
    wr = wr.at[:, :N_GROUPS].set(w_route_group[0]).at[:, EXPERT_LANE0:EXPERT_LANE0 + n_exp].set(w_route_expert[0])
    wr_hi = wr.astype(BF16)
    wr_mid = (wr - wr_hi.astype(F32)).astype(BF16)
    wr_lo = (wr - wr_hi.astype(F32) - wr_mid.astype(F32)).astype(BF16)
    wr3 = jnp.stack([wr_hi, wr_mid, wr_lo])
    br = jnp.zeros((1, LANES), F32)
    br = br.at[0, :N_GROUPS].set(b_route_group[0]).at[0, EXPERT_LANE0:EXPERT_LANE0 + n_exp].set(b_route_expert[0])

    head = jnp.concatenate([jnp.zeros((TILE - N_META, d), F32), meta_tokens.astype(F32)], axis=0)
    hn = _norm1(x, head, norm1_g[0]).reshape(n_tok, d)
    tm = _row_tile(lp)
    tpb = lp // tm
    tables = _rope_tables(lp)
    tab_spec = pl.BlockSpec((tm, LANES), lambda j, i: (i % tpb, 0))
    row_spec = lambda tn: pl.BlockSpec((1, tn), lambda j, i: (0, j))

    sbqkv = _proj(_proj_scale_kernel, hn, w_sb, [sb_colscale], [row_spec(sb_w)], BF16, tm, sb_w)
    q_mla = _proj(functools.partial(_proj_mlaq_kernel, scale=mla_scale * LOG2E), hn, w_mq,
                  list(tables), [tab_spec] * 3, BF16, tm, min(4, MLA_HEADS) * MLA_QK_PAD)
    ckr = _proj(_proj_scale_kernel, hn, w_c, [jnp.ones((1, w_c.shape[1]), F32)],
                [row_spec(w_c.shape[1])], F32, tm, w_c.shape[1])
    gates = _proj(_proj_gate_kernel, hn, w_g, [b_gate[0].reshape(1, 2 * d)], [row_spec(d)], F32, tm, d)
    k_mla, v_mla = _kvup(ckr, kv_norm_g[0], w_uk[0].astype(BF16), w_uv[0].astype(BF16), tables, tm, tpb)

    o_a = _sb_attention(sbqkv.reshape(b, lp, 3 * sb_w), seq)
    o_b = _mla_attention(q_mla.reshape(b, lp, -1), k_mla.reshape(b, lp, -1), v_mla.reshape(b, lp, -1), seq)
    h1 = _merge(o_a, o_b, gates.reshape(b, lp, 2 * d), x,
                w_proj_a[0].astype(BF16), w_proj_b[0].astype(BF16), w_out[0].astype(BF16)).reshape(n_real, d)

    hp, route, counts = _route(h1, norm2_g[0], wr3, br)
    cnt = counts[0, EXPERT_LANE0:EXPERT_LANE0 + n_exp].astype(jnp.int32)
    padded = (cnt + ROUTE_BLOCK - 1) // ROUTE_BLOCK * ROUTE_BLOCK
    pends = jnp.cumsum(padded)
    pstarts = pends - padded
    n_blocks = (n_real * TOP_K + n_exp * (ROUTE_BLOCK - 1) + ROUTE_BLOCK - 1) // ROUTE_BLOCK
    n_used = (pends[-1] // ROUTE_BLOCK).astype(jnp.int32).reshape(1)
    blk_start = jnp.minimum(jnp.arange(n_blocks), n_used[0] - 1) * ROUTE_BLOCK
    block_e = jnp.clip(jnp.searchsorted(pends, blk_start, side='right'), 0, n_exp - 1).astype(jnp.int32)
    e1 = route[:, 0].astype(jnp.int32)
    e2 = route[:, 1].astype(jnp.int32)
    dest0 = pstarts[e1] + route[:, 4].astype(jnp.int32)
    dest1 = pstarts[e2] + route[:, 5].astype(jnp.int32)

    xs = _dispatch(dest0, dest1, hp, n_blocks * ROUTE_BLOCK)
    y = _experts(block_e, n_used, xs, w1[0].astype(BF16), w3[0].astype(BF16), w2[0].astype(BF16))
    out = _combine(dest0, dest1, h1, route, final_g, y)
    return out.reshape(b, seq, d)
```

```python
import functools

import jax
import jax.numpy as jnp
from jax import lax
from jax.experimental import pallas as pl
from jax.experimental.pallas import tpu as pltpu

N_META = 16
CHUNK = 64
SB_HEADS = 8
SB_HEAD_DIM = 128
MLA_HEADS = 16
MLA_NOPE_DIM = 128
MLA_ROPE_DIM = 64
MLA_V_DIM = 128
MLA_KV_RANK = 512
ROPE_THETA = 10000.0
N_GROUPS = 4
EXPERTS_PER_GROUP = 8
TOP_K = 2
ROUTE_BLOCK = 256
RMS_EPS = 1e-6

TILE = 256
LANES = 128
MLA_QK_PAD = 256
ATTN_Q_TILES = 4
EXPERT_LANE0 = 8
VMEM_LIMIT = 56 * 1024 * 1024
NEG_BIG = -1e30
STICK_GONE_LOG2 = 152.0

F32 = jnp.float32
BF16 = jnp.bfloat16
LOG2E = 1.4426950408889634


def _cparams(sem):
    return pltpu.CompilerParams(dimension_semantics=sem, vmem_limit_bytes=VMEM_LIMIT)


def _rms(v, g):
    ms = jnp.mean(v * v, axis=-1, keepdims=True)
    return v * lax.rsqrt(ms + RMS_EPS) * g


def _row_tile(lp):
    for t in (768, 512, 256):
        if lp % t == 0:
            return t
    raise ValueError(lp)


def _norm1_kernel(x_ref, head_ref, g_ref, o_ref):
    i = pl.program_id(1)

    @pl.when(i == 0)
    def _():
        o_ref[0] = _rms(head_ref[...], g_ref[...]).astype(BF16)

    @pl.when(i > 0)
    def _():
        o_ref[0] = _rms(x_ref[0], g_ref[...]).astype(BF16)


def _norm1(x, head, g):
    b, s, d = x.shape
    nt = s // TILE + 1
    return pl.pallas_call(
        _norm1_kernel,
        out_shape=jax.ShapeDtypeStruct((b, nt * TILE, d), BF16),
        grid=(b, nt),
        in_specs=[
            pl.BlockSpec((1, TILE, d), lambda bi, i: (bi, jnp.maximum(i - 1, 0), 0)),
            pl.BlockSpec((TILE, d), lambda bi, i: (0, 0)),
            pl.BlockSpec((1, d), lambda bi, i: (0, 0)),
        ],
        out_specs=pl.BlockSpec((1, TILE, d), lambda bi, i: (bi, i, 0)),
        compiler_params=_cparams(("parallel", "parallel")),
        name="norm1",
    )(x, head, g.reshape(1, d))


def _rope_rows(r, cos_t, sin_a, sin_b):
    return r * cos_t + pltpu.roll(r, 96, 1) * sin_a + pltpu.roll(r, 32, 1) * sin_b


def _proj_scale_kernel(x_ref, w_ref, s_ref, o_ref):
    acc = jnp.dot(x_ref[...], w_ref[...], preferred_element_type=F32)
    o_ref[...] = (acc * s_ref[...]).astype(o_ref.dtype)


def _proj_gate_kernel(x_ref, w_ref, b_ref, o_ref):
    acc = jnp.dot(x_ref[...], w_ref[...], preferred_element_type=F32) + b_ref[...]
    o_ref[...] = 1.0 / (1.0 + jnp.exp(-acc))


def _proj_mlaq_kernel(x_ref, w_ref, cos_ref, sa_ref, sb_ref, o_ref, *, scale):
    acc = jnp.dot(x_ref[...], w_ref[...], preferred_element_type=F32)
    cos_t, sin_a, sin_b = cos_ref[...], sa_ref[...], sb_ref[...]
    for hh in range(acc.shape[1] // MLA_QK_PAD):
        c0 = hh * MLA_QK_PAD
        nope = acc[:, c0:c0 + LANES] * scale
        rope = _rope_rows(acc[:, c0 + LANES:c0 + 2 * LANES], cos_t, sin_a, sin_b) * scale
        o_ref[:, c0:c0 + LANES] = nope.astype(o_ref.dtype)
        o_ref[:, c0 + LANES:c0 + 2 * LANES] = rope.astype(o_ref.dtype)


def _store_lane_tiles(o_ref, val_t):
    for c in range(val_t.shape[1] // TILE):
        o_ref[c] = val_t[:, c * TILE:(c + 1) * TILE].astype(o_ref.dtype)


def _proj_t_kernel(x_ref, wt_ref, o_ref):
    acc_t = lax.dot_general(wt_ref[...], x_ref[...], (((1,), (1,)), ((), ())), preferred_element_type=F32)
    _store_lane_tiles(o_ref, acc_t)


def _proj_t(hn2d, wt, tm):
    m, k = hn2d.shape
    n = wt.shape[0]
    return pl.pallas_call(
        _proj_t_kernel,
        out_shape=jax.ShapeDtypeStruct((m // TILE, n, TILE), BF16),
        grid=(m // tm,),
        in_specs=[pl.BlockSpec((tm, k), lambda i: (i, 0)), pl.BlockSpec((n, k), lambda i: (0, 0))],
        out_specs=pl.BlockSpec((tm // TILE, n, TILE), lambda i: (i, 0, 0)),
        compiler_params=_cparams(("parallel",)),
        name="proj_t",
    )(hn2d, wt)


def _proj(kernel, hn2d, w, extras, extra_specs, out_dtype, tm, tn):
    m, k = hn2d.shape
    n = w.shape[1]
    return pl.pallas_call(
        kernel,
        out_shape=jax.ShapeDtypeStruct((m, n), out_dtype),
        grid=(n // tn, m // tm),
        in_specs=[
            pl.BlockSpec((tm, k), lambda j, i: (i, 0)),
            pl.BlockSpec((k, tn), lambda j, i: (0, j)),
        ] + extra_specs,
        out_specs=pl.BlockSpec((tm, tn), lambda j, i: (i, j)),
        compiler_params=_cparams(("parallel", "parallel")),
        name=getattr(kernel, "__name__", None) or kernel.func.__name__,
    )(hn2d, w, *extras)


def _kvup_kernel(c_ref, g_ref, wk_ref, wvt_ref, cos_ref, sa_ref, sb_ref, k_ref, vt_ref, *, rank):
    ckr = c_ref[...]
    cn = _rms(ckr[:, :rank], g_ref[...]).astype(BF16)
    kn = jnp.dot(cn, wk_ref[...], preferred_element_type=F32)
    vv_t = lax.dot_general(wvt_ref[...], cn, (((1,), (1,)), ((), ())), preferred_element_type=F32)
    rope = _rope_rows(ckr[:, rank:rank + LANES], cos_ref[...], sa_ref[...], sb_ref[...]).astype(BF16)
    for h in range(kn.shape[1] // MLA_NOPE_DIM):
        k_ref[:, h * MLA_QK_PAD:h * MLA_QK_PAD + LANES] = kn[:, h * LANES:(h + 1) * LANES].astype(BF16)
        k_ref[:, h * MLA_QK_PAD + LANES:(h + 1) * MLA_QK_PAD] = rope
    _store_lane_tiles(vt_ref, vv_t)


def _kvup(ckr, g, wk, wvt, tables, tm, tiles_per_batch):
    m, cw = ckr.shape
    rank = wk.shape[0]
    nk = wk.shape[1]
    nv = wvt.shape[0]
    heads = nk // MLA_NOPE_DIM
    tab_spec = pl.BlockSpec((tm, LANES), lambda i: (i % tiles_per_batch, 0))
    return pl.pallas_call(
        functools.partial(_kvup_kernel, rank=rank),
        out_shape=(jax.ShapeDtypeStruct((m, heads * MLA_QK_PAD), BF16),
                   jax.ShapeDtypeStruct((m // TILE, nv, TILE), BF16)),
        grid=(m // tm,),
        in_specs=[
            pl.BlockSpec((tm, cw), lambda i: (i, 0)),
            pl.BlockSpec((1, rank), lambda i: (0, 0)),
            pl.BlockSpec(wk.shape, lambda i: (0, 0)),
            pl.BlockSpec(wvt.shape, lambda i: (0, 0)),
            tab_spec, tab_spec, tab_spec,
        ],
        out_specs=(pl.BlockSpec((tm, heads * MLA_QK_PAD), lambda i: (i, 0)),
                   pl.BlockSpec((tm // TILE, nv, TILE), lambda i: (i, 0, 0))),
        compiler_params=_cparams(("parallel",)),
        name="kvup",
    )(ckr, g.reshape(1, rank), wk, wvt, *tables)


def _softplus2(z):
    neg_abs = pltpu.bitcast(pltpu.bitcast(z, jnp.uint32) | jnp.uint32(0x80000000), F32)
    return jnp.maximum(z, 0.0) + jnp.log2(1.0 + jnp.exp2(neg_abs))


def _sb_kernel(*refs, nsub, first_valid):
    q_refs = refs[:nsub]
    k_ref, vt_ref, o_ref, carry_ref, acc_ref = refs[nsub:]
    qs = pl.program_id(2)
    first_tile = 1 + nsub * qs
    carry_ref[...] = jnp.zeros_like(carry_ref)
    acc_ref[...] = jnp.zeros_like(acc_ref)
    trow = lax.broadcasted_iota(jnp.int32, (TILE, 2 * TILE), 0)
    tcol = lax.broadcasted_iota(jnp.int32, (TILE, 2 * TILE), 1) & (TILE - 1)
    tri2 = jnp.where(tcol >= trow, 1.0, 0.0).astype(BF16)

    def update(c0, j, mask):
        l0 = c0 * TILE
        q = jnp.concatenate([q_refs[c][0] for c in range(c0, nsub)], axis=0)
        kb = k_ref[0, pl.ds(pl.multiple_of(j * TILE, TILE), TILE), :]
        z = lax.dot_general(kb, q, (((1,), (1,)), ((), ())), preferred_element_type=F32)
        sp = _softplus2(z)
        if mask is not None:
            sp = jnp.where(mask, sp, 0.0)
        hi32 = pltpu.bitcast(pltpu.bitcast(sp, jnp.uint32) & jnp.uint32(0xFFFF0000), F32)
        parts = jnp.concatenate([hi32.astype(BF16), (sp - hi32).astype(BF16)], axis=0)
        cs = jnp.dot(tri2, parts, preferred_element_type=F32)
        carry = carry_ref[:, l0:]
        a = jnp.exp2(z - cs - carry)
        if mask is not None:
            a = jnp.where(mask, a, 0.0)
        acc_ref[:, l0:] = acc_ref[:, l0:] + jnp.dot(vt_ref[j], a.astype(BF16), preferred_element_type=F32)
        carry_ref[:, l0:] = carry + cs[0:1, :]

    for c in reversed(range(nsub)):
        nq = (nsub - c) * TILE
        key = lax.broadcasted_iota(jnp.int32, (TILE, nq), 0)
        qry = lax.broadcasted_iota(jnp.int32, (TILE, nq), 1)
        update(c, first_tile + c, (qry >= TILE) | (key < qry))

    def stick_left():
        return jnp.min(carry_ref[...]) < STICK_GONE_LOG2

    def body(state):
        t, _ = state
        j = first_tile - 1 - 2 * t
        update(0, j, None)
        update(0, j - 1, None)
        return t + 1, stick_left()

    n_pairs = (nsub * qs) // 2
    _, alive = lax.while_loop(lambda st: (st[0] < n_pairs) & st[1], body, (jnp.int32(0), stick_left()))

    @pl.when(alive)
    def _():
        key0 = lax.broadcasted_iota(jnp.int32, (TILE, nsub * TILE), 0)
        update(0, 0, key0 >= first_valid)

    o_ref[0] = acc_ref[...].T.astype(o_ref.dtype)


def _q_specs(nsub, width, col0):
    return [pl.BlockSpec((1, TILE, width),
                         functools.partial(lambda bi, h, i, c: (bi, nsub * i + 1 + c, col0 + h), c=c))
            for c in range(nsub)]


def _sb_attention(qk, vt, seq):
    b, lp, w2 = qk.shape
    heads = w2 // (2 * SB_HEAD_DIM)
    nsub = ATTN_Q_TILES
    rows = nsub * TILE
    return pl.pallas_call(
        functools.partial(_sb_kernel, nsub=nsub, first_valid=TILE - N_META),
        out_shape=jax.ShapeDtypeStruct((b, seq, heads * SB_HEAD_DIM), BF16),
        grid=(b, heads, seq // rows),
        in_specs=_q_specs(nsub, SB_HEAD_DIM, 0) + [
            pl.BlockSpec((1, lp, SB_HEAD_DIM), lambda bi, h, i: (bi, 0, heads + h)),
            pl.BlockSpec((lp // TILE, SB_HEAD_DIM, TILE), lambda bi, h, i: (bi, h, 0)),
        ],
        out_specs=pl.BlockSpec((1, rows, SB_HEAD_DIM), lambda bi, h, i: (bi, i, h)),
        scratch_shapes=[pltpu.VMEM((1, rows), F32), pltpu.VMEM((SB_HEAD_DIM, rows), F32)],
        compiler_params=_cparams(("parallel", "parallel", "arbitrary")),
        name="sb_attention",
    )(*([qk] * (nsub + 1)), vt)


def _mla_kernel(*refs, nsub, first_valid):
    q_refs = refs[:nsub]
    k_ref, vt_ref, o_ref, m_ref, l_ref, acc_ref = refs[nsub:]
    qs = pl.program_id(2)
    first_tile = 1 + nsub * qs
    m_ref[...] = jnp.full(m_ref.shape, NEG_BIG, F32)
    l_ref[...] = jnp.zeros_like(l_ref)
    acc_ref[...] = jnp.zeros_like(acc_ref)
    shift = CHUNK.bit_length() - 1

    def update(c0, j, mask, ntiles=1):
        l0 = c0 * TILE
        q = jnp.concatenate([q_refs[c][0] for c in range(c0, nsub)], axis=0)
        kb = k_ref[0, pl.ds(pl.multiple_of(j * TILE, TILE), ntiles * TILE), :]
        vtb = vt_ref[j] if ntiles == 1 else jnp.concatenate([vt_ref[j + t] for t in range(ntiles)], axis=1)
        s = lax.dot_general(kb, q, (((1,), (1,)), ((), ())), preferred_element_type=F32)
        if mask is not None:
            s = jnp.where(mask, s, NEG_BIG)
        m_old = m_ref[:, l0:]
        m_new = jnp.maximum(m_old, jnp.max(s, axis=0, keepdims=True))
        alpha = jnp.exp2(m_old - m_new)
        p = jnp.exp2(s - m_new)
        l_ref[:, l0:] = alpha * l_ref[:, l0:] + jnp.sum(p, axis=0, keepdims=True)
        acc_ref[:, l0:] = alpha * acc_ref[:, l0:] + jnp.dot(vtb, p.astype(BF16), preferred_element_type=F32)
        m_ref[:, l0:] = m_new

    key0 = lax.broadcasted_iota(jnp.int32, (TILE, nsub * TILE), 0)
    update(0, 0, key0 >= first_valid)

    def body(t, carry):
        update(0, 1 + 2 * t, None, ntiles=2)
        return carry

    lax.fori_loop(0, (nsub * qs) // 2, body, 0)
    for c in range(nsub):
        nq = (nsub - c) * TILE
        key = lax.broadcasted_iota(jnp.int32, (TILE, nq), 0)
        qry = lax.broadcasted_iota(jnp.int32, (TILE, nq), 1)
        update(c, first_tile + c, (qry >= TILE) | ((key >> shift) <= (qry >> shift)))
    o_ref[0] = (acc_ref[...] / l_ref[...]).T.astype(o_ref.dtype)


def _mla_attention(q, k, vt, seq):
    b, lp, _ = q.shape
    heads = vt.shape[1] // MLA_V_DIM
    nsub = ATTN_Q_TILES
    rows = nsub * TILE
    return pl.pallas_call(
        functools.partial(_mla_kernel, nsub=nsub, first_valid=TILE - N_META),
        out_shape=jax.ShapeDtypeStruct((b, seq, heads * MLA_V_DIM), BF16),
        grid=(b, heads, seq // rows),
        in_specs=_q_specs(nsub, MLA_QK_PAD, 0) + [
            pl.BlockSpec((1, lp, MLA_QK_PAD), lambda bi, h, i: (bi, 0, h)),
            pl.BlockSpec((lp // TILE, MLA_V_DIM, TILE), lambda bi, h, i: (bi, h, 0)),
        ],
        out_specs=pl.BlockSpec((1, rows, MLA_V_DIM), lambda bi, h, i: (bi, i, h)),
        scratch_shapes=[pltpu.VMEM((1, rows), F32), pltpu.VMEM((1, rows), F32),
                        pltpu.VMEM((MLA_V_DIM, rows), F32)],
        compiler_params=_cparams(("parallel", "parallel", "arbitrary")),
        name="mla_attention",
    )(*([q] * nsub), k, vt)


def _merge_kernel(oa_ref, ob_ref, ga_ref, gb_ref, x_ref, wpa_ref, wpb_ref, wo_ref, o_ref):
    pa = jnp.dot(oa_ref[0], wpa_ref[...], preferred_element_type=F32)
    pb = jnp.dot(ob_ref[0], wpb_ref[...], preferred_element_type=F32)
    y = ga_ref[0] * pa + gb_ref[0] * pb
    o_ref[0] = x_ref[0] + jnp.dot(y.astype(BF16), wo_ref[...], preferred_element_type=F32)


def _merge(oa, ob, gates, x, wpa, wpb, wo):
    b, s, d = x.shape
    nt = s // TILE
    resident = lambda w: pl.BlockSpec(w.shape, lambda bi, i: (0, 0), pipeline_mode=pl.Buffered(1))
    return pl.pallas_call(
        _merge_kernel,
        out_shape=jax.ShapeDtypeStruct((b, s, d), F32),
        grid=(b, nt),
        in_specs=[
            pl.BlockSpec((1, TILE, oa.shape[2]), lambda bi, i: (bi, i, 0)),
            pl.BlockSpec((1, TILE, ob.shape[2]), lambda bi, i: (bi, i, 0)),
            pl.BlockSpec((1, TILE, d), lambda bi, i: (bi, i + 1, 0)),
            pl.BlockSpec((1, TILE, d), lambda bi, i: (bi, i + 1, 1)),
            pl.BlockSpec((1, TILE, d), lambda bi, i: (bi, i, 0)),
            resident(wpa), resident(wpb), resident(wo),
        ],
        out_specs=pl.BlockSpec((1, TILE, d), lambda bi, i: (bi, i, 0)),
        compiler_params=_cparams(("parallel", "parallel")),
        name="merge",
    )(oa, ob, gates, gates, x, wpa, wpb, wo)


def _split3(a):
    hi = a.astype(BF16)
    r1 = a - hi.astype(F32)
    mid = r1.astype(BF16)
    lo = (r1 - mid.astype(F32)).astype(BF16)
    return hi, mid, lo


def _route_kernel(h_ref, g_ref, wr_ref, br_ref, hp_ref, r_ref, cnt_ref, carry_ref, *, tm, half):
    i = pl.program_id(0)

    @pl.when(i == 0)
    def _():
        carry_ref[...] = jnp.zeros_like(carry_ref)

    hn = _rms(h_ref[...], g_ref[...])

    lo_bits = pltpu.bitcast(hn[:, :half].astype(BF16).astype(F32), jnp.uint32)
    hi_bits = pltpu.bitcast(hn[:, half:].astype(BF16).astype(F32), jnp.uint32)
    hp_ref[...] = (hi_bits & jnp.uint32(0xFFFF0000)) | (lo_bits >> 16)

    a_hi, a_mid, a_lo = _split3(hn)
    w_hi, w_mid, w_lo = wr_ref[0], wr_ref[1], wr_ref[2]
    dot = lambda a, w: jnp.dot(a, w, preferred_element_type=F32)
    lg = (dot(a_lo, w_hi) + dot(a_hi, w_lo) + dot(a_mid, w_mid)
          + dot(a_mid, w_hi) + dot(a_hi, w_mid) + dot(a_hi, w_hi)) + br_ref[...]

    lane = lax.broadcasted_iota(jnp.int32, lg.shape, 1)
    rmax = lambda v: jnp.max(v, axis=1, keepdims=True)
    rmin = lambda v: jnp.min(v, axis=1, keepdims=True)
    rsum = lambda v: jnp.sum(v, axis=1, keepdims=True)

    gmask = lane < N_GROUPS
    gl = jnp.where(gmask, lg, -jnp.inf)
    gmax = rmax(gl)
    gsel = rmin(jnp.where(gl == gmax, lane, LANES))
    p_g = 1.0 / rsum(jnp.where(gmask, jnp.exp(lg - gmax), 0.0))

    e_lo = EXPERT_LANE0 + gsel * EXPERTS_PER_GROUP
    emask = (lane >= e_lo) & (lane < e_lo + EXPERTS_PER_GROUP)
    emax = rmax(jnp.where(emask, lg, -jnp.inf))
    ex = jnp.where(emask, jnp.exp(lg - emax), 0.0)
    prob = jnp.where(emask, ex / rsum(ex), -1.0)
    top1 = rmax(prob)
    i1 = rmin(jnp.where(prob == top1, lane, LANES))
    prob2 = jnp.where(lane == i1, -1.0, prob)
    top2 = rmax(prob2)
    i2 = rmin(jnp.where(prob2 == top2, lane, LANES))
    denom = top1 + top2
    w1 = p_g * top1 / denom
    w2 = p_g * top2 / denom

    sel = ((lane == i1) | (lane == i2))
    row = lax.broadcasted_iota(jnp.int32, (tm, tm), 0)
    col = lax.broadcasted_iota(jnp.int32, (tm, tm), 1)
    before = (col < row).astype(BF16)
    prefix = dot(before, jnp.where(sel, 1.0, 0.0).astype(BF16)) + carry_ref[...]
    rank1 = rsum(jnp.where(lane == i1, prefix, 0.0))
    rank2 = rsum(jnp.where(lane == i2, prefix, 0.0))
    carry_ref[...] = carry_ref[...] + jnp.sum(jnp.where(sel, 1.0, 0.0), axis=0, keepdims=True)
    cnt_ref[...] = carry_ref[...]

    e1 = (i1 - EXPERT_LANE0).astype(F32)
    e2 = (i2 - EXPERT_LANE0).astype(F32)
    out = jnp.zeros(lg.shape, F32)
    for k, val in enumerate((e1, e2, w1, w2, rank1, rank2)):
        out = jnp.where(lane == k, val, out)
    r_ref[...] = out


def _route(h1, g, wr3, br):
    n, d = h1.shape
    tm = TILE
    return pl.pallas_call(
        functools.partial(_route_kernel, tm=tm, half=d // 2),
        out_shape=(jax.ShapeDtypeStruct((n, d // 2), jnp.uint32),
                   jax.ShapeDtypeStruct((n, LANES), F32),
                   jax.ShapeDtypeStruct((1, LANES), F32)),
        grid=(n // tm,),
        in_specs=[
            pl.BlockSpec((tm, d), lambda i: (i, 0)),
            pl.BlockSpec((1, d), lambda i: (0, 0)),
            pl.BlockSpec(wr3.shape, lambda i: (0, 0, 0)),
            pl.BlockSpec((1, LANES), lambda i: (0, 0)),
        ],
        out_specs=(pl.BlockSpec((tm, d // 2), lambda i: (i, 0)),
                   pl.BlockSpec((tm, LANES), lambda i: (i, 0)),
                   pl.BlockSpec((1, LANES), lambda i: (0, 0))),
        scratch_shapes=[pltpu.VMEM((1, LANES), F32)],
        compiler_params=_cparams(("arbitrary",)),
        name="route",
    )(h1, g.reshape(1, d), wr3, br)


def _dispatch_kernel(d0_ref, d1_ref, src_ref, init_ref, dst_ref, sem, *, tm):
    del init_ref
    base = pl.program_id(0) * tm

    def copy(r, dest_ref, s):
        return pltpu.make_async_copy(src_ref.at[pl.ds(r, 1)], dst_ref.at[pl.ds(dest_ref[base + r], 1)], sem.at[s])

    def start(r, c):
        copy(r, d0_ref, 0).start()
        copy(r, d1_ref, 1).start()
        return c

    def wait(r, c):
        copy(r, d0_ref, 0).wait()
        copy(r, d1_ref, 1).wait()
        return c

    lax.fori_loop(0, tm, start, 0)
    lax.fori_loop(0, tm, wait, 0)


def _dispatch(dest0, dest1, hp, p_rows):
    n, w = hp.shape
    tm = TILE
    init = jnp.zeros((p_rows, w), hp.dtype)
    return pl.pallas_call(
        functools.partial(_dispatch_kernel, tm=tm),
        out_shape=jax.ShapeDtypeStruct((p_rows, w), hp.dtype),
        grid_spec=pltpu.PrefetchScalarGridSpec(
            num_scalar_prefetch=2,
            grid=(n // tm,),
            in_specs=[pl.BlockSpec((tm, w), lambda i, a, b: (i, 0)), pl.BlockSpec(memory_space=pl.ANY)],
            out_specs=pl.BlockSpec(memory_space=pl.ANY),
            scratch_shapes=[pltpu.SemaphoreType.DMA((2,))],
        ),
        input_output_aliases={3: 0},
        compiler_params=_cparams(("arbitrary",)),
        name="dispatch",
    )(dest0, dest1, hp, init)


def _expert_kernel(be_ref, nu_ref, x_ref, w1_ref, w3_ref, w2_ref, y_ref):
    del be_ref

    @pl.when(pl.program_id(0) < nu_ref[0])
    def _():
        xw = x_ref[...]
        lo = pltpu.bitcast(xw << 16, F32).astype(BF16)
        hi = pltpu.bitcast(xw & jnp.uint32(0xFFFF0000), F32).astype(BF16)
        xb = jnp.concatenate([lo, hi], axis=1)
        a = jnp.dot(xb, w1_ref[0], preferred_element_type=F32)
        g = jnp.dot(xb, w3_ref[0], preferred_element_type=F32)
        hid = (a * (1.0 / (1.0 + jnp.exp(-a))) * g).astype(BF16)
        y_ref[...] = jnp.dot(hid, w2_ref[0], preferred_element_type=F32)

    @pl.when(pl.program_id(0) >= nu_ref[0])
    def _():
        y_ref[...] = jnp.zeros_like(y_ref)


def _experts(block_e, n_used, xs, w1, w3, w2):
    p_rows, half = xs.shape
    _, d, ff = w1.shape
    nb = p_rows // ROUTE_BLOCK
    last = lambda j, nu: jnp.minimum(j, nu[0] - 1)
    return pl.pallas_call(
        _expert_kernel,
        out_shape=jax.ShapeDtypeStruct((p_rows, d), F32),
        grid_spec=pltpu.PrefetchScalarGridSpec(
            num_scalar_prefetch=2,
            grid=(nb,),
            in_specs=[
                pl.BlockSpec((ROUTE_BLOCK, half), lambda j, be, nu: (last(j, nu), 0)),
                pl.BlockSpec((1, d, ff), lambda j, be, nu: (be[last(j, nu)], 0, 0)),
                pl.BlockSpec((1, d, ff), lambda j, be, nu: (be[last(j, nu)], 0, 0)),
                pl.BlockSpec((1, ff, d), lambda j, be, nu: (be[last(j, nu)], 0, 0)),
            ],
            out_specs=pl.BlockSpec((ROUTE_BLOCK, d), lambda j, be, nu: (j, 0)),
        ),
        compiler_params=_cparams(("arbitrary",)),
        name="experts",
    )(block_e, n_used, xs, w1, w3, w2)


def _combine_kernel(d0_ref, d1_ref, h_ref, r_ref, g_ref, y_ref, o_ref, ya, yb, sem, *, tm):
    base = pl.program_id(0) * tm

    def copy(r, dest_ref, buf, s):
        return pltpu.make_async_copy(y_ref.at[pl.ds(dest_ref[base + r], 1)], buf.at[pl.ds(r, 1)], sem.at[s])

    def start(r, c):
        copy(r, d0_ref, ya, 0).start()
        copy(r, d1_ref, yb, 1).start()
        return c

    def wait(r, c):
        copy(r, d0_ref, ya, 0).wait()
        copy(r, d1_ref, yb, 1).wait()
        return c

    lax.fori_loop(0, tm, start, 0)
    lax.fori_loop(0, tm, wait, 0)

    route = r_ref[...]
    lane = lax.broadcasted_iota(jnp.int32, route.shape, 1)
    w1 = jnp.sum(jnp.where(lane == 2, route, 0.0), axis=1, keepdims=True)
    w2 = jnp.sum(jnp.where(lane == 3, route, 0.0), axis=1, keepdims=True)
    h2 = h_ref[...] + (ya[...] * w1 + yb[...] * w2)
    o_ref[...] = _rms(h2, g_ref[...])


def _combine(dest0, dest1, h1, route, g, y):
    n, d = h1.shape
    tm = TILE
    return pl.pallas_call(
        functools.partial(_combine_kernel, tm=tm),
        out_shape=jax.ShapeDtypeStruct((n, d), F32),
        grid_spec=pltpu.PrefetchScalarGridSpec(
            num_scalar_prefetch=2,
            grid=(n // tm,),
            in_specs=[
                pl.BlockSpec((tm, d), lambda i, a, b: (i, 0)),
                pl.BlockSpec((tm, LANES), lambda i, a, b: (i, 0)),
                pl.BlockSpec((1, d), lambda i, a, b: (0, 0)),
                pl.BlockSpec(memory_space=pl.ANY),
            ],
            out_specs=pl.BlockSpec((tm, d), lambda i, a, b: (i, 0)),
            scratch_shapes=[pltpu.VMEM((tm, d), F32), pltpu.VMEM((tm, d), F32),
                            pltpu.SemaphoreType.DMA((2,))],
        ),
        compiler_params=_cparams(("arbitrary",)),
        name="combine",
    )(dest0, dest1, h1, route, g.reshape(1, d), y)


def _rope_tables(lp):
    half = MLA_ROPE_DIM // 2
    inv = ROPE_THETA ** (-jnp.arange(half, dtype=F32) / half)
    pos = (jnp.arange(lp) - (TILE - N_META)).astype(F32)
    ang = pos[:, None] * inv[None, :]
    cos, sin = jnp.cos(ang), jnp.sin(ang)
    z32 = jnp.zeros((lp, half), F32)
    z64 = jnp.zeros((lp, LANES - MLA_ROPE_DIM), F32)
    return (jnp.concatenate([cos, cos, z64], axis=1),
            jnp.concatenate([-sin, z32, z64], axis=1),
            jnp.concatenate([z32, sin, z64], axis=1))


def kernel(x, meta_tokens, norm1_g, w_in, b_gate, kv_norm_g, w_uk, w_uv, w_proj_a, w_proj_b, w_out,
           norm2_g, w_route_group, b_route_group, w_route_expert, b_route_expert, w1, w3, w2, final_g):
    b, seq, d = x.shape
    assert seq % TILE == 0 and TILE % CHUNK == 0 and N_META <= TILE
    lp = TILE + seq
    n_tok = b * lp
    n_real = b * seq
    n_exp = N_GROUPS * EXPERTS_PER_GROUP
    sb_w = SB_HEADS * SB_HEAD_DIM
    qk_dim = MLA_NOPE_DIM + MLA_ROPE_DIM
    mq_w = MLA_HEADS * qk_dim
    sb_scale = SB_HEAD_DIM ** -0.5
    mla_scale = qk_dim ** -0.5

    wi = w_in[0]
    o_q = 3 * sb_w
    o_c = o_q + mq_w
    o_r = o_c + MLA_KV_RANK
    o_g = o_r + MLA_ROPE_DIM
    w_sb = wi[:, :2 * sb_w].astype(BF16)
    w_sbv_t = wi[:, 2 * sb_w:o_q].T.astype(BF16)
    w_mq = jnp.pad(wi[:, o_q:o_c].reshape(d, MLA_HEADS, qk_dim),
                   ((0, 0), (0, 0), (0, MLA_QK_PAD - qk_dim))).reshape(d, MLA_HEADS * MLA_QK_PAD).astype(BF16)
    w_c = jnp.pad(wi[:, o_c:o_g], ((0, 0), (0, LANES - MLA_ROPE_DIM))).astype(BF16)
    w_g = wi[:, o_g:].astype(BF16)
    sb_colscale = jnp.concatenate([jnp.full((1, sb_w), sb_scale * LOG2E, F32), jnp.ones((1, sb_w), F32)], axis=1)
    wr = jnp.zeros((d, LANES), F32)
    wr = wr.at[:, :N_GROUPS].set(w_route_group[0]).at[:, EXPERT_LANE0:EXPERT_LANE0 + n_exp].set(w_route_expert[0])
    wr_hi = wr.astype(BF16)
    wr_mid = (wr - wr_hi.astype(F32)).astype(BF16)
    wr_lo = (wr - wr_hi.astype(F32) - wr_mid.astype(F32)).astype(BF16)
    wr3 = jnp.stack([wr_hi, wr_mid, wr_lo])
    br = jnp.zeros((1, LANES), F32)
    br = br.at[0, :N_GROUPS].set(b_route_group[0]).at[0, EXPERT_LANE0:EXPERT_LANE0 + n_exp].set(b_route_expert[0])

    head = jnp.concatenate([jnp.zeros((TILE - N_META, d), F32), meta_tokens.astype(F32)], axis=0)
    hn = _norm1(x, head, norm1_g[0]).reshape(n_tok, d)
    tm = _row_tile(lp)
    tpb = lp // tm
    tables = _rope_tables(lp)
    tab_spec = pl.BlockSpec((tm, LANES), lambda j, i: (i % tpb, 0))
    row_spec = lambda tn: pl.BlockSpec((1, tn), lambda j, i: (0, j))

    sb_qk = _proj(_proj_scale_kernel, hn, w_sb, [sb_colscale], [row_spec(sb_w)], BF16, tm, sb_w)
    sb_vt = _proj_t(hn, w_sbv_t, tm)
    q_mla = _proj(functools.partial(_proj_mlaq_kernel, scale=mla_scale * LOG2E), hn, w_mq,
                  list(tables), [tab_spec] * 3, BF16, tm, min(4, MLA_HEADS) * MLA_QK_PAD)
    ckr = _proj(_proj_scale_kernel, hn, w_c, [jnp.ones((1, w_c.shape[1]), F32)],
                [row_spec(w_c.shape[1])], F32, tm, w_c.shape[1])
    gates = _proj(_proj_gate_kernel, hn, w_g, [b_gate[0].reshape(1, 2 * d)], [row_spec(d)], F32, tm, d)
    k_mla, vt_mla = _kvup(ckr, kv_norm_g[0], w_uk[0].astype(BF16), w_uv[0].T.astype(BF16), tables, tm, tpb)

    o_a = _sb_attention(sb_qk.reshape(b, lp, 2 * sb_w), sb_vt, seq)
    o_b = _mla_attention(q_mla.reshape(b, lp, -1), k_mla.reshape(b, lp, -1), vt_mla, seq)
    h1 = _merge(o_a, o_b, gates.reshape(b, lp, 2 * d), x,
                w_proj_a[0].astype(BF16), w_proj_b[0].astype(BF16), w_out[0].astype(BF16)).reshape(n_real, d)

    hp, route, counts = _route(h1, norm2_g[0], wr3, br)
    cnt = counts[0, EXPERT_LANE0:EXPERT_LANE0 + n_exp].astype(jnp.int32)
    padded = (cnt + ROUTE_BLOCK - 1) // ROUTE_BLOCK * ROUTE_BLOCK
    pends = jnp.cumsum(padded)
    pstarts = pends - padded
    n_blocks = (n_real * TOP_K + n_exp * (ROUTE_BLOCK - 1) + ROUTE_BLOCK - 1) // ROUTE_BLOCK
    n_used = (pends[-1] // ROUTE_BLOCK).astype(jnp.int32).reshape(1)
    blk_start = jnp.minimum(jnp.arange(n_blocks), n_used[0] - 1) * ROUTE_BLOCK
    block_e = jnp.clip(jnp.searchsorted(pends, blk_start, side='right'), 0, n_exp - 1).astype(jnp.int32)
    e1 = route[:, 0].astype(jnp.int32)
    e2 = route[:, 1].astype(jnp.int32)
    dest0 = pstarts[e1] + route[:, 4].astype(jnp.int32)
    dest1 = pstarts[e2] + route[:, 5].astype(jnp.int32)

    xs = _dispatch(dest0, dest1, hp, n_blocks * ROUTE_BLOCK)
    y = _experts(block_e, n_used, xs, w1[0].astype(BF16), w3[0].astype(BF16), w2[0].astype(BF16))
    out = _combine(dest0, dest1, h1, route, final_g, y)
    return out.reshape(b, seq, d)
```

```python
import functools

import jax
import jax.numpy as jnp
from jax import lax
from jax.experimental import pallas as pl
from jax.experimental.pallas import tpu as pltpu

N_META = 16
CHUNK = 64
SB_HEADS = 8
SB_HEAD_DIM = 128
MLA_HEADS = 16
MLA_NOPE_DIM = 128
MLA_ROPE_DIM = 64
MLA_V_DIM = 128
MLA_KV_RANK = 512
ROPE_THETA = 10000.0
N_GROUPS = 4
EXPERTS_PER_GROUP = 8
TOP_K = 2
ROUTE_BLOCK = 256
RMS_EPS = 1e-6

TILE = 256
LANES = 128
MLA_QK_PAD = 256
ATTN_Q_TILES = 4
EXPERT_LANE0 = 8
VMEM_LIMIT = 56 * 1024 * 1024
NEG_BIG = -1e30
STICK_GONE_LOG2 = 152.0

F32 = jnp.float32
BF16 = jnp.bfloat16
LOG2E = 1.4426950408889634


def _cparams(sem):
    return pltpu.CompilerParams(dimension_semantics=sem, vmem_limit_bytes=VMEM_LIMIT)


def _rms(v, g):
    ms = jnp.mean(v * v, axis=-1, keepdims=True)
    return v * lax.rsqrt(ms + RMS_EPS) * g


def _row_tile(lp):
    for t in (768, 512, 256):
        if lp % t == 0:
            return t
    raise ValueError(lp)


def _norm1_kernel(x_ref, head_ref, g_ref, o_ref):
    i = pl.program_id(1)

    @pl.when(i == 0)
    def _():
        o_ref[0] = _rms(head_ref[...], g_ref[...]).astype(BF16)

    @pl.when(i > 0)
    def _():
        o_ref[0] = _rms(x_ref[0], g_ref[...]).astype(BF16)


def _norm1(x, head, g):
    b, s, d = x.shape
    nt = s // TILE + 1
    return pl.pallas_call(
        _norm1_kernel,
        out_shape=jax.ShapeDtypeStruct((b, nt * TILE, d), BF16),
        grid=(b, nt),
        in_specs=[
            pl.BlockSpec((1, TILE, d), lambda bi, i: (bi, jnp.maximum(i - 1, 0), 0)),
            pl.BlockSpec((TILE, d), lambda bi, i: (0, 0)),
            pl.BlockSpec((1, d), lambda bi, i: (0, 0)),
        ],
        out_specs=pl.BlockSpec((1, TILE, d), lambda bi, i: (bi, i, 0)),
        compiler_params=_cparams(("parallel", "parallel")),
        name="norm1",
    )(x, head, g.reshape(1, d))


def _rope_rows(r, cos_t, sin_a, sin_b):
    return r * cos_t + pltpu.roll(r, 96, 1) * sin_a + pltpu.roll(r, 32, 1) * sin_b


def _proj_scale_kernel(x_ref, w_ref, s_ref, o_ref):
    acc = jnp.dot(x_ref[...], w_ref[...], preferred_element_type=F32)
    o_ref[...] = (acc * s_ref[...]).astype(o_ref.dtype)


def _proj_gate_kernel(x_ref, w_ref, b_ref, o_ref):
    acc = jnp.dot(x_ref[...], w_ref[...], preferred_element_type=F32) + b_ref[...]
    o_ref[...] = 1.0 / (1.0 + jnp.exp(-acc))


def _proj_mlaq_kernel(x_ref, w_ref, cos_ref, sa_ref, sb_ref, o_ref, *, scale):
    acc = jnp.dot(x_ref[...], w_ref[...], preferred_element_type=F32)
    cos_t, sin_a, sin_b = cos_ref[...], sa_ref[...], sb_ref[...]
    for hh in range(acc.shape[1] // MLA_QK_PAD):
        c0 = hh * MLA_QK_PAD
        nope = acc[:, c0:c0 + LANES] * scale
        rope = _rope_rows(acc[:, c0 + LANES:c0 + 2 * LANES], cos_t, sin_a, sin_b) * scale
        o_ref[:, c0:c0 + LANES] = nope.astype(o_ref.dtype)
        o_ref[:, c0 + LANES:c0 + 2 * LANES] = rope.astype(o_ref.dtype)


def _store_lane_tiles(o_ref, val_t):
    for c in range(val_t.shape[1] // TILE):
        o_ref[c] = val_t[:, c * TILE:(c + 1) * TILE].astype(o_ref.dtype)


def _proj_t_kernel(x_ref, wt_ref, o_ref):
    acc_t = lax.dot_general(wt_ref[...], x_ref[...], (((1,), (1,)), ((), ())), preferred_element_type=F32)
    _store_lane_tiles(o_ref, acc_t)


def _proj_t(hn2d, wt, tm):
    m, k = hn2d.shape
    n = wt.shape[0]
    return pl.pallas_call(
        _proj_t_kernel,
        out_shape=jax.ShapeDtypeStruct((m // TILE, n, TILE), BF16),
        grid=(m // tm,),
        in_specs=[pl.BlockSpec((tm, k), lambda i: (i, 0)), pl.BlockSpec((n, k), lambda i: (0, 0))],
        out_specs=pl.BlockSpec((tm // TILE, n, TILE), lambda i: (i, 0, 0)),
        compiler_params=_cparams(("parallel",)),
        name="proj_t",
    )(hn2d, wt)


def _proj(kernel, hn2d, w, extras, extra_specs, out_dtype, tm, tn):
    m, k = hn2d.shape
    n = w.shape[1]
    return pl.pallas_call(
        kernel,
        out_shape=jax.ShapeDtypeStruct((m, n), out_dtype),
        grid=(n // tn, m // tm),
        in_specs=[
            pl.BlockSpec((tm, k), lambda j, i: (i, 0)),
            pl.BlockSpec((k, tn), lambda j, i: (0, j)),
        ] + extra_specs,
        out_specs=pl.BlockSpec((tm, tn), lambda j, i: (i, j)),
        compiler_params=_cparams(("parallel", "parallel")),
        name=getattr(kernel, "__name__", None) or kernel.func.__name__,
    )(hn2d, w, *extras)


def _kvup_kernel(c_ref, g_ref, wk_ref, wvt_ref, cos_ref, sa_ref, sb_ref, k_ref, vt_ref, *, rank):
    ckr = c_ref[...]
    cn = _rms(ckr[:, :rank], g_ref[...]).astype(BF16)
    kn = jnp.dot(cn, wk_ref[...], preferred_element_type=F32)
    vv_t = lax.dot_general(wvt_ref[...], cn, (((1,), (1,)), ((), ())), preferred_element_type=F32)
    rope = _rope_rows(ckr[:, rank:rank + LANES], cos_ref[...], sa_ref[...], sb_ref[...]).astype(BF16)
    for h in range(kn.shape[1] // MLA_NOPE_DIM):
        k_ref[:, h * MLA_QK_PAD:h * MLA_QK_PAD + LANES] = kn[:, h * LANES:(h + 1) * LANES].astype(BF16)
        k_ref[:, h * MLA_QK_PAD + LANES:(h + 1) * MLA_QK_PAD] = rope
    _store_lane_tiles(vt_ref, vv_t)


def _kvup(ckr, g, wk, wvt, tables, tm, tiles_per_batch):
    m, cw = ckr.shape
    rank = wk.shape[0]
    nk = wk.shape[1]
    nv = wvt.shape[0]
    heads = nk // MLA_NOPE_DIM
    tab_spec = pl.BlockSpec((tm, LANES), lambda i: (i % tiles_per_batch, 0))
    return pl.pallas_call(
        functools.partial(_kvup_kernel, rank=rank),
        out_shape=(jax.ShapeDtypeStruct((m, heads * MLA_QK_PAD), BF16),
                   jax.ShapeDtypeStruct((m // TILE, nv, TILE), BF16)),
        grid=(m // tm,),
        in_specs=[
            pl.BlockSpec((tm, cw), lambda i: (i, 0)),
            pl.BlockSpec((1, rank), lambda i: (0, 0)),
            pl.BlockSpec(wk.shape, lambda i: (0, 0)),
            pl.BlockSpec(wvt.shape, lambda i: (0, 0)),
            tab_spec, tab_spec, tab_spec,
        ],
        out_specs=(pl.BlockSpec((tm, heads * MLA_QK_PAD), lambda i: (i, 0)),
                   pl.BlockSpec((tm // TILE, nv, TILE), lambda i: (i, 0, 0))),
        compiler_params=_cparams(("parallel",)),
        name="kvup",
    )(ckr, g.reshape(1, rank), wk, wvt, *tables)


def _softplus2(z):
    neg_abs = pltpu.bitcast(pltpu.bitcast(z, jnp.uint32) | jnp.uint32(0x80000000), F32)
    return jnp.maximum(z, 0.0) + jnp.log2(1.0 + jnp.exp2(neg_abs))


def _sb_kernel(*refs, nsub, first_valid):
    q_refs = refs[:nsub]
    k_ref, vt_ref, o_ref, carry_ref, acc_ref = refs[nsub:]
    qs = pl.program_id(2)
    first_tile = 1 + nsub * qs
    carry_ref[...] = jnp.zeros_like(carry_ref)
    acc_ref[...] = jnp.zeros_like(acc_ref)
    trow = lax.broadcasted_iota(jnp.int32, (TILE, 2 * TILE), 0)
    tcol = lax.broadcasted_iota(jnp.int32, (TILE, 2 * TILE), 1) & (TILE - 1)
    tri2 = jnp.where(tcol >= trow, 1.0, 0.0).astype(BF16)

    def update(c0, j, mask):
        l0 = c0 * TILE
        q = jnp.concatenate([q_refs[c][0] for c in range(c0, nsub)], axis=0)
        kb = k_ref[0, pl.ds(pl.multiple_of(j * TILE, TILE), TILE), :]
        z = lax.dot_general(kb, q, (((1,), (1,)), ((), ())), preferred_element_type=F32)
        sp = _softplus2(z)
        if mask is not None:
            sp = jnp.where(mask, sp, 0.0)
        hi32 = pltpu.bitcast(pltpu.bitcast(sp, jnp.uint32) & jnp.uint32(0xFFFF0000), F32)
        parts = jnp.concatenate([hi32.astype(BF16), (sp - hi32).astype(BF16)], axis=0)
        cs = jnp.dot(tri2, parts, preferred_element_type=F32)
        carry = carry_ref[:, l0:]
        a = jnp.exp2(z - cs - carry)
        if mask is not None:
            a = jnp.where(mask, a, 0.0)
        acc_ref[:, l0:] = acc_ref[:, l0:] + jnp.dot(vt_ref[j], a.astype(BF16), preferred_element_type=F32)
        carry_ref[:, l0:] = carry + cs[0:1, :]

    for c in reversed(range(nsub)):
        nq = (nsub - c) * TILE
        key = lax.broadcasted_iota(jnp.int32, (TILE, nq), 0)
        qry = lax.broadcasted_iota(jnp.int32, (TILE, nq), 1)
        update(c, first_tile + c, (qry >= TILE) | (key < qry))

    def stick_left():
        return jnp.min(carry_ref[...]) < STICK_GONE_LOG2

    def body(state):
        t, _ = state
        j = first_tile - 1 - 2 * t
        update(0, j, None)
        update(0, j - 1, None)
        return t + 1, stick_left()

    n_pairs = (nsub * qs) // 2
    _, alive = lax.while_loop(lambda st: (st[0] < n_pairs) & st[1], body, (jnp.int32(0), stick_left()))

    @pl.when(alive)
    def _():
        key0 = lax.broadcasted_iota(jnp.int32, (TILE, nsub * TILE), 0)
        update(0, 0, key0 >= first_valid)

    o_ref[0] = acc_ref[...].T.astype(o_ref.dtype)


def _q_specs(nsub, width, col0):
    return [pl.BlockSpec((1, TILE, width),
                         functools.partial(lambda bi, h, i, c: (bi, nsub * i + 1 + c, col0 + h), c=c))
            for c in range(nsub)]


def _sb_attention(qk, vt, seq):
    b, lp, w2 = qk.shape
    heads = w2 // (2 * SB_HEAD_DIM)
    nsub = ATTN_Q_TILES
    rows = nsub * TILE
    return pl.pallas_call(
        functools.partial(_sb_kernel, nsub=nsub, first_valid=TILE - N_META),
        out_shape=jax.ShapeDtypeStruct((b, seq, heads * SB_HEAD_DIM), BF16),
        grid=(b, heads, seq // rows),
        in_specs=_q_specs(nsub, SB_HEAD_DIM, 0) + [
            pl.BlockSpec((1, lp, SB_HEAD_DIM), lambda bi, h, i: (bi, 0, heads + h)),
            pl.BlockSpec((lp // TILE, SB_HEAD_DIM, TILE), lambda bi, h, i: (bi, h, 0)),
        ],
        out_specs=pl.BlockSpec((1, rows, SB_HEAD_DIM), lambda bi, h, i: (bi, i, h)),
        scratch_shapes=[pltpu.VMEM((1, rows), F32), pltpu.VMEM((SB_HEAD_DIM, rows), F32)],
        compiler_params=_cparams(("parallel", "parallel", "arbitrary")),
        name="sb_attention",
    )(*([qk] * (nsub + 1)), vt)


def _mla_kernel(*refs, nsub, first_valid):
    q_refs = refs[:nsub]
    k_ref, vt_ref, bias_ref, o_ref, m_ref, l_ref, acc_ref, s0_ref, s1_ref, p0_ref, p1_ref = refs[nsub:]
    qs = pl.program_id(2)
    rows = nsub * TILE
    blk_keys = 2 * TILE
    n_full = (nsub * qs) // 2

    def queries():
        return jnp.concatenate([q_refs[c][0] for c in range(nsub)], axis=0)

    def scores(blk, s_ref):
        kb = k_ref[0, pl.ds(pl.multiple_of((1 + 2 * blk) * TILE, TILE), blk_keys), :]
        s_ref[...] = lax.dot_general(kb, queries(), (((1,), (1,)), ((), ())), preferred_element_type=F32)

    def softmax(s_ref, p_ref, bias):
        s = s_ref[...] if bias is None else s_ref[...] + bias
        m_old = m_ref[...]
        m_new = jnp.maximum(m_old, jnp.max(s, axis=0, keepdims=True))
        alpha = jnp.exp2(m_old - m_new)
        p = jnp.exp2(s - m_new)
        l_ref[...] = alpha * l_ref[...] + jnp.sum(p, axis=0, keepdims=True)
        m_ref[...] = m_new
        p_ref[...] = p.astype(BF16)
        return alpha

    def values(blk, p_ref):
        j = 1 + 2 * blk
        vtb = jnp.concatenate([vt_ref[j], vt_ref[j + 1]], axis=1)
        return jnp.dot(vtb, p_ref[...], preferred_element_type=F32)

    def stage(blk, s_cur, s_nxt, p_cur, p_prev, bias=None, lookahead=True):
        if lookahead:
            scores(blk + 1, s_nxt)
        alpha = softmax(s_cur, p_cur, bias)
        acc_ref[...] = alpha * (acc_ref[...] + values(jnp.maximum(blk - 1, 0), p_prev))

    s = lax.dot_general(k_ref[0, first_valid:TILE, :], queries(), (((1,), (1,)), ((), ())),
                        preferred_element_type=F32)
    m0 = jnp.max(s, axis=0, keepdims=True)
    p = jnp.exp2(s - m0)
    m_ref[...] = m0
    l_ref[...] = jnp.sum(p, axis=0, keepdims=True)
    p_tile = jnp.concatenate([jnp.zeros((first_valid, rows), BF16), p.astype(BF16)], axis=0)
    acc_ref[...] = jnp.dot(vt_ref[0], p_tile, preferred_element_type=F32)

    scores(0, s0_ref)
    p1_ref[...] = jnp.zeros_like(p1_ref)

    def body(u, carry):
        stage(2 * u, s0_ref, s1_ref, p0_ref, p1_ref)
        stage(2 * u + 1, s1_ref, s0_ref, p1_ref, p0_ref)
        return carry

    lax.fori_loop(0, n_full // 2, body, 0)
    stage(n_full, s0_ref, s1_ref, p0_ref, p1_ref, bias=bias_ref[:blk_keys, :])
    stage(n_full + 1, s1_ref, s0_ref, p1_ref, p0_ref, bias=bias_ref[blk_keys:, :], lookahead=False)
    acc = acc_ref[...] + values(n_full + 1, p1_ref)
    o_ref[0] = (acc / l_ref[...]).T.astype(o_ref.dtype)


def _mla_attention(q, k, vt, seq):
    b, lp, _ = q.shape
    heads = vt.shape[1] // MLA_V_DIM
    nsub = ATTN_Q_TILES
    assert nsub == 4, "the kernel visits the query-overlapping keys as exactly two 2-tile blocks"
    rows = nsub * TILE
    shift = CHUNK.bit_length() - 1
    key = lax.broadcasted_iota(jnp.int32, (rows, rows), 0)
    qry = lax.broadcasted_iota(jnp.int32, (rows, rows), 1)
    bias = jnp.where((key >> shift) <= (qry >> shift), 0.0, NEG_BIG).astype(F32)
    return pl.pallas_call(
        functools.partial(_mla_kernel, nsub=nsub, first_valid=TILE - N_META),
        out_shape=jax.ShapeDtypeStruct((b, seq, heads * MLA_V_DIM), BF16),
        grid=(b, heads, seq // rows),
        in_specs=_q_specs(nsub, MLA_QK_PAD, 0) + [
            pl.BlockSpec((1, lp, MLA_QK_PAD), lambda bi, h, i: (bi, 0, h)),
            pl.BlockSpec((lp // TILE, MLA_V_DIM, TILE), lambda bi, h, i: (bi, h, 0)),
            pl.BlockSpec((rows, rows), lambda bi, h, i: (0, 0), pipeline_mode=pl.Buffered(1)),
        ],
        out_specs=pl.BlockSpec((1, rows, MLA_V_DIM), lambda bi, h, i: (bi, i, h)),
        scratch_shapes=[pltpu.VMEM((1, rows), F32), pltpu.VMEM((1, rows), F32),
                        pltpu.VMEM((MLA_V_DIM, rows), F32),
                        pltpu.VMEM((2 * TILE, rows), F32), pltpu.VMEM((2 * TILE, rows), F32),
                        pltpu.VMEM((2 * TILE, rows), BF16), pltpu.VMEM((2 * TILE, rows), BF16)],
        compiler_params=_cparams(("parallel", "parallel", "arbitrary")),
        name="mla_attention",
    )(*([q] * nsub), k, vt, bias)


def _merge_kernel(oa_ref, ob_ref, ga_ref, gb_ref, x_ref, wpa_ref, wpb_ref, wo_ref, o_ref):
    pa = jnp.dot(oa_ref[0], wpa_ref[...], preferred_element_type=F32)
    pb = jnp.dot(ob_ref[0], wpb_ref[...], preferred_element_type=F32)
    y = ga_ref[0] * pa + gb_ref[0] * pb
    o_ref[0] = x_ref[0] + jnp.dot(y.astype(BF16), wo_ref[...], preferred_element_type=F32)


def _merge(oa, ob, gates, x, wpa, wpb, wo):
    b, s, d = x.shape
    nt = s // TILE
    resident = lambda w: pl.BlockSpec(w.shape, lambda bi, i: (0, 0), pipeline_mode=pl.Buffered(1))
    return pl.pallas_call(
        _merge_kernel,
        out_shape=jax.ShapeDtypeStruct((b, s, d), F32),
        grid=(b, nt),
        in_specs=[
            pl.BlockSpec((1, TILE, oa.shape[2]), lambda bi, i: (bi, i, 0)),
            pl.BlockSpec((1, TILE, ob.shape[2]), lambda bi, i: (bi, i, 0)),
            pl.BlockSpec((1, TILE, d), lambda bi, i: (bi, i + 1, 0)),
            pl.BlockSpec((1, TILE, d), lambda bi, i: (bi, i + 1, 1)),
            pl.BlockSpec((1, TILE, d), lambda bi, i: (bi, i, 0)),
            resident(wpa), resident(wpb), resident(wo),
        ],
        out_specs=pl.BlockSpec((1, TILE, d), lambda bi, i: (bi, i, 0)),
        compiler_params=_cparams(("parallel", "parallel")),
        name="merge",
    )(oa, ob, gates, gates, x, wpa, wpb, wo)


def _split3(a):
    hi = a.astype(BF16)
    r1 = a - hi.astype(F32)
    mid = r1.astype(BF16)
    lo = (r1 - mid.astype(F32)).astype(BF16)
    return hi, mid, lo


def _route_kernel(h_ref, g_ref, wr_ref, br_ref, hp_ref, r_ref, cnt_ref, carry_ref, *, tm, half):
    i = pl.program_id(0)

    @pl.when(i == 0)
    def _():
        carry_ref[...] = jnp.zeros_like(carry_ref)

    hn = _rms(h_ref[...], g_ref[...])

    lo_bits = pltpu.bitcast(hn[:, :half].astype(BF16).astype(F32), jnp.uint32)
    hi_bits = pltpu.bitcast(hn[:, half:].astype(BF16).astype(F32), jnp.uint32)
    hp_ref[...] = (hi_bits & jnp.uint32(0xFFFF0000)) | (lo_bits >> 16)

    a_hi, a_mid, a_lo = _split3(hn)
    w_hi, w_mid, w_lo = wr_ref[0], wr_ref[1], wr_ref[2]
    dot = lambda a, w: jnp.dot(a, w, preferred_element_type=F32)
    lg = (dot(a_lo, w_hi) + dot(a_hi, w_lo) + dot(a_mid, w_mid)
          + dot(a_mid, w_hi) + dot(a_hi, w_mid) + dot(a_hi, w_hi)) + br_ref[...]

    lane = lax.broadcasted_iota(jnp.int32, lg.shape, 1)
    rmax = lambda v: jnp.max(v, axis=1, keepdims=True)
    rmin = lambda v: jnp.min(v, axis=1, keepdims=True)
    rsum = lambda v: jnp.sum(v, axis=1, keepdims=True)

    gmask = lane < N_GROUPS
    gl = jnp.where(gmask, lg, -jnp.inf)
    gmax = rmax(gl)
    gsel = rmin(jnp.where(gl == gmax, lane, LANES))
    p_g = 1.0 / rsum(jnp.where(gmask, jnp.exp(lg - gmax), 0.0))

    e_lo = EXPERT_LANE0 + gsel * EXPERTS_PER_GROUP
    emask = (lane >= e_lo) & (lane < e_lo + EXPERTS_PER_GROUP)
    emax = rmax(jnp.where(emask, lg, -jnp.inf))
    ex = jnp.where(emask, jnp.exp(lg - emax), 0.0)
    prob = jnp.where(emask, ex / rsum(ex), -1.0)
    top1 = rmax(prob)
    i1 = rmin(jnp.where(prob == top1, lane, LANES))
    prob2 = jnp.where(lane == i1, -1.0, prob)
    top2 = rmax(prob2)
    i2 = rmin(jnp.where(prob2 == top2, lane, LANES))
    denom = top1 + top2
    w1 = p_g * top1 / denom
    w2 = p_g * top2 / denom

    sel = ((lane == i1) | (lane == i2))
    row = lax.broadcasted_iota(jnp.int32, (tm, tm), 0)
    col = lax.broadcasted_iota(jnp.int32, (tm, tm), 1)
    before = (col < row).astype(BF16)
    prefix = dot(before, jnp.where(sel, 1.0, 0.0).astype(BF16)) + carry_ref[...]
    rank1 = rsum(jnp.where(lane == i1, prefix, 0.0))
    rank2 = rsum(jnp.where(lane == i2, prefix, 0.0))
    carry_ref[...] = carry_ref[...] + jnp.sum(jnp.where(sel, 1.0, 0.0), axis=0, keepdims=True)
    cnt_ref[...] = carry_ref[...]

    e1 = (i1 - EXPERT_LANE0).astype(F32)
    e2 = (i2 - EXPERT_LANE0).astype(F32)
    out = jnp.zeros(lg.shape, F32)
    for k, val in enumerate((e1, e2, w1, w2, rank1, rank2)):
        out = jnp.where(lane == k, val, out)
    r_ref[...] = out


def _route(h1, g, wr3, br):
    n, d = h1.shape
    tm = TILE
    return pl.pallas_call(
        functools.partial(_route_kernel, tm=tm, half=d // 2),
        out_shape=(jax.ShapeDtypeStruct((n, d // 2), jnp.uint32),
                   jax.ShapeDtypeStruct((n, LANES), F32),
                   jax.ShapeDtypeStruct((1, LANES), F32)),
        grid=(n // tm,),
        in_specs=[
            pl.BlockSpec((tm, d), lambda i: (i, 0)),
            pl.BlockSpec((1, d), lambda i: (0, 0)),
            pl.BlockSpec(wr3.shape, lambda i: (0, 0, 0)),
            pl.BlockSpec((1, LANES), lambda i: (0, 0)),
        ],
        out_specs=(pl.BlockSpec((tm, d // 2), lambda i: (i, 0)),
                   pl.BlockSpec((tm, LANES), lambda i: (i, 0)),
                   pl.BlockSpec((1, LANES), lambda i: (0, 0))),
        scratch_shapes=[pltpu.VMEM((1, LANES), F32)],
        compiler_params=_cparams(("arbitrary",)),
        name="route",
    )(h1, g.reshape(1, d), wr3, br)


def _dispatch_kernel(d0_ref, d1_ref, src_ref, init_ref, dst_ref, sem, *, tm):
    del init_ref
    base = pl.program_id(0) * tm

    def copy(r, dest_ref, s):
        return pltpu.make_async_copy(src_ref.at[pl.ds(r, 1)], dst_ref.at[pl.ds(dest_ref[base + r], 1)], sem.at[s])

    def start(r, c):
        copy(r, d0_ref, 0).start()
        copy(r, d1_ref, 1).start()
        return c

    def wait(r, c):
        copy(r, d0_ref, 0).wait()
        copy(r, d1_ref, 1).wait()
        return c

    lax.fori_loop(0, tm, start, 0)
    lax.fori_loop(0, tm, wait, 0)


def _dispatch(dest0, dest1, hp, p_rows):
    n, w = hp.shape
    tm = TILE
    init = jnp.zeros((p_rows, w), hp.dtype)
    return pl.pallas_call(
        functools.partial(_dispatch_kernel, tm=tm),
        out_shape=jax.ShapeDtypeStruct((p_rows, w), hp.dtype),
        grid_spec=pltpu.PrefetchScalarGridSpec(
            num_scalar_prefetch=2,
            grid=(n // tm,),
            in_specs=[pl.BlockSpec((tm, w), lambda i, a, b: (i, 0)), pl.BlockSpec(memory_space=pl.ANY)],
            out_specs=pl.BlockSpec(memory_space=pl.ANY),
            scratch_shapes=[pltpu.SemaphoreType.DMA((2,))],
        ),
        input_output_aliases={3: 0},
        compiler_params=_cparams(("arbitrary",)),
        name="dispatch",
    )(dest0, dest1, hp, init)


def _expert_kernel(be_ref, nu_ref, x_ref, w1_ref, w3_ref, w2_ref, y_ref):
    del be_ref

    @pl.when(pl.program_id(0) < nu_ref[0])
    def _():
        xw = x_ref[...]
        lo = pltpu.bitcast(xw << 16, F32).astype(BF16)
        hi = pltpu.bitcast(xw & jnp.uint32(0xFFFF0000), F32).astype(BF16)
        xb = jnp.concatenate([lo, hi], axis=1)
        a = jnp.dot(xb, w1_ref[0], preferred_element_type=F32)
        g = jnp.dot(xb, w3_ref[0], preferred_element_type=F32)
        hid = (a * (1.0 / (1.0 + jnp.exp(-a))) * g).astype(BF16)
        y_ref[...] = jnp.dot(hid, w2_ref[0], preferred_element_type=F32)

    @pl.when(pl.program_id(0) >= nu_ref[0])
    def _():
        y_ref[...] = jnp.zeros_like(y_ref)


def _experts(block_e, n_used, xs, w1, w3, w2):
    p_rows, half = xs.shape
    _, d, ff = w1.shape
    nb = p_rows // ROUTE_BLOCK
    last = lambda j, nu: jnp.minimum(j, nu[0] - 1)
    return pl.pallas_call(
        _expert_kernel,
        out_shape=jax.ShapeDtypeStruct((p_rows, d), F32),
        grid_spec=pltpu.PrefetchScalarGridSpec(
            num_scalar_prefetch=2,
            grid=(nb,),
            in_specs=[
                pl.BlockSpec((ROUTE_BLOCK, half), lambda j, be, nu: (last(j, nu), 0)),
                pl.BlockSpec((1, d, ff), lambda j, be, nu: (be[last(j, nu)], 0, 0)),
                pl.BlockSpec((1, d, ff), lambda j, be, nu: (be[last(j, nu)], 0, 0)),
                pl.BlockSpec((1, ff, d), lambda j, be, nu: (be[last(j, nu)], 0, 0)),
            ],
            out_specs=pl.BlockSpec((ROUTE_BLOCK, d), lambda j, be, nu: (j, 0)),
        ),
        compiler_params=_cparams(("arbitrary",)),
        name="experts",
    )(block_e, n_used, xs, w1, w3, w2)


def _combine_kernel(d0_ref, d1_ref, h_ref, r_ref, g_ref, y_ref, o_ref, ya, yb, sem, *, tm):
    base = pl.program_id(0) * tm

    def copy(r, dest_ref, buf, s):
        return pltpu.make_async_copy(y_ref.at[pl.ds(dest_ref[base + r], 1)], buf.at[pl.ds(r, 1)], sem.at[s])

    def start(r, c):
        copy(r, d0_ref, ya, 0).start()
        copy(r, d1_ref, yb, 1).start()
        return c

    def wait(r, c):
        copy(r, d0_ref, ya, 0).wait()
        copy(r, d1_ref, yb, 1).wait()
        return c

    lax.fori_loop(0, tm, start, 0)
    lax.fori_loop(0, tm, wait, 0)

    route = r_ref[...]
    lane = lax.broadcasted_iota(jnp.int32, route.shape, 1)
    w1 = jnp.sum(jnp.where(lane == 2, route, 0.0), axis=1, keepdims=True)
    w2 = jnp.sum(jnp.where(lane == 3, route, 0.0), axis=1, keepdims=True)
    h2 = h_ref[...] + (ya[...] * w1 + yb[...] * w2)
    o_ref[...] = _rms(h2, g_ref[...])


def _combine(dest0, dest1, h1, route, g, y):
    n, d = h1.shape
    tm = TILE
    return pl.pallas_call(
        functools.partial(_combine_kernel, tm=tm),
        out_shape=jax.ShapeDtypeStruct((n, d), F32),
        grid_spec=pltpu.PrefetchScalarGridSpec(
            num_scalar_prefetch=2,
            grid=(n // tm,),
            in_specs=[
                pl.BlockSpec((tm, d), lambda i, a, b: (i, 0)),
                pl.BlockSpec((tm, LANES), lambda i, a, b: (i, 0)),
                pl.BlockSpec((1, d), lambda i, a, b: (0, 0)),
                pl.BlockSpec(memory_space=pl.ANY),
            ],
            out_specs=pl.BlockSpec((tm, d), lambda i, a, b: (i, 0)),
            scratch_shapes=[pltpu.VMEM((tm, d), F32), pltpu.VMEM((tm, d), F32),
                            pltpu.SemaphoreType.DMA((2,))],
        ),
        compiler_params=_cparams(("arbitrary",)),
        name="combine",
    )(dest0, dest1, h1, route, g.reshape(1, d), y)


def _rope_tables(lp):
    half = MLA_ROPE_DIM // 2
    inv = ROPE_THETA ** (-jnp.arange(half, dtype=F32) / half)
    pos = (jnp.arange(lp) - (TILE - N_META)).astype(F32)
    ang = pos[:, None] * inv[None, :]
    cos, sin = jnp.cos(ang), jnp.sin(ang)
    z32 = jnp.zeros((lp, half), F32)
    z64 = jnp.zeros((lp, LANES - MLA_ROPE_DIM), F32)
    return (jnp.concatenate([cos, cos, z64], axis=1),
            jnp.concatenate([-sin, z32, z64], axis=1),
            jnp.concatenate([z32, sin, z64], axis=1))


def kernel(x, meta_tokens, norm1_g, w_in, b_gate, kv_norm_g, w_uk, w_uv, w_proj_a, w_proj_b, w_out,
           norm2_g, w_route_group, b_route_group, w_route_expert, b_route_expert, w1, w3, w2, final_g):
    b, seq, d = x.shape
    assert seq % TILE == 0 and TILE % CHUNK == 0 and N_META <= TILE
    lp = TILE + seq
    n_tok = b * lp
    n_real = b * seq
    n_exp = N_GROUPS * EXPERTS_PER_GROUP
    sb_w = SB_HEADS * SB_HEAD_DIM
    qk_dim = MLA_NOPE_DIM + MLA_ROPE_DIM
    mq_w = MLA_HEADS * qk_dim
    sb_scale = SB_HEAD_DIM ** -0.5
    mla_scale = qk_dim ** -0.5

    wi = w_in[0]
    o_q = 3 * sb_w
    o_c = o_q + mq_w
    o_r = o_c + MLA_KV_RANK
    o_g = o_r + MLA_ROPE_DIM
    w_sb = wi[:, :2 * sb_w].astype(BF16)
    w_sbv_t = wi[:, 2 * sb_w:o_q].T.astype(BF16)
    w_mq = jnp.pad(wi[:, o_q:o_c].reshape(d, MLA_HEADS, qk_dim),
                   ((0, 0), (0, 0), (0, MLA_QK_PAD - qk_dim))).reshape(d, MLA_HEADS * MLA_QK_PAD).astype(BF16)
    w_c = jnp.pad(wi[:, o_c:o_g], ((0, 0), (0, LANES - MLA_ROPE_DIM))).astype(BF16)
    w_g = wi[:, o_g:].astype(BF16)
    sb_colscale = jnp.concatenate([jnp.full((1, sb_w), sb_scale * LOG2E, F32), jnp.ones((1, sb_w), F32)], axis=1)
    wr = jnp.zeros((d, LANES), F32)
    wr = wr.at[:, :N_GROUPS].set(w_route_group[0]).at[:, EXPERT_LANE0:EXPERT_LANE0 + n_exp].set(w_route_expert[0])
    wr_hi = wr.astype(BF16)
    wr_mid = (wr - wr_hi.astype(F32)).astype(BF16)
    wr_lo = (wr - wr_hi.astype(F32) - wr_mid.astype(F32)).astype(BF16)
    wr3 = jnp.stack([wr_hi, wr_mid, wr_lo])
    br = jnp.zeros((1, LANES), F32)
    br = br.at[0, :N_GROUPS].set(b_route_group[0]).at[0, EXPERT_LANE0:EXPERT_LANE0 + n_exp].set(b_route_expert[0])

    head = jnp.concatenate([jnp.zeros((TILE - N_META, d), F32), meta_tokens.astype(F32)], axis=0)
    hn = _norm1(x, head, norm1_g[0]).reshape(n_tok, d)
    tm = _row_tile(lp)
    tpb = lp // tm
    tables = _rope_tables(lp)
    tab_spec = pl.BlockSpec((tm, LANES), lambda j, i: (i % tpb, 0))
    row_spec = lambda tn: pl.BlockSpec((1, tn), lambda j, i: (0, j))

    sb_qk = _proj(_proj_scale_kernel, hn, w_sb, [sb_colscale], [row_spec(sb_w)], BF16, tm, sb_w)
    sb_vt = _proj_t(hn, w_sbv_t, tm)
    q_mla = _proj(functools.partial(_proj_mlaq_kernel, scale=mla_scale * LOG2E), hn, w_mq,
                  list(tables), [tab_spec] * 3, BF16, tm, min(4, MLA_HEADS) * MLA_QK_PAD)
    ckr = _proj(_proj_scale_kernel, hn, w_c, [jnp.ones((1, w_c.shape[1]), F32)],
                [row_spec(w_c.shape[1])], F32, tm, w_c.shape[1])
    gates = _proj(_proj_gate_kernel, hn, w_g, [b_gate[0].reshape(1, 2 * d)], [row_spec(d)], F32, tm, d)
    k_mla, vt_mla = _kvup(ckr, kv_norm_g[0], w_uk[0].astype(BF16), w_uv[0].T.astype(BF16), tables, tm, tpb)

    o_a = _sb_attention(sb_qk.reshape(b, lp, 2 * sb_w), sb_vt, seq)
    o_b = _mla_attention(q_mla.reshape(b, lp, -1), k_mla.reshape(b, lp, -1), vt_mla, seq)
    h1 = _merge(o_a, o_b, gates.reshape(b, lp, 2 * d), x,
                w_proj_a[0].astype(BF16), w_proj_b[0].astype(BF16), w_out[0].astype(BF16)).reshape(n_real, d)

    hp, route, counts = _route(h1, norm2_g[0], wr3, br)
    cnt = counts[0, EXPERT_LANE0:EXPERT_LANE0 + n_exp].astype(jnp.int32)
    padded = (cnt + ROUTE_BLOCK - 1) // ROUTE_BLOCK * ROUTE_BLOCK
    pends = jnp.cumsum(padded)
    pstarts = pends - padded
    n_blocks = (n_real * TOP_K + n_exp * (ROUTE_BLOCK - 1) + ROUTE_BLOCK - 1) // ROUTE_BLOCK
    n_used = (pends[-1] // ROUTE_BLOCK).astype(jnp.int32).reshape(1)
    blk_start = jnp.minimum(jnp.arange(n_blocks), n_used[0] - 1) * ROUTE_BLOCK
    block_e = jnp.minimum(jnp.sum(pends[None, :] <= blk_start[:, None], axis=1), n_exp - 1).astype(jnp.int32)
    ids = route[:, :6].astype(jnp.int32)
    expert_iota = jnp.arange(n_exp, dtype=jnp.int32)[None, :]
    start_of = lambda e: jnp.sum(jnp.where(e[:, None] == expert_iota, pstarts[None, :], 0), axis=1)
    dest0 = (start_of(ids[:, 0]) + ids[:, 4]).astype(jnp.int32)
    dest1 = (start_of(ids[:, 1]) + ids[:, 5]).astype(jnp.int32)

    xs = _dispatch(dest0, dest1, hp, n_blocks * ROUTE_BLOCK)
    y = _experts(block_e, n_used, xs, w1[0].astype(BF16), w3[0].astype(BF16), w2[0].astype(BF16))
    out = _combine(dest0, dest1, h1, route, final_g, y)
    return out.reshape(b, seq, d)
```

```python
import functools

import jax
import jax.numpy as jnp
from jax import lax
from jax.experimental import pallas as pl
from jax.experimental.pallas import tpu as pltpu

N_META = 16
CHUNK = 64
SB_HEADS = 8
SB_HEAD_DIM = 128
MLA_HEADS = 16
MLA_NOPE_DIM = 128
MLA_ROPE_DIM = 64
MLA_V_DIM = 128
MLA_KV_RANK = 512
ROPE_THETA = 10000.0
N_GROUPS = 4
EXPERTS_PER_GROUP = 8
TOP_K = 2
ROUTE_BLOCK = 256
RMS_EPS = 1e-6

TILE = 256
LANES = 128
MLA_QK_PAD = 256
ATTN_Q_TILES = 4
EXPERT_LANE0 = 8
VMEM_LIMIT = 56 * 1024 * 1024
NEG_BIG = -1e30
DMA_ISSUE_UNROLL = 8
DENOM_ROWS = 16
STICK_GONE_LOG2 = 152.0

F32 = jnp.float32
BF16 = jnp.bfloat16
LOG2E = 1.4426950408889634


def _cparams(sem):
    return pltpu.CompilerParams(dimension_semantics=sem, vmem_limit_bytes=VMEM_LIMIT)


def _rms(v, g):
    ms = jnp.mean(v * v, axis=-1, keepdims=True)
    return v * lax.rsqrt(ms + RMS_EPS) * g


def _row_tile(lp):
    for t in (768, 512, 256):
        if lp % t == 0:
            return t
    raise ValueError(lp)


def _norm1_kernel(x_ref, head_ref, g_ref, o_ref):
    i = pl.program_id(1)

    @pl.when(i == 0)
    def _():
        o_ref[0] = _rms(head_ref[...], g_ref[...]).astype(BF16)

    @pl.when(i > 0)
    def _():
        o_ref[0] = _rms(x_ref[0], g_ref[...]).astype(BF16)


def _norm1(x, head, g):
    b, s, d = x.shape
    nt = s // TILE + 1
    return pl.pallas_call(
        _norm1_kernel,
        out_shape=jax.ShapeDtypeStruct((b, nt * TILE, d), BF16),
        grid=(b, nt),
        in_specs=[
            pl.BlockSpec((1, TILE, d), lambda bi, i: (bi, jnp.maximum(i - 1, 0), 0)),
            pl.BlockSpec((TILE, d), lambda bi, i: (0, 0)),
            pl.BlockSpec((1, d), lambda bi, i: (0, 0)),
        ],
        out_specs=pl.BlockSpec((1, TILE, d), lambda bi, i: (bi, i, 0)),
        compiler_params=_cparams(("parallel", "parallel")),
        name="norm1",
    )(x, head, g.reshape(1, d))


def _rope_rows(r, cos_t, sin_a, sin_b):
    return r * cos_t + pltpu.roll(r, 96, 1) * sin_a + pltpu.roll(r, 32, 1) * sin_b


def _proj_scale_kernel(x_ref, w_ref, s_ref, o_ref):
    acc = jnp.dot(x_ref[...], w_ref[...], preferred_element_type=F32)
    o_ref[...] = (acc * s_ref[...]).astype(o_ref.dtype)


def _proj_gate_kernel(x_ref, w_ref, b_ref, o_ref):
    acc = jnp.dot(x_ref[...], w_ref[...], preferred_element_type=F32) + b_ref[...]
    o_ref[...] = 1.0 / (1.0 + jnp.exp(-acc))


def _proj_mlaq_kernel(x_ref, w_ref, cos_ref, sa_ref, sb_ref, o_ref, *, scale):
    acc = jnp.dot(x_ref[...], w_ref[...], preferred_element_type=F32)
    cos_t, sin_a, sin_b = cos_ref[...], sa_ref[...], sb_ref[...]
    for hh in range(acc.shape[1] // MLA_QK_PAD):
        c0 = hh * MLA_QK_PAD
        nope = acc[:, c0:c0 + LANES] * scale
        rope = _rope_rows(acc[:, c0 + LANES:c0 + 2 * LANES], cos_t, sin_a, sin_b) * scale
        o_ref[:, c0:c0 + LANES] = nope.astype(o_ref.dtype)
        o_ref[:, c0 + LANES:c0 + 2 * LANES] = rope.astype(o_ref.dtype)


def _store_lane_tiles(o_ref, val_t):
    for c in range(val_t.shape[1] // TILE):
        o_ref[c] = val_t[:, c * TILE:(c + 1) * TILE].astype(o_ref.dtype)


def _proj_t_kernel(x_ref, wt_ref, o_ref):
    acc_t = lax.dot_general(wt_ref[...], x_ref[...], (((1,), (1,)), ((), ())), preferred_element_type=F32)
    _store_lane_tiles(o_ref, acc_t)


def _proj_t(hn2d, wt, tm):
    m, k = hn2d.shape
    n = wt.shape[0]
    return pl.pallas_call(
        _proj_t_kernel,
        out_shape=jax.ShapeDtypeStruct((m // TILE, n, TILE), BF16),
        grid=(m // tm,),
        in_specs=[pl.BlockSpec((tm, k), lambda i: (i, 0)), pl.BlockSpec((n, k), lambda i: (0, 0))],
        out_specs=pl.BlockSpec((tm // TILE, n, TILE), lambda i: (i, 0, 0)),
        compiler_params=_cparams(("parallel",)),
        name="proj_t",
    )(hn2d, wt)


def _proj(kernel, hn2d, w, extras, extra_specs, out_dtype, tm, tn):
    m, k = hn2d.shape
    n = w.shape[1]
    return pl.pallas_call(
        kernel,
        out_shape=jax.ShapeDtypeStruct((m, n), out_dtype),
        grid=(n // tn, m // tm),
        in_specs=[
            pl.BlockSpec((tm, k), lambda j, i: (i, 0)),
            pl.BlockSpec((k, tn), lambda j, i: (0, j)),
        ] + extra_specs,
        out_specs=pl.BlockSpec((tm, tn), lambda j, i: (i, j)),
        compiler_params=_cparams(("parallel", "parallel")),
        name=getattr(kernel, "__name__", None) or kernel.func.__name__,
    )(hn2d, w, *extras)


def _kvup_kernel(c_ref, g_ref, wk_ref, wvt_ref, cos_ref, sa_ref, sb_ref, k_ref, vt_ref, *, rank):
    ckr = c_ref[...]
    cn = _rms(ckr[:, :rank], g_ref[...]).astype(BF16)
    kn = jnp.dot(cn, wk_ref[...], preferred_element_type=F32)
    vv_t = lax.dot_general(wvt_ref[...], cn, (((1,), (1,)), ((), ())), preferred_element_type=F32)
    rope = _rope_rows(ckr[:, rank:rank + LANES], cos_ref[...], sa_ref[...], sb_ref[...]).astype(BF16)
    for h in range(kn.shape[1] // MLA_NOPE_DIM):
        k_ref[:, h * MLA_QK_PAD:h * MLA_QK_PAD + LANES] = kn[:, h * LANES:(h + 1) * LANES].astype(BF16)
        k_ref[:, h * MLA_QK_PAD + LANES:(h + 1) * MLA_QK_PAD] = rope
    _store_lane_tiles(vt_ref, vv_t)


def _kvup(ckr, g, wk, wvt, tables, tm, tiles_per_batch):
    m, cw = ckr.shape
    rank = wk.shape[0]
    nk = wk.shape[1]
    nv = wvt.shape[0]
    heads = nk // MLA_NOPE_DIM
    tab_spec = pl.BlockSpec((tm, LANES), lambda i: (i % tiles_per_batch, 0))
    return pl.pallas_call(
        functools.partial(_kvup_kernel, rank=rank),
        out_shape=(jax.ShapeDtypeStruct((m, heads * MLA_QK_PAD), BF16),
                   jax.ShapeDtypeStruct((m // TILE, nv, TILE), BF16)),
        grid=(m // tm,),
        in_specs=[
            pl.BlockSpec((tm, cw), lambda i: (i, 0)),
            pl.BlockSpec((1, rank), lambda i: (0, 0)),
            pl.BlockSpec(wk.shape, lambda i: (0, 0)),
            pl.BlockSpec(wvt.shape, lambda i: (0, 0)),
            tab_spec, tab_spec, tab_spec,
        ],
        out_specs=(pl.BlockSpec((tm, heads * MLA_QK_PAD), lambda i: (i, 0)),
                   pl.BlockSpec((tm // TILE, nv, TILE), lambda i: (i, 0, 0))),
        compiler_params=_cparams(("parallel",)),
        name="kvup",
    )(ckr, g.reshape(1, rank), wk, wvt, *tables)


def _softplus2(z):
    neg_abs = pltpu.bitcast(pltpu.bitcast(z, jnp.uint32) | jnp.uint32(0x80000000), F32)
    return jnp.maximum(z, 0.0) + jnp.log2(1.0 + jnp.exp2(neg_abs))


def _sb_kernel(*refs, nsub, first_valid):
    q_refs = refs[:nsub]
    k_ref, vt_ref, o_ref, carry_ref, acc_ref = refs[nsub:]
    qs = pl.program_id(2)
    first_tile = 1 + nsub * qs
    carry_ref[...] = jnp.zeros_like(carry_ref)
    acc_ref[...] = jnp.zeros_like(acc_ref)
    trow = lax.broadcasted_iota(jnp.int32, (TILE, 2 * TILE), 0)
    tcol = lax.broadcasted_iota(jnp.int32, (TILE, 2 * TILE), 1) & (TILE - 1)
    tri2 = jnp.where(tcol >= trow, 1.0, 0.0).astype(BF16)

    def update(c0, c1, j, mask):
        lanes = slice(c0 * TILE, c1 * TILE)
        q = jnp.concatenate([q_refs[c][0] for c in range(c0, c1)], axis=0)
        kb = k_ref[0, pl.ds(pl.multiple_of(j * TILE, TILE), TILE), :]
        z = lax.dot_general(kb, q, (((1,), (1,)), ((), ())), preferred_element_type=F32)
        sp = _softplus2(z)
        if mask is not None:
            sp = jnp.where(mask, sp, 0.0)
        hi32 = pltpu.bitcast(pltpu.bitcast(sp, jnp.uint32) & jnp.uint32(0xFFFF0000), F32)
        parts = jnp.concatenate([hi32.astype(BF16), (sp - hi32).astype(BF16)], axis=0)
        cs = jnp.dot(tri2, parts, preferred_element_type=F32)
        carry = carry_ref[:, lanes]
        a = jnp.exp2(z - cs - carry)
        if mask is not None:
            a = jnp.where(mask, a, 0.0)
        acc_ref[:, lanes] = acc_ref[:, lanes] + jnp.dot(vt_ref[j], a.astype(BF16), preferred_element_type=F32)
        carry_ref[:, lanes] = carry + cs[0:1, :]

    def stick_left(c0, c1):
        return jnp.min(carry_ref[:, c0 * TILE:c1 * TILE]) < STICK_GONE_LOG2

    def diag_mask(nq):
        key = lax.broadcasted_iota(jnp.int32, (TILE, nq * TILE), 0)
        qry = lax.broadcasted_iota(jnp.int32, (TILE, nq * TILE), 1)
        return (qry >= TILE) | (key < qry)

    half = nsub // 2
    for c in reversed(range(nsub)):
        near = min(c + half, nsub)
        update(c, near, first_tile + c, diag_mask(near - c))
        if near < nsub:
            @pl.when(stick_left(near, nsub))
            def _():
                update(near, nsub, first_tile + c, None)

    for c0, c1 in ((0, half), (half, nsub)):
        def body(state):
            j, _ = state
            update(c0, c1, j, None)
            return j - 1, stick_left(c0, c1)

        _, alive = lax.while_loop(lambda st: (st[0] > 0) & st[1], body, (first_tile - 1, stick_left(c0, c1)))

        @pl.when(alive)
        def _():
            key0 = lax.broadcasted_iota(jnp.int32, (TILE, (c1 - c0) * TILE), 0)
            update(c0, c1, 0, key0 >= first_valid)

    o_ref[0] = acc_ref[...].T.astype(o_ref.dtype)


def _q_specs(nsub, width, col0):
    return [pl.BlockSpec((1, TILE, width),
                         functools.partial(lambda bi, h, i, c: (bi, nsub * i + 1 + c, col0 + h), c=c))
            for c in range(nsub)]


def _sb_attention(qk, vt, seq):
    b, lp, w2 = qk.shape
    heads = w2 // (2 * SB_HEAD_DIM)
    nsub = ATTN_Q_TILES
    rows = nsub * TILE
    return pl.pallas_call(
        functools.partial(_sb_kernel, nsub=nsub, first_valid=TILE - N_META),
        out_shape=jax.ShapeDtypeStruct((b, seq, heads * SB_HEAD_DIM), BF16),
        grid=(b, heads, seq // rows),
        in_specs=_q_specs(nsub, SB_HEAD_DIM, 0) + [
            pl.BlockSpec((1, lp, SB_HEAD_DIM), lambda bi, h, i: (bi, 0, heads + h)),
            pl.BlockSpec((lp // TILE, SB_HEAD_DIM, TILE), lambda bi, h, i: (bi, h, 0)),
        ],
        out_specs=pl.BlockSpec((1, rows, SB_HEAD_DIM), lambda bi, h, i: (bi, i, h)),
        scratch_shapes=[pltpu.VMEM((1, rows), F32), pltpu.VMEM((SB_HEAD_DIM, rows), F32)],
        compiler_params=_cparams(("parallel", "parallel", "arbitrary")),
        name="sb_attention",
    )(*([qk] * (nsub + 1)), vt)


def _mla_kernel(*refs, nsub, first_valid):
    q_refs = refs[:nsub]
    k_ref, vt_ref, bias_ref, o_ref, m_ref, acc_ref, s0_ref, s1_ref, p0_ref, p1_ref, x0_ref, x1_ref = refs[nsub:]
    qs = pl.program_id(2)
    rows = nsub * TILE
    blk_keys = 2 * TILE
    n_full = (nsub * qs) // 2

    def queries():
        return jnp.concatenate([q_refs[c][0] for c in range(nsub)], axis=0)

    def scores(blk, s_ref, smax_ref):
        kb = k_ref[0, pl.ds(pl.multiple_of((1 + 2 * blk) * TILE, TILE), blk_keys), :]
        s = lax.dot_general(kb, queries(), (((1,), (1,)), ((), ())), preferred_element_type=F32)
        s_ref[...] = s
        smax_ref[...] = jnp.max(s, axis=0, keepdims=True)

    def softmax(s_ref, smax_ref, p_ref, bias):
        if bias is None:
            s = s_ref[...]
            smax = smax_ref[...]
        else:
            s = s_ref[...] + bias
            smax = jnp.max(s, axis=0, keepdims=True)
        m_old = m_ref[...]
        m_new = jnp.maximum(m_old, smax)
        alpha = jnp.exp2(m_old - m_new)
        p = jnp.exp2(s - m_new)
        m_ref[...] = m_new
        p_ref[...] = p.astype(BF16)
        return alpha

    def with_ones(vtb):
        return jnp.concatenate([vtb, jnp.ones((DENOM_ROWS, vtb.shape[1]), BF16)], axis=0)

    def values(blk, p_ref):
        j = 1 + 2 * blk
        vtb = with_ones(jnp.concatenate([vt_ref[j], vt_ref[j + 1]], axis=1))
        return jnp.dot(vtb, p_ref[...], preferred_element_type=F32)

    def stage(blk, cur, nxt, bias=None, lookahead=True):
        (s_cur, x_cur, p_cur), (s_nxt, x_nxt, p_prev) = cur, nxt
        if lookahead:
            scores(blk + 1, s_nxt, x_nxt)
        alpha = softmax(s_cur, x_cur, p_cur, bias)
        acc_ref[...] = alpha * (acc_ref[...] + values(jnp.maximum(blk - 1, 0), p_prev))

    s = lax.dot_general(k_ref[0, first_valid:TILE, :], queries(), (((1,), (1,)), ((), ())),
                        preferred_element_type=F32)
    m0 = jnp.max(s, axis=0, keepdims=True)
    p = jnp.exp2(s - m0)
    m_ref[...] = m0
    p_tile = jnp.concatenate([jnp.zeros((first_valid, rows), BF16), p.astype(BF16)], axis=0)
    acc_ref[...] = jnp.dot(with_ones(vt_ref[0]), p_tile, preferred_element_type=F32)

    even = (s0_ref, x0_ref, p0_ref)
    odd = (s1_ref, x1_ref, p1_ref)
    scores(0, s0_ref, x0_ref)
    p1_ref[...] = jnp.zeros_like(p1_ref)

    def body(u, carry):
        stage(2 * u, even, odd)
        stage(2 * u + 1, odd, even)
        return carry

    lax.fori_loop(0, n_full // 2, body, 0)
    stage(n_full, even, odd, bias=bias_ref[:blk_keys, :])
    stage(n_full + 1, odd, even, bias=bias_ref[blk_keys:, :], lookahead=False)
    acc = acc_ref[...] + values(n_full + 1, p1_ref)
    o_ref[0] = (acc[:MLA_V_DIM] / acc[MLA_V_DIM:MLA_V_DIM + 1]).T.astype(o_ref.dtype)


def _mla_attention(q, k, vt, seq):
    b, lp, _ = q.shape
    heads = vt.shape[1] // MLA_V_DIM
    nsub = ATTN_Q_TILES
    assert nsub == 4, "the kernel visits the query-overlapping keys as exactly two 2-tile blocks"
    rows = nsub * TILE
    shift = CHUNK.bit_length() - 1
    key = lax.broadcasted_iota(jnp.int32, (rows, rows), 0)
    qry = lax.broadcasted_iota(jnp.int32, (rows, rows), 1)
    bias = jnp.where((key >> shift) <= (qry >> shift), 0.0, NEG_BIG).astype(F32)
    return pl.pallas_call(
        functools.partial(_mla_kernel, nsub=nsub, first_valid=TILE - N_META),
        out_shape=jax.ShapeDtypeStruct((b, seq, heads * MLA_V_DIM), BF16),
        grid=(b, heads, seq // rows),
        in_specs=_q_specs(nsub, MLA_QK_PAD, 0) + [
            pl.BlockSpec((1, lp, MLA_QK_PAD), lambda bi, h, i: (bi, 0, h)),
            pl.BlockSpec((lp // TILE, MLA_V_DIM, TILE), lambda bi, h, i: (bi, h, 0)),
            pl.BlockSpec((rows, rows), lambda bi, h, i: (0, 0), pipeline_mode=pl.Buffered(1)),
        ],
        out_specs=pl.BlockSpec((1, rows, MLA_V_DIM), lambda bi, h, i: (bi, i, h)),
        scratch_shapes=[pltpu.VMEM((1, rows), F32),
                        pltpu.VMEM((MLA_V_DIM + DENOM_ROWS, rows), F32),
                        pltpu.VMEM((2 * TILE, rows), F32), pltpu.VMEM((2 * TILE, rows), F32),
                        pltpu.VMEM((2 * TILE, rows), BF16), pltpu.VMEM((2 * TILE, rows), BF16),
                        pltpu.VMEM((1, rows), F32), pltpu.VMEM((1, rows), F32)],
        compiler_params=_cparams(("parallel", "parallel", "arbitrary")),
        name="mla_attention",
    )(*([q] * nsub), k, vt, bias)


def _merge_kernel(oa_ref, ob_ref, ga_ref, gb_ref, x_ref, wpa_ref, wpb_ref, wo_ref, o_ref):
    pa = jnp.dot(oa_ref[0], wpa_ref[...], preferred_element_type=F32)
    pb = jnp.dot(ob_ref[0], wpb_ref[...], preferred_element_type=F32)
    y = ga_ref[0] * pa + gb_ref[0] * pb
    o_ref[0] = x_ref[0] + jnp.dot(y.astype(BF16), wo_ref[...], preferred_element_type=F32)


def _merge(oa, ob, gates, x, wpa, wpb, wo):
    b, s, d = x.shape
    nt = s // TILE
    resident = lambda w: pl.BlockSpec(w.shape, lambda bi, i: (0, 0), pipeline_mode=pl.Buffered(1))
    return pl.pallas_call(
        _merge_kernel,
        out_shape=jax.ShapeDtypeStruct((b, s, d), F32),
        grid=(b, nt),
        in_specs=[
            pl.BlockSpec((1, TILE, oa.shape[2]), lambda bi, i: (bi, i, 0)),
            pl.BlockSpec((1, TILE, ob.shape[2]), lambda bi, i: (bi, i, 0)),
            pl.BlockSpec((1, TILE, d), lambda bi, i: (bi, i + 1, 0)),
            pl.BlockSpec((1, TILE, d), lambda bi, i: (bi, i + 1, 1)),
            pl.BlockSpec((1, TILE, d), lambda bi, i: (bi, i, 0)),
            resident(wpa), resident(wpb), resident(wo),
        ],
        out_specs=pl.BlockSpec((1, TILE, d), lambda bi, i: (bi, i, 0)),
        compiler_params=_cparams(("parallel", "parallel")),
        name="merge",
    )(oa, ob, gates, gates, x, wpa, wpb, wo)


def _split3(a):
    hi = a.astype(BF16)
    r1 = a - hi.astype(F32)
    mid = r1.astype(BF16)
    lo = (r1 - mid.astype(F32)).astype(BF16)
    return hi, mid, lo


def _route_kernel(h_ref, g_ref, wr_ref, br_ref, hp_ref, r_ref, cnt_ref, carry_ref, *, tm, half):
    i = pl.program_id(0)

    @pl.when(i == 0)
    def _():
        carry_ref[...] = jnp.zeros_like(carry_ref)

    hn = _rms(h_ref[...], g_ref[...])

    lo_bits = pltpu.bitcast(hn[:, :half].astype(BF16).astype(F32), jnp.uint32)
    hi_bits = pltpu.bitcast(hn[:, half:].astype(BF16).astype(F32), jnp.uint32)
    hp_ref[...] = (hi_bits & jnp.uint32(0xFFFF0000)) | (lo_bits >> 16)

    a_hi, a_mid, a_lo = _split3(hn)
    w_hi, w_mid, w_lo = wr_ref[0], wr_ref[1], wr_ref[2]
    dot = lambda a, w: jnp.dot(a, w, preferred_element_type=F32)
    lg = (dot(a_lo, w_hi) + dot(a_hi, w_lo) + dot(a_mid, w_mid)
          + dot(a_mid, w_hi) + dot(a_hi, w_mid) + dot(a_hi, w_hi)) + br_ref[...]

    lane = lax.broadcasted_iota(jnp.int32, lg.shape, 1)
    rmax = lambda v: jnp.max(v, axis=1, keepdims=True)
    rmin = lambda v: jnp.min(v, axis=1, keepdims=True)
    rsum = lambda v: jnp.sum(v, axis=1, keepdims=True)

    gmask = lane < N_GROUPS
    gl = jnp.where(gmask, lg, -jnp.inf)
    gmax = rmax(gl)
    gsel = rmin(jnp.where(gl == gmax, lane, LANES))
    p_g = 1.0 / rsum(jnp.where(gmask, jnp.exp(lg - gmax), 0.0))

    e_lo = EXPERT_LANE0 + gsel * EXPERTS_PER_GROUP
    emask = (lane >= e_lo) & (lane < e_lo + EXPERTS_PER_GROUP)
    emax = rmax(jnp.where(emask, lg, -jnp.inf))
    ex = jnp.where(emask, jnp.exp(lg - emax), 0.0)
    prob = jnp.where(emask, ex / rsum(ex), -1.0)
    top1 = rmax(prob)
    i1 = rmin(jnp.where(prob == top1, lane, LANES))
    prob2 = jnp.where(lane == i1, -1.0, prob)
    top2 = rmax(prob2)
    i2 = rmin(jnp.where(prob2 == top2, lane, LANES))
    denom = top1 + top2
    w1 = p_g * top1 / denom
    w2 = p_g * top2 / denom

    sel = ((lane == i1) | (lane == i2))
    row = lax.broadcasted_iota(jnp.int32, (tm, tm), 0)
    col = lax.broadcasted_iota(jnp.int32, (tm, tm), 1)
    before = (col < row).astype(BF16)
    prefix = dot(before, jnp.where(sel, 1.0, 0.0).astype(BF16)) + carry_ref[...]
    rank1 = rsum(jnp.where(lane == i1, prefix, 0.0))
    rank2 = rsum(jnp.where(lane == i2, prefix, 0.0))
    carry_ref[...] = carry_ref[...] + jnp.sum(jnp.where(sel, 1.0, 0.0), axis=0, keepdims=True)
    cnt_ref[...] = carry_ref[...]

    e1 = (i1 - EXPERT_LANE0).astype(F32)
    e2 = (i2 - EXPERT_LANE0).astype(F32)
    out = jnp.zeros(lg.shape, F32)
    for k, val in enumerate((e1, e2, w1, w2, rank1, rank2)):
        out = jnp.where(lane == k, val, out)
    r_ref[...] = out


def _route(h1, g, wr3, br):
    n, d = h1.shape
    tm = TILE
    return pl.pallas_call(
        functools.partial(_route_kernel, tm=tm, half=d // 2),
        out_shape=(jax.ShapeDtypeStruct((n, d // 2), jnp.uint32),
                   jax.ShapeDtypeStruct((n, LANES), F32),
                   jax.ShapeDtypeStruct((1, LANES), F32)),
        grid=(n // tm,),
        in_specs=[
            pl.BlockSpec((tm, d), lambda i: (i, 0)),
            pl.BlockSpec((1, d), lambda i: (0, 0)),
            pl.BlockSpec(wr3.shape, lambda i: (0, 0, 0)),
            pl.BlockSpec((1, LANES), lambda i: (0, 0)),
        ],
        out_specs=(pl.BlockSpec((tm, d // 2), lambda i: (i, 0)),
                   pl.BlockSpec((tm, LANES), lambda i: (i, 0)),
                   pl.BlockSpec((1, LANES), lambda i: (0, 0))),
        scratch_shapes=[pltpu.VMEM((1, LANES), F32)],
        compiler_params=_cparams(("arbitrary",)),
        name="route",
    )(h1, g.reshape(1, d), wr3, br)


def _dispatch_kernel(d0_ref, d1_ref, src_ref, init_ref, dst_ref, sem, *, tm):
    del init_ref
    base = pl.program_id(0) * tm

    def copy(r, dest_ref, s):
        return pltpu.make_async_copy(src_ref.at[pl.ds(r, 1)], dst_ref.at[pl.ds(dest_ref[base + r], 1)], sem.at[s])

    def start(r, c):
        copy(r, d0_ref, 0).start()
        copy(r, d1_ref, 1).start()
        return c

    lax.fori_loop(0, tm, start, 0, unroll=DMA_ISSUE_UNROLL)
    for s in range(2):
        pltpu.make_async_copy(src_ref, dst_ref.at[pl.ds(0, tm)], sem.at[s]).wait()


def _dispatch(dest0, dest1, hp, p_rows):
    n, w = hp.shape
    tm = TILE
    init = jnp.zeros((p_rows, w), hp.dtype)
    return pl.pallas_call(
        functools.partial(_dispatch_kernel, tm=tm),
        out_shape=jax.ShapeDtypeStruct((p_rows, w), hp.dtype),
        grid_spec=pltpu.PrefetchScalarGridSpec(
            num_scalar_prefetch=2,
            grid=(n // tm,),
            in_specs=[pl.BlockSpec((tm, w), lambda i, a, b: (i, 0)), pl.BlockSpec(memory_space=pl.ANY)],
            out_specs=pl.BlockSpec(memory_space=pl.ANY),
            scratch_shapes=[pltpu.SemaphoreType.DMA((2,))],
        ),
        input_output_aliases={3: 0},
        compiler_params=_cparams(("arbitrary",)),
        name="dispatch",
    )(dest0, dest1, hp, init)


def _expert_kernel(be_ref, nu_ref, x_ref, w1_ref, w3_ref, w2_ref, y_ref):
    del be_ref

    @pl.when(pl.program_id(0) < nu_ref[0])
    def _():
        xw = x_ref[...]
        lo = pltpu.bitcast(xw << 16, F32).astype(BF16)
        hi = pltpu.bitcast(xw & jnp.uint32(0xFFFF0000), F32).astype(BF16)
        xb = jnp.concatenate([lo, hi], axis=1)
        a = jnp.dot(xb, w1_ref[0], preferred_element_type=F32)
        g = jnp.dot(xb, w3_ref[0], preferred_element_type=F32)
        hid = (a * (1.0 / (1.0 + jnp.exp(-a))) * g).astype(BF16)
        y_ref[...] = jnp.dot(hid, w2_ref[0], preferred_element_type=F32)

    @pl.when(pl.program_id(0) >= nu_ref[0])
    def _():
        y_ref[...] = jnp.zeros_like(y_ref)


def _experts(block_e, n_used, xs, w1, w3, w2):
    p_rows, half = xs.shape
    _, d, ff = w1.shape
    nb = p_rows // ROUTE_BLOCK
    last = lambda j, nu: jnp.minimum(j, nu[0] - 1)
    return pl.pallas_call(
        _expert_kernel,
        out_shape=jax.ShapeDtypeStruct((p_rows, d), F32),
        grid_spec=pltpu.PrefetchScalarGridSpec(
            num_scalar_prefetch=2,
            grid=(nb,),
            in_specs=[
                pl.BlockSpec((ROUTE_BLOCK, half), lambda j, be, nu: (last(j, nu), 0)),
                pl.BlockSpec((1, d, ff), lambda j, be, nu: (be[last(j, nu)], 0, 0)),
                pl.BlockSpec((1, d, ff), lambda j, be, nu: (be[last(j, nu)], 0, 0)),
                pl.BlockSpec((1, ff, d), lambda j, be, nu: (be[last(j, nu)], 0, 0)),
            ],
            out_specs=pl.BlockSpec((ROUTE_BLOCK, d), lambda j, be, nu: (j, 0)),
        ),
        compiler_params=_cparams(("arbitrary",)),
        name="experts",
    )(block_e, n_used, xs, w1, w3, w2)


def _combine_kernel(d0_ref, d1_ref, h_ref, r_ref, g_ref, y_ref, o_ref, ya, yb, sem, *, tm):
    base = pl.program_id(0) * tm

    def copy(r, dest_ref, buf, s):
        return pltpu.make_async_copy(y_ref.at[pl.ds(dest_ref[base + r], 1)], buf.at[pl.ds(r, 1)], sem.at[s])

    def start(r, c):
        copy(r, d0_ref, ya, 0).start()
        copy(r, d1_ref, yb, 1).start()
        return c

    lax.fori_loop(0, tm, start, 0, unroll=DMA_ISSUE_UNROLL)
    pltpu.make_async_copy(y_ref.at[pl.ds(0, tm)], ya, sem.at[0]).wait()
    pltpu.make_async_copy(y_ref.at[pl.ds(0, tm)], yb, sem.at[1]).wait()

    route = r_ref[...]
    lane = lax.broadcasted_iota(jnp.int32, route.shape, 1)
    w1 = jnp.sum(jnp.where(lane == 2, route, 0.0), axis=1, keepdims=True)
    w2 = jnp.sum(jnp.where(lane == 3, route, 0.0), axis=1, keepdims=True)
    h2 = h_ref[...] + (ya[...] * w1 + yb[...] * w2)
    o_ref[...] = _rms(h2, g_ref[...])


def _combine(dest0, dest1, h1, route, g, y):
    n, d = h1.shape
    tm = TILE
    return pl.pallas_call(
        functools.partial(_combine_kernel, tm=tm),
        out_shape=jax.ShapeDtypeStruct((n, d), F32),
        grid_spec=pltpu.PrefetchScalarGridSpec(
            num_scalar_prefetch=2,
            grid=(n // tm,),
            in_specs=[
                pl.BlockSpec((tm, d), lambda i, a, b: (i, 0)),
                pl.BlockSpec((tm, LANES), lambda i, a, b: (i, 0)),
                pl.BlockSpec((1, d), lambda i, a, b: (0, 0)),
                pl.BlockSpec(memory_space=pl.ANY),
            ],
            out_specs=pl.BlockSpec((tm, d), lambda i, a, b: (i, 0)),
            scratch_shapes=[pltpu.VMEM((tm, d), F32), pltpu.VMEM((tm, d), F32),
                            pltpu.SemaphoreType.DMA((2,))],
        ),
        compiler_params=_cparams(("arbitrary",)),
        name="combine",
    )(dest0, dest1, h1, route, g.reshape(1, d), y)


def _rope_tables(lp):
    half = MLA_ROPE_DIM // 2
    inv = ROPE_THETA ** (-jnp.arange(half, dtype=F32) / half)
    pos = (jnp.arange(lp) - (TILE - N_META)).astype(F32)
    ang = pos[:, None] * inv[None, :]
    cos, sin = jnp.cos(ang), jnp.sin(ang)
    z32 = jnp.zeros((lp, half), F32)
    z64 = jnp.zeros((lp, LANES - MLA_ROPE_DIM), F32)
    return (jnp.concatenate([cos, cos, z64], axis=1),
            jnp.concatenate([-sin, z32, z64], axis=1),
            jnp.concatenate([z32, sin, z64], axis=1))


def kernel(x, meta_tokens, norm1_g, w_in, b_gate, kv_norm_g, w_uk, w_uv, w_proj_a, w_proj_b, w_out,
           norm2_g, w_route_group, b_route_group, w_route_expert, b_route_expert, w1, w3, w2, final_g):
    b, seq, d = x.shape
    assert seq % TILE == 0 and TILE % CHUNK == 0 and N_META <= TILE
    lp = TILE + seq
    n_tok = b * lp
    n_real = b * seq
    n_exp = N_GROUPS * EXPERTS_PER_GROUP
    sb_w = SB_HEADS * SB_HEAD_DIM
    qk_dim = MLA_NOPE_DIM + MLA_ROPE_DIM
    mq_w = MLA_HEADS * qk_dim
    sb_scale = SB_HEAD_DIM ** -0.5
    mla_scale = qk_dim ** -0.5

    wi = w_in[0]
    o_q = 3 * sb_w
    o_c = o_q + mq_w
    o_r = o_c + MLA_KV_RANK
    o_g = o_r + MLA_ROPE_DIM
    w_sb = wi[:, :2 * sb_w].astype(BF16)
    w_sbv_t = wi[:, 2 * sb_w:o_q].T.astype(BF16)
    w_mq = jnp.pad(wi[:, o_q:o_c].reshape(d, MLA_HEADS, qk_dim),
                   ((0, 0), (0, 0), (0, MLA_QK_PAD - qk_dim))).reshape(d, MLA_HEADS * MLA_QK_PAD).astype(BF16)
    w_c = jnp.pad(wi[:, o_c:o_g], ((0, 0), (0, LANES - MLA_ROPE_DIM))).astype(BF16)
    w_g = wi[:, o_g:].astype(BF16)
    sb_colscale = jnp.concatenate([jnp.full((1, sb_w), sb_scale * LOG2E, F32), jnp.ones((1, sb_w), F32)], axis=1)
    wr = jnp.zeros((d, LANES), F32)
    wr = wr.at[:, :N_GROUPS].set(w_route_group[0]).at[:, EXPERT_LANE0:EXPERT_LANE0 + n_exp].set(w_route_expert[0])
    wr_hi = wr.astype(BF16)
    wr_mid = (wr - wr_hi.astype(F32)).astype(BF16)
    wr_lo = (wr - wr_hi.astype(F32) - wr_mid.astype(F32)).astype(BF16)
    wr3 = jnp.stack([wr_hi, wr_mid, wr_lo])
    br = jnp.zeros((1, LANES), F32)
    br = br.at[0, :N_GROUPS].set(b_route_group[0]).at[0, EXPERT_LANE0:EXPERT_LANE0 + n_exp].set(b_route_expert[0])

    head = jnp.concatenate([jnp.zeros((TILE - N_META, d), F32), meta_tokens.astype(F32)], axis=0)
    hn = _norm1(x, head, norm1_g[0]).reshape(n_tok, d)
    tm = _row_tile(lp)
    tpb = lp // tm
    tables = _rope_tables(lp)
    tab_spec = pl.BlockSpec((tm, LANES), lambda j, i: (i % tpb, 0))
    row_spec = lambda tn: pl.BlockSpec((1, tn), lambda j, i: (0, j))

    sb_qk = _proj(_proj_scale_kernel, hn, w_sb, [sb_colscale], [row_spec(sb_w)], BF16, tm, sb_w)
    sb_vt = _proj_t(hn, w_sbv_t, tm)
    q_mla = _proj(functools.partial(_proj_mlaq_kernel, scale=mla_scale * LOG2E), hn, w_mq,
                  list(tables), [tab_spec] * 3, BF16, tm, min(4, MLA_HEADS) * MLA_QK_PAD)
    ckr = _proj(_proj_scale_kernel, hn, w_c, [jnp.ones((1, w_c.shape[1]), F32)],
                [row_spec(w_c.shape[1])], F32, tm, w_c.shape[1])
    gates = _proj(_proj_gate_kernel, hn, w_g, [b_gate[0].reshape(1, 2 * d)], [row_spec(d)], F32, tm, d)
    k_mla, vt_mla = _kvup(ckr, kv_norm_g[0], w_uk[0].astype(BF16), w_uv[0].T.astype(BF16), tables, tm, tpb)

    o_a = _sb_attention(sb_qk.reshape(b, lp, 2 * sb_w), sb_vt, seq)
    o_b = _mla_attention(q_mla.reshape(b, lp, -1), k_mla.reshape(b, lp, -1), vt_mla, seq)
    h1 = _merge(o_a, o_b, gates.reshape(b, lp, 2 * d), x,
                w_proj_a[0].astype(BF16), w_proj_b[0].astype(BF16), w_out[0].astype(BF16)).reshape(n_real, d)

    hp, route, counts = _route(h1, norm2_g[0], wr3, br)
    cnt = counts[0, EXPERT_LANE0:EXPERT_LANE0 + n_exp].astype(jnp.int32)
    padded = (cnt + ROUTE_BLOCK - 1) // ROUTE_BLOCK * ROUTE_BLOCK
    pends = jnp.cumsum(padded)
    pstarts = pends - padded
    n_blocks = (n_real * TOP_K + n_exp * (ROUTE_BLOCK - 1) + ROUTE_BLOCK - 1) // ROUTE_BLOCK
    n_used = (pends[-1] // ROUTE_BLOCK).astype(jnp.int32).reshape(1)
    blk_start = jnp.minimum(jnp.arange(n_blocks), n_used[0] - 1) * ROUTE_BLOCK
    block_e = jnp.minimum(jnp.sum(pends[None, :] <= blk_start[:, None], axis=1), n_exp - 1).astype(jnp.int32)
    ids = route[:, :6].astype(jnp.int32)
    expert_iota = jnp.arange(n_exp, dtype=jnp.int32)[None, :]
    start_of = lambda e: jnp.sum(jnp.where(e[:, None] == expert_iota, pstarts[None, :], 0), axis=1)
    dest0 = (start_of(ids[:, 0]) + ids[:, 4]).astype(jnp.int32)
    dest1 = (start_of(ids[:, 1]) + ids[:, 5]).astype(jnp.int32)

    xs = _dispatch(dest0, dest1, hp, n_blocks * ROUTE_BLOCK)
    y = _experts(block_e, n_used, xs, w1[0].astype(BF16), w3[0].astype(BF16), w2[0].astype(BF16))
    out = _combine(dest0, dest1, h1, route, final_g, y)
    return out.reshape(b, seq, d)
```

```python
import functools

import jax
import jax.numpy as jnp
from jax import lax
from jax.experimental import pallas as pl
from jax.experimental.pallas import tpu as pltpu

N_META = 16
CHUNK = 64
SB_HEADS = 8
SB_HEAD_DIM = 128
MLA_HEADS = 16
MLA_NOPE_DIM = 128
MLA_ROPE_DIM = 64
MLA_V_DIM = 128
MLA_KV_RANK = 512
ROPE_THETA = 10000.0
N_GROUPS = 4
EXPERTS_PER_GROUP = 8
TOP_K = 2
ROUTE_BLOCK = 256
RMS_EPS = 1e-6

TILE = 256
LANES = 128
MLA_QK_PAD = 256
ATTN_Q_TILES = 4
EXPERT_LANE0 = 8
VMEM_LIMIT = 56 * 1024 * 1024
NEG_BIG = -1e30
DMA_ISSUE_UNROLL = 8
DENOM_ROWS = 16
EXPERT_FF_SPLIT = 2
STICK_GONE_LOG2 = 152.0

F32 = jnp.float32
BF16 = jnp.bfloat16
LOG2E = 1.4426950408889634


def _cparams(sem):
    return pltpu.CompilerParams(dimension_semantics=sem, vmem_limit_bytes=VMEM_LIMIT)


def _rms(v, g):
    ms = jnp.mean(v * v, axis=-1, keepdims=True)
    return v * lax.rsqrt(ms + RMS_EPS) * g


def _row_tile(lp):
    for t in (768, 512, 256):
        if lp % t == 0:
            return t
    raise ValueError(lp)


def _norm1_kernel(x_ref, head_ref, g_ref, o_ref):
    i = pl.program_id(1)

    @pl.when(i == 0)
    def _():
        o_ref[0] = _rms(head_ref[...], g_ref[...]).astype(BF16)

    @pl.when(i > 0)
    def _():
        o_ref[0] = _rms(x_ref[0], g_ref[...]).astype(BF16)


def _norm1(x, head, g):
    b, s, d = x.shape
    nt = s // TILE + 1
    return pl.pallas_call(
        _norm1_kernel,
        out_shape=jax.ShapeDtypeStruct((b, nt * TILE, d), BF16),
        grid=(b, nt),
        in_specs=[
            pl.BlockSpec((1, TILE, d), lambda bi, i: (bi, jnp.maximum(i - 1, 0), 0)),
            pl.BlockSpec((TILE, d), lambda bi, i: (0, 0)),
            pl.BlockSpec((1, d), lambda bi, i: (0, 0)),
        ],
        out_specs=pl.BlockSpec((1, TILE, d), lambda bi, i: (bi, i, 0)),
        compiler_params=_cparams(("parallel", "parallel")),
        name="norm1",
    )(x, head, g.reshape(1, d))


def _rope_rows(r, cos_t, sin_a, sin_b):
    return r * cos_t + pltpu.roll(r, 96, 1) * sin_a + pltpu.roll(r, 32, 1) * sin_b


def _proj_scale_kernel(x_ref, w_ref, s_ref, o_ref):
    acc = jnp.dot(x_ref[...], w_ref[...], preferred_element_type=F32)
    o_ref[...] = (acc * s_ref[...]).astype(o_ref.dtype)


def _proj_gate_kernel(x_ref, w_ref, b_ref, o_ref):
    acc = jnp.dot(x_ref[...], w_ref[...], preferred_element_type=F32) + b_ref[...]
    o_ref[...] = 1.0 / (1.0 + jnp.exp(-acc))


def _proj_mlaq_kernel(x_ref, w_ref, cos_ref, sa_ref, sb_ref, o_ref, *, scale):
    acc = jnp.dot(x_ref[...], w_ref[...], preferred_element_type=F32)
    cos_t, sin_a, sin_b = cos_ref[...], sa_ref[...], sb_ref[...]
    for hh in range(acc.shape[1] // MLA_QK_PAD):
        c0 = hh * MLA_QK_PAD
        nope = acc[:, c0:c0 + LANES] * scale
        rope = _rope_rows(acc[:, c0 + LANES:c0 + 2 * LANES], cos_t, sin_a, sin_b) * scale
        o_ref[:, c0:c0 + LANES] = nope.astype(o_ref.dtype)
        o_ref[:, c0 + LANES:c0 + 2 * LANES] = rope.astype(o_ref.dtype)


def _store_lane_tiles(o_ref, val_t):
    for c in range(val_t.shape[1] // TILE):
        o_ref[c] = val_t[:, c * TILE:(c + 1) * TILE].astype(o_ref.dtype)


def _proj_t_kernel(x_ref, wt_ref, o_ref):
    acc_t = lax.dot_general(wt_ref[...], x_ref[...], (((1,), (1,)), ((), ())), preferred_element_type=F32)
    _store_lane_tiles(o_ref, acc_t)


def _proj_t(hn2d, wt, tm):
    m, k = hn2d.shape
    n = wt.shape[0]
    return pl.pallas_call(
        _proj_t_kernel,
        out_shape=jax.ShapeDtypeStruct((m // TILE, n, TILE), BF16),
        grid=(m // tm,),
        in_specs=[pl.BlockSpec((tm, k), lambda i: (i, 0)), pl.BlockSpec((n, k), lambda i: (0, 0))],
        out_specs=pl.BlockSpec((tm // TILE, n, TILE), lambda i: (i, 0, 0)),
        compiler_params=_cparams(("parallel",)),
        name="proj_t",
    )(hn2d, wt)


def _proj(kernel, hn2d, w, extras, extra_specs, out_dtype, tm, tn):
    m, k = hn2d.shape
    n = w.shape[1]
    return pl.pallas_call(
        kernel,
        out_shape=jax.ShapeDtypeStruct((m, n), out_dtype),
        grid=(n // tn, m // tm),
        in_specs=[
            pl.BlockSpec((tm, k), lambda j, i: (i, 0)),
            pl.BlockSpec((k, tn), lambda j, i: (0, j)),
        ] + extra_specs,
        out_specs=pl.BlockSpec((tm, tn), lambda j, i: (i, j)),
        compiler_params=_cparams(("parallel", "parallel")),
        name=getattr(kernel, "__name__", None) or kernel.func.__name__,
    )(hn2d, w, *extras)


def _kvup_kernel(c_ref, g_ref, wk_ref, wvt_ref, cos_ref, sa_ref, sb_ref, k_ref, vt_ref, *, rank):
    ckr = c_ref[...]
    cn = _rms(ckr[:, :rank], g_ref[...]).astype(BF16)
    kn = jnp.dot(cn, wk_ref[...], preferred_element_type=F32)
    vv_t = lax.dot_general(wvt_ref[...], cn, (((1,), (1,)), ((), ())), preferred_element_type=F32)
    rope = _rope_rows(ckr[:, rank:rank + LANES], cos_ref[...], sa_ref[...], sb_ref[...]).astype(BF16)
    for h in range(kn.shape[1] // MLA_NOPE_DIM):
        k_ref[:, h * MLA_QK_PAD:h * MLA_QK_PAD + LANES] = kn[:, h * LANES:(h + 1) * LANES].astype(BF16)
        k_ref[:, h * MLA_QK_PAD + LANES:(h + 1) * MLA_QK_PAD] = rope
    _store_lane_tiles(vt_ref, vv_t)


def _kvup(ckr, g, wk, wvt, tables, tm, tiles_per_batch):
    m, cw = ckr.shape
    rank = wk.shape[0]
    nk = wk.shape[1]
    nv = wvt.shape[0]
    heads = nk // MLA_NOPE_DIM
    tab_spec = pl.BlockSpec((tm, LANES), lambda i: (i % tiles_per_batch, 0))
    return pl.pallas_call(
        functools.partial(_kvup_kernel, rank=rank),
        out_shape=(jax.ShapeDtypeStruct((m, heads * MLA_QK_PAD), BF16),
                   jax.ShapeDtypeStruct((m // TILE, nv, TILE), BF16)),
        grid=(m // tm,),
        in_specs=[
            pl.BlockSpec((tm, cw), lambda i: (i, 0)),
            pl.BlockSpec((1, rank), lambda i: (0, 0)),
            pl.BlockSpec(wk.shape, lambda i: (0, 0)),
            pl.BlockSpec(wvt.shape, lambda i: (0, 0)),
            tab_spec, tab_spec, tab_spec,
        ],
        out_specs=(pl.BlockSpec((tm, heads * MLA_QK_PAD), lambda i: (i, 0)),
                   pl.BlockSpec((tm // TILE, nv, TILE), lambda i: (i, 0, 0))),
        compiler_params=_cparams(("parallel",)),
        name="kvup",
    )(ckr, g.reshape(1, rank), wk, wvt, *tables)


def _softplus2(z):
    neg_abs = pltpu.bitcast(pltpu.bitcast(z, jnp.uint32) | jnp.uint32(0x80000000), F32)
    return jnp.maximum(z, 0.0) + jnp.log2(1.0 + jnp.exp2(neg_abs))


def _sb_kernel(*refs, nsub, first_valid):
    q_refs = refs[:nsub]
    k_ref, vt_ref, o_ref, carry_ref, acc_ref = refs[nsub:]
    qs = pl.program_id(2)
    first_tile = 1 + nsub * qs
    carry_ref[...] = jnp.zeros_like(carry_ref)
    acc_ref[...] = jnp.zeros_like(acc_ref)
    trow = lax.broadcasted_iota(jnp.int32, (TILE, 2 * TILE), 0)
    tcol = lax.broadcasted_iota(jnp.int32, (TILE, 2 * TILE), 1) & (TILE - 1)
    tri2 = jnp.where(tcol >= trow, 1.0, 0.0).astype(BF16)

    def update(c0, c1, j, mask):
        lanes = slice(c0 * TILE, c1 * TILE)
        q = jnp.concatenate([q_refs[c][0] for c in range(c0, c1)], axis=0)
        kb = k_ref[0, pl.ds(pl.multiple_of(j * TILE, TILE), TILE), :]
        z = lax.dot_general(kb, q, (((1,), (1,)), ((), ())), preferred_element_type=F32)
        sp = _softplus2(z)
        if mask is not None:
            sp = jnp.where(mask, sp, 0.0)
        hi32 = pltpu.bitcast(pltpu.bitcast(sp, jnp.uint32) & jnp.uint32(0xFFFF0000), F32)
        parts = jnp.concatenate([hi32.astype(BF16), (sp - hi32).astype(BF16)], axis=0)
        cs = jnp.dot(tri2, parts, preferred_element_type=F32)
        carry = carry_ref[:, lanes]
        a = jnp.exp2(z - cs - carry)
        if mask is not None:
            a = jnp.where(mask, a, 0.0)
        acc_ref[:, lanes] = acc_ref[:, lanes] + jnp.dot(vt_ref[j], a.astype(BF16), preferred_element_type=F32)
        carry_ref[:, lanes] = carry + cs[0:1, :]

    def stick_left(c0, c1):
        return jnp.min(carry_ref[:, c0 * TILE:c1 * TILE]) < STICK_GONE_LOG2

    def diag_mask(nq):
        key = lax.broadcasted_iota(jnp.int32, (TILE, nq * TILE), 0)
        qry = lax.broadcasted_iota(jnp.int32, (TILE, nq * TILE), 1)
        return (qry >= TILE) | (key < qry)

    half = nsub // 2
    for c in reversed(range(nsub)):
        near = min(c + half, nsub)
        update(c, near, first_tile + c, diag_mask(near - c))
        if near < nsub:
            @pl.when(stick_left(near, nsub))
            def _():
                update(near, nsub, first_tile + c, None)

    for c0, c1 in ((0, half), (half, nsub)):
        def body(state):
            j, _ = state
            update(c0, c1, j, None)
            return j - 1, stick_left(c0, c1)

        _, alive = lax.while_loop(lambda st: (st[0] > 0) & st[1], body, (first_tile - 1, stick_left(c0, c1)))

        @pl.when(alive)
        def _():
            key0 = lax.broadcasted_iota(jnp.int32, (TILE, (c1 - c0) * TILE), 0)
            update(c0, c1, 0, key0 >= first_valid)

    o_ref[0] = acc_ref[...].T.astype(o_ref.dtype)


def _q_specs(nsub, width, col0):
    return [pl.BlockSpec((1, TILE, width),
                         functools.partial(lambda bi, h, i, c: (bi, nsub * i + 1 + c, col0 + h), c=c))
            for c in range(nsub)]


def _sb_attention(qk, vt, seq):
    b, lp, w2 = qk.shape
    heads = w2 // (2 * SB_HEAD_DIM)
    nsub = ATTN_Q_TILES
    rows = nsub * TILE
    return pl.pallas_call(
        functools.partial(_sb_kernel, nsub=nsub, first_valid=TILE - N_META),
        out_shape=jax.ShapeDtypeStruct((b, seq, heads * SB_HEAD_DIM), BF16),
        grid=(b, heads, seq // rows),
        in_specs=_q_specs(nsub, SB_HEAD_DIM, 0) + [
            pl.BlockSpec((1, lp, SB_HEAD_DIM), lambda bi, h, i: (bi, 0, heads + h)),
            pl.BlockSpec((lp // TILE, SB_HEAD_DIM, TILE), lambda bi, h, i: (bi, h, 0)),
        ],
        out_specs=pl.BlockSpec((1, rows, SB_HEAD_DIM), lambda bi, h, i: (bi, i, h)),
        scratch_shapes=[pltpu.VMEM((1, rows), F32), pltpu.VMEM((SB_HEAD_DIM, rows), F32)],
        compiler_params=_cparams(("parallel", "parallel", "arbitrary")),
        name="sb_attention",
    )(*([qk] * (nsub + 1)), vt)


def _mla_kernel(*refs, nsub, first_valid):
    q_refs = refs[:nsub]
    k_ref, vt_ref, bias_ref, o_ref, m_ref, acc_ref, s0_ref, s1_ref, p0_ref, p1_ref, x0_ref, x1_ref = refs[nsub:]
    qs = pl.program_id(2)
    rows = nsub * TILE
    blk_keys = 2 * TILE
    n_full = (nsub * qs) // 2

    def queries():
        return jnp.concatenate([q_refs[c][0] for c in range(nsub)], axis=0)

    def scores(blk, s_ref, smax_ref):
        kb = k_ref[0, pl.ds(pl.multiple_of((1 + 2 * blk) * TILE, TILE), blk_keys), :]
        s = lax.dot_general(kb, queries(), (((1,), (1,)), ((), ())), preferred_element_type=F32)
        s_ref[...] = s
        smax_ref[...] = jnp.max(s, axis=0, keepdims=True)

    def softmax(s_ref, smax_ref, p_ref, bias):
        if bias is None:
            s = s_ref[...]
            smax = smax_ref[...]
        else:
            s = s_ref[...] + bias
            smax = jnp.max(s, axis=0, keepdims=True)
        m_old = m_ref[...]
        m_new = jnp.maximum(m_old, smax)
        alpha = jnp.exp2(m_old - m_new)
        p = jnp.exp2(s - m_new)
        m_ref[...] = m_new
        p_ref[...] = p.astype(BF16)
        return alpha

    def with_ones(vtb):
        return jnp.concatenate([vtb, jnp.ones((DENOM_ROWS, vtb.shape[1]), BF16)], axis=0)

    def values(blk, p_ref):
        j = 1 + 2 * blk
        vtb = with_ones(jnp.concatenate([vt_ref[j], vt_ref[j + 1]], axis=1))
        return jnp.dot(vtb, p_ref[...], preferred_element_type=F32)

    def stage(blk, cur, nxt, bias=None, lookahead=True):
        (s_cur, x_cur, p_cur), (s_nxt, x_nxt, p_prev) = cur, nxt
        if lookahead:
            scores(blk + 1, s_nxt, x_nxt)
        alpha = softmax(s_cur, x_cur, p_cur, bias)
        acc_ref[...] = alpha * (acc_ref[...] + values(jnp.maximum(blk - 1, 0), p_prev))

    s = lax.dot_general(k_ref[0, first_valid:TILE, :], queries(), (((1,), (1,)), ((), ())),
                        preferred_element_type=F32)
    m0 = jnp.max(s, axis=0, keepdims=True)
    p = jnp.exp2(s - m0)
    m_ref[...] = m0
    p_tile = jnp.concatenate([jnp.zeros((first_valid, rows), BF16), p.astype(BF16)], axis=0)
    acc_ref[...] = jnp.dot(with_ones(vt_ref[0]), p_tile, preferred_element_type=F32)

    even = (s0_ref, x0_ref, p0_ref)
    odd = (s1_ref, x1_ref, p1_ref)
    scores(0, s0_ref, x0_ref)
    p1_ref[...] = jnp.zeros_like(p1_ref)

    def body(u, carry):
        stage(2 * u, even, odd)
        stage(2 * u + 1, odd, even)
        return carry

    lax.fori_loop(0, n_full // 2, body, 0)
    stage(n_full, even, odd, bias=bias_ref[:blk_keys, :])
    stage(n_full + 1, odd, even, bias=bias_ref[blk_keys:, :], lookahead=False)
    acc = acc_ref[...] + values(n_full + 1, p1_ref)
    o_ref[0] = (acc[:MLA_V_DIM] / acc[MLA_V_DIM:MLA_V_DIM + 1]).T.astype(o_ref.dtype)


def _mla_attention(q, k, vt, seq):
    b, lp, _ = q.shape
    heads = vt.shape[1] // MLA_V_DIM
    nsub = ATTN_Q_TILES
    assert nsub == 4, "the kernel visits the query-overlapping keys as exactly two 2-tile blocks"
    rows = nsub * TILE
    shift = CHUNK.bit_length() - 1
    key = lax.broadcasted_iota(jnp.int32, (rows, rows), 0)
    qry = lax.broadcasted_iota(jnp.int32, (rows, rows), 1)
    bias = jnp.where((key >> shift) <= (qry >> shift), 0.0, NEG_BIG).astype(F32)
    return pl.pallas_call(
        functools.partial(_mla_kernel, nsub=nsub, first_valid=TILE - N_META),
        out_shape=jax.ShapeDtypeStruct((b, seq, heads * MLA_V_DIM), BF16),
        grid=(b, heads, seq // rows),
        in_specs=_q_specs(nsub, MLA_QK_PAD, 0) + [
            pl.BlockSpec((1, lp, MLA_QK_PAD), lambda bi, h, i: (bi, 0, h)),
            pl.BlockSpec((lp // TILE, MLA_V_DIM, TILE), lambda bi, h, i: (bi, h, 0)),
            pl.BlockSpec((rows, rows), lambda bi, h, i: (0, 0), pipeline_mode=pl.Buffered(1)),
        ],
        out_specs=pl.BlockSpec((1, rows, MLA_V_DIM), lambda bi, h, i: (bi, i, h)),
        scratch_shapes=[pltpu.VMEM((1, rows), F32),
                        pltpu.VMEM((MLA_V_DIM + DENOM_ROWS, rows), F32),
                        pltpu.VMEM((2 * TILE, rows), F32), pltpu.VMEM((2 * TILE, rows), F32),
                        pltpu.VMEM((2 * TILE, rows), BF16), pltpu.VMEM((2 * TILE, rows), BF16),
                        pltpu.VMEM((1, rows), F32), pltpu.VMEM((1, rows), F32)],
        compiler_params=_cparams(("parallel", "parallel", "arbitrary")),
        name="mla_attention",
    )(*([q] * nsub), k, vt, bias)


def _merge_kernel(oa_ref, ob_ref, ga_ref, gb_ref, x_ref, wpa_ref, wpb_ref, wo_ref, o_ref):
    pa = jnp.dot(oa_ref[0], wpa_ref[...], preferred_element_type=F32)
    pb = jnp.dot(ob_ref[0], wpb_ref[...], preferred_element_type=F32)
    y = ga_ref[0] * pa + gb_ref[0] * pb
    o_ref[0] = x_ref[0] + jnp.dot(y.astype(BF16), wo_ref[...], preferred_element_type=F32)


def _merge(oa, ob, gates, x, wpa, wpb, wo):
    b, s, d = x.shape
    nt = s // TILE
    resident = lambda w: pl.BlockSpec(w.shape, lambda bi, i: (0, 0), pipeline_mode=pl.Buffered(1))
    return pl.pallas_call(
        _merge_kernel,
        out_shape=jax.ShapeDtypeStruct((b, s, d), F32),
        grid=(b, nt),
        in_specs=[
            pl.BlockSpec((1, TILE, oa.shape[2]), lambda bi, i: (bi, i, 0)),
            pl.BlockSpec((1, TILE, ob.shape[2]), lambda bi, i: (bi, i, 0)),
            pl.BlockSpec((1, TILE, d), lambda bi, i: (bi, i + 1, 0)),
            pl.BlockSpec((1, TILE, d), lambda bi, i: (bi, i + 1, 1)),
            pl.BlockSpec((1, TILE, d), lambda bi, i: (bi, i, 0)),
            resident(wpa), resident(wpb), resident(wo),
        ],
        out_specs=pl.BlockSpec((1, TILE, d), lambda bi, i: (bi, i, 0)),
        compiler_params=_cparams(("parallel", "parallel")),
        name="merge",
    )(oa, ob, gates, gates, x, wpa, wpb, wo)


def _split2(a):
    hi = a.astype(BF16)
    return hi, (a - hi.astype(F32)).astype(BF16)


def _route_kernel(h_ref, g_ref, wr_ref, br_ref, hp_ref, r_ref, cnt_ref, carry_ref, *, tm, half):
    i = pl.program_id(0)

    @pl.when(i == 0)
    def _():
        carry_ref[...] = jnp.zeros_like(carry_ref)

    hn = _rms(h_ref[...], g_ref[...])

    lo_bits = pltpu.bitcast(hn[:, :half].astype(BF16).astype(F32), jnp.uint32)
    hi_bits = pltpu.bitcast(hn[:, half:].astype(BF16).astype(F32), jnp.uint32)
    hp_ref[...] = (hi_bits & jnp.uint32(0xFFFF0000)) | (lo_bits >> 16)

    a_hi, a_mid = _split2(hn)
    w_hi, w_mid = wr_ref[0], wr_ref[1]
    dot = lambda a, w: jnp.dot(a, w, preferred_element_type=F32)
    lg = (dot(a_mid, w_hi) + dot(a_hi, w_mid) + dot(a_hi, w_hi)) + br_ref[...]

    lane = lax.broadcasted_iota(jnp.int32, lg.shape, 1)
    rmax = lambda v: jnp.max(v, axis=1, keepdims=True)
    rmin = lambda v: jnp.min(v, axis=1, keepdims=True)
    rsum = lambda v: jnp.sum(v, axis=1, keepdims=True)

    gmask = lane < N_GROUPS
    gl = jnp.where(gmask, lg, -jnp.inf)
    gmax = rmax(gl)
    gsel = rmin(jnp.where(gl == gmax, lane, LANES))
    p_g = 1.0 / rsum(jnp.where(gmask, jnp.exp(lg - gmax), 0.0))

    e_lo = EXPERT_LANE0 + gsel * EXPERTS_PER_GROUP
    emask = (lane >= e_lo) & (lane < e_lo + EXPERTS_PER_GROUP)
    emax = rmax(jnp.where(emask, lg, -jnp.inf))
    ex = jnp.where(emask, jnp.exp(lg - emax), 0.0)
    prob = jnp.where(emask, ex / rsum(ex), -1.0)
    top1 = rmax(prob)
    i1 = rmin(jnp.where(prob == top1, lane, LANES))
    prob2 = jnp.where(lane == i1, -1.0, prob)
    top2 = rmax(prob2)
    i2 = rmin(jnp.where(prob2 == top2, lane, LANES))
    denom = top1 + top2
    w1 = p_g * top1 / denom
    w2 = p_g * top2 / denom

    sel = ((lane == i1) | (lane == i2))
    row = lax.broadcasted_iota(jnp.int32, (tm, tm), 0)
    col = lax.broadcasted_iota(jnp.int32, (tm, tm), 1)
    before = (col < row).astype(BF16)
    prefix = dot(before, jnp.where(sel, 1.0, 0.0).astype(BF16)) + carry_ref[...]
    rank1 = rsum(jnp.where(lane == i1, prefix, 0.0))
    rank2 = rsum(jnp.where(lane == i2, prefix, 0.0))
    carry_ref[...] = carry_ref[...] + jnp.sum(jnp.where(sel, 1.0, 0.0), axis=0, keepdims=True)
    cnt_ref[...] = carry_ref[...]

    e1 = (i1 - EXPERT_LANE0).astype(F32)
    e2 = (i2 - EXPERT_LANE0).astype(F32)
    out = jnp.zeros(lg.shape, F32)
    for k, val in enumerate((e1, e2, w1, w2, rank1, rank2)):
        out = jnp.where(lane == k, val, out)
    r_ref[...] = out


def _route(h1, g, wr3, br):
    n, d = h1.shape
    tm = TILE
    return pl.pallas_call(
        functools.partial(_route_kernel, tm=tm, half=d // 2),
        out_shape=(jax.ShapeDtypeStruct((n, d // 2), jnp.uint32),
                   jax.ShapeDtypeStruct((n, LANES), F32),
                   jax.ShapeDtypeStruct((1, LANES), F32)),
        grid=(n // tm,),
        in_specs=[
            pl.BlockSpec((tm, d), lambda i: (i, 0)),
            pl.BlockSpec((1, d), lambda i: (0, 0)),
            pl.BlockSpec(wr3.shape, lambda i: (0, 0, 0)),
            pl.BlockSpec((1, LANES), lambda i: (0, 0)),
        ],
        out_specs=(pl.BlockSpec((tm, d // 2), lambda i: (i, 0)),
                   pl.BlockSpec((tm, LANES), lambda i: (i, 0)),
                   pl.BlockSpec((1, LANES), lambda i: (0, 0))),
        scratch_shapes=[pltpu.VMEM((1, LANES), F32)],
        compiler_params=_cparams(("arbitrary",)),
        name="route",
    )(h1, g.reshape(1, d), wr3, br)


def _dispatch_kernel(d0_ref, d1_ref, src_ref, init_ref, dst_ref, sem, *, tm):
    del init_ref
    base = pl.program_id(0) * tm

    def copy(r, dest_ref, s):
        return pltpu.make_async_copy(src_ref.at[pl.ds(r, 1)], dst_ref.at[pl.ds(dest_ref[base + r], 1)], sem.at[s])

    def start(r, c):
        copy(r, d0_ref, 0).start()
        copy(r, d1_ref, 1).start()
        return c

    lax.fori_loop(0, tm, start, 0, unroll=DMA_ISSUE_UNROLL)
    for s in range(2):
        pltpu.make_async_copy(src_ref, dst_ref.at[pl.ds(0, tm)], sem.at[s]).wait()


def _dispatch(dest0, dest1, hp, p_rows):
    n, w = hp.shape
    tm = TILE
    init = jnp.zeros((p_rows, w), hp.dtype)
    return pl.pallas_call(
        functools.partial(_dispatch_kernel, tm=tm),
        out_shape=jax.ShapeDtypeStruct((p_rows, w), hp.dtype),
        grid_spec=pltpu.PrefetchScalarGridSpec(
            num_scalar_prefetch=2,
            grid=(n // tm,),
            in_specs=[pl.BlockSpec((tm, w), lambda i, a, b: (i, 0)), pl.BlockSpec(memory_space=pl.ANY)],
            out_specs=pl.BlockSpec(memory_space=pl.ANY),
            scratch_shapes=[pltpu.SemaphoreType.DMA((2,))],
        ),
        input_output_aliases={3: 0},
        compiler_params=_cparams(("arbitrary",)),
        name="dispatch",
    )(dest0, dest1, hp, init)


def _expert_kernel(be_ref, nu_ref, first_ref, x_ref, w1_ref, w3_ref, w2_ref, y_ref, c1_ref, c3_ref, c2_ref):
    del be_ref
    j = pl.program_id(0)
    h = pl.program_id(1)
    live = j < nu_ref[0]

    @pl.when(live & (first_ref[j] == 1))
    def _():
        c1_ref[h] = w1_ref[0].astype(BF16)
        c3_ref[h] = w3_ref[0].astype(BF16)
        c2_ref[h] = w2_ref[0].astype(BF16)

    @pl.when(live)
    def _():
        xw = x_ref[...]
        lo = pltpu.bitcast(xw << 16, F32).astype(BF16)
        hi = pltpu.bitcast(xw & jnp.uint32(0xFFFF0000), F32).astype(BF16)
        xb = jnp.concatenate([lo, hi], axis=1)
        a = jnp.dot(xb, c1_ref[h], preferred_element_type=F32)
        g = jnp.dot(xb, c3_ref[h], preferred_element_type=F32)
        hid = (a * (1.0 / (1.0 + jnp.exp(-a))) * g).astype(BF16)
        part = jnp.dot(hid, c2_ref[h], preferred_element_type=F32)

        @pl.when(h == 0)
        def _():
            y_ref[...] = part

        @pl.when(h > 0)
        def _():
            y_ref[...] = y_ref[...] + part

    @pl.when(jnp.logical_not(live) & (h == 0))
    def _():
        y_ref[...] = jnp.zeros_like(y_ref)


def _experts(block_e, n_used, first, xs, w1, w3, w2):
    p_rows, half = xs.shape
    _, d, ff = w1.shape
    nb = p_rows // ROUTE_BLOCK
    nh = EXPERT_FF_SPLIT
    fh = ff // nh
    last = lambda j, nu: jnp.minimum(j, nu[0] - 1)

    def half_of(j, h, nu, fr):
        return jnp.where((fr[last(j, nu)] == 1) & (j < nu[0]), h, nh - 1)

    return pl.pallas_call(
        _expert_kernel,
        out_shape=jax.ShapeDtypeStruct((p_rows, d), F32),
        grid_spec=pltpu.PrefetchScalarGridSpec(
            num_scalar_prefetch=3,
            grid=(nb, nh),
            in_specs=[
                pl.BlockSpec((ROUTE_BLOCK, half), lambda j, h, be, nu, fr: (last(j, nu), 0)),
                pl.BlockSpec((1, d, fh), lambda j, h, be, nu, fr: (be[last(j, nu)], 0, half_of(j, h, nu, fr))),
                pl.BlockSpec((1, d, fh), lambda j, h, be, nu, fr: (be[last(j, nu)], 0, half_of(j, h, nu, fr))),
                pl.BlockSpec((1, fh, d), lambda j, h, be, nu, fr: (be[last(j, nu)], half_of(j, h, nu, fr), 0)),
            ],
            out_specs=pl.BlockSpec((ROUTE_BLOCK, d), lambda j, h, be, nu, fr: (j, 0)),
            scratch_shapes=[pltpu.VMEM((nh, d, fh), BF16), pltpu.VMEM((nh, d, fh), BF16),
                            pltpu.VMEM((nh, fh, d), BF16)],
        ),
        compiler_params=_cparams(("arbitrary", "arbitrary")),
        name="experts",
    )(block_e, n_used, first, xs, w1, w3, w2)


def _combine_kernel(d0_ref, d1_ref, h_ref, r_ref, g_ref, y_ref, o_ref, ya, yb, sem, *, tm):
    base = pl.program_id(0) * tm

    def copy(r, dest_ref, buf, s):
        return pltpu.make_async_copy(y_ref.at[pl.ds(dest_ref[base + r], 1)], buf.at[pl.ds(r, 1)], sem.at[s])

    def start(r, c):
        copy(r, d0_ref, ya, 0).start()
        copy(r, d1_ref, yb, 1).start()
        return c

    lax.fori_loop(0, tm, start, 0, unroll=DMA_ISSUE_UNROLL)
    pltpu.make_async_copy(y_ref.at[pl.ds(0, tm)], ya, sem.at[0]).wait()
    pltpu.make_async_copy(y_ref.at[pl.ds(0, tm)], yb, sem.at[1]).wait()

    route = r_ref[...]
    lane = lax.broadcasted_iota(jnp.int32, route.shape, 1)
    w1 = jnp.sum(jnp.where(lane == 2, route, 0.0), axis=1, keepdims=True)
    w2 = jnp.sum(jnp.where(lane == 3, route, 0.0), axis=1, keepdims=True)
    h2 = h_ref[...] + (ya[...] * w1 + yb[...] * w2)
    o_ref[...] = _rms(h2, g_ref[...])


def _combine(dest0, dest1, h1, route, g, y):
    n, d = h1.shape
    tm = TILE
    return pl.pallas_call(
        functools.partial(_combine_kernel, tm=tm),
        out_shape=jax.ShapeDtypeStruct((n, d), F32),
        grid_spec=pltpu.PrefetchScalarGridSpec(
            num_scalar_prefetch=2,
            grid=(n // tm,),
            in_specs=[
                pl.BlockSpec((tm, d), lambda i, a, b: (i, 0)),
                pl.BlockSpec((tm, LANES), lambda i, a, b: (i, 0)),
                pl.BlockSpec((1, d), lambda i, a, b: (0, 0)),
                pl.BlockSpec(memory_space=pl.ANY),
            ],
            out_specs=pl.BlockSpec((tm, d), lambda i, a, b: (i, 0)),
            scratch_shapes=[pltpu.VMEM((tm, d), F32), pltpu.VMEM((tm, d), F32),
                            pltpu.SemaphoreType.DMA((2,))],
        ),
        compiler_params=_cparams(("arbitrary",)),
        name="combine",
    )(dest0, dest1, h1, route, g.reshape(1, d), y)


def _rope_tables(lp):
    half = MLA_ROPE_DIM // 2
    inv = ROPE_THETA ** (-jnp.arange(half, dtype=F32) / half)
    pos = (jnp.arange(lp) - (TILE - N_META)).astype(F32)
    ang = pos[:, None] * inv[None, :]
    cos, sin = jnp.cos(ang), jnp.sin(ang)
    z32 = jnp.zeros((lp, half), F32)
    z64 = jnp.zeros((lp, LANES - MLA_ROPE_DIM), F32)
    return (jnp.concatenate([cos, cos, z64], axis=1),
            jnp.concatenate([-sin, z32, z64], axis=1),
            jnp.concatenate([z32, sin, z64], axis=1))


def kernel(x, meta_tokens, norm1_g, w_in, b_gate, kv_norm_g, w_uk, w_uv, w_proj_a, w_proj_b, w_out,
           norm2_g, w_route_group, b_route_group, w_route_expert, b_route_expert, w1, w3, w2, final_g):
    b, seq, d = x.shape
    assert seq % TILE == 0 and TILE % CHUNK == 0 and N_META <= TILE
    lp = TILE + seq
    n_tok = b * lp
    n_real = b * seq
    n_exp = N_GROUPS * EXPERTS_PER_GROUP
    sb_w = SB_HEADS * SB_HEAD_DIM
    qk_dim = MLA_NOPE_DIM + MLA_ROPE_DIM
    mq_w = MLA_HEADS * qk_dim
    sb_scale = SB_HEAD_DIM ** -0.5
    mla_scale = qk_dim ** -0.5

    wi = w_in[0]
    o_q = 3 * sb_w
    o_c = o_q + mq_w
    o_r = o_c + MLA_KV_RANK
    o_g = o_r + MLA_ROPE_DIM
    w_sb = wi[:, :2 * sb_w].astype(BF16)
    w_sbv_t = wi[:, 2 * sb_w:o_q].T.astype(BF16)
    w_mq = jnp.pad(wi[:, o_q:o_c].reshape(d, MLA_HEADS, qk_dim),
                   ((0, 0), (0, 0), (0, MLA_QK_PAD - qk_dim))).reshape(d, MLA_HEADS * MLA_QK_PAD).astype(BF16)
    w_c = jnp.pad(wi[:, o_c:o_g], ((0, 0), (0, LANES - MLA_ROPE_DIM))).astype(BF16)
    w_g = wi[:, o_g:].astype(BF16)
    sb_colscale = jnp.concatenate([jnp.full((1, sb_w), sb_scale * LOG2E, F32), jnp.ones((1, sb_w), F32)], axis=1)
    wr = jnp.zeros((d, LANES), F32)
    wr = wr.at[:, :N_GROUPS].set(w_route_group[0]).at[:, EXPERT_LANE0:EXPERT_LANE0 + n_exp].set(w_route_expert[0])
    wr_hi = wr.astype(BF16)
    wr_mid = (wr - wr_hi.astype(F32)).astype(BF16)
    wr3 = jnp.stack([wr_hi, wr_mid])
    br = jnp.zeros((1, LANES), F32)
    br = br.at[0, :N_GROUPS].set(b_route_group[0]).at[0, EXPERT_LANE0:EXPERT_LANE0 + n_exp].set(b_route_expert[0])

    head = jnp.concatenate([jnp.zeros((TILE - N_META, d), F32), meta_tokens.astype(F32)], axis=0)
    hn = _norm1(x, head, norm1_g[0]).reshape(n_tok, d)
    tm = _row_tile(lp)
    tpb = lp // tm
    tables = _rope_tables(lp)
    tab_spec = pl.BlockSpec((tm, LANES), lambda j, i: (i % tpb, 0))
    row_spec = lambda tn: pl.BlockSpec((1, tn), lambda j, i: (0, j))

    sb_qk = _proj(_proj_scale_kernel, hn, w_sb, [sb_colscale], [row_spec(sb_w)], BF16, tm, sb_w)
    sb_vt = _proj_t(hn, w_sbv_t, tm)
    q_mla = _proj(functools.partial(_proj_mlaq_kernel, scale=mla_scale * LOG2E), hn, w_mq,
                  list(tables), [tab_spec] * 3, BF16, tm, min(4, MLA_HEADS) * MLA_QK_PAD)
    ckr = _proj(_proj_scale_kernel, hn, w_c, [jnp.ones((1, w_c.shape[1]), F32)],
                [row_spec(w_c.shape[1])], F32, tm, w_c.shape[1])
    gates = _proj(_proj_gate_kernel, hn, w_g, [b_gate[0].reshape(1, 2 * d)], [row_spec(d)], F32, tm, d)
    k_mla, vt_mla = _kvup(ckr, kv_norm_g[0], w_uk[0].astype(BF16), w_uv[0].T.astype(BF16), tables, tm, tpb)

    o_a = _sb_attention(sb_qk.reshape(b, lp, 2 * sb_w), sb_vt, seq)
    o_b = _mla_attention(q_mla.reshape(b, lp, -1), k_mla.reshape(b, lp, -1), vt_mla, seq)
    h1 = _merge(o_a, o_b, gates.reshape(b, lp, 2 * d), x,
                w_proj_a[0].astype(BF16), w_proj_b[0].astype(BF16), w_out[0].astype(BF16)).reshape(n_real, d)

    hp, route, counts = _route(h1, norm2_g[0], wr3, br)
    cnt = counts[0, EXPERT_LANE0:EXPERT_LANE0 + n_exp].astype(jnp.int32)
    padded = (cnt + ROUTE_BLOCK - 1) // ROUTE_BLOCK * ROUTE_BLOCK
    pends = jnp.cumsum(padded)
    pstarts = pends - padded
    n_blocks = (n_real * TOP_K + n_exp * (ROUTE_BLOCK - 1) + ROUTE_BLOCK - 1) // ROUTE_BLOCK
    n_used = (pends[-1] // ROUTE_BLOCK).astype(jnp.int32).reshape(1)
    blk_start = jnp.minimum(jnp.arange(n_blocks), n_used[0] - 1) * ROUTE_BLOCK
    block_e = jnp.minimum(jnp.sum(pends[None, :] <= blk_start[:, None], axis=1), n_exp - 1).astype(jnp.int32)
    ids = route[:, :6].astype(jnp.int32)
    expert_iota = jnp.arange(n_exp, dtype=jnp.int32)[None, :]
    start_of = lambda e: jnp.sum(jnp.where(e[:, None] == expert_iota, pstarts[None, :], 0), axis=1)
    dest0 = (start_of(ids[:, 0]) + ids[:, 4]).astype(jnp.int32)
    dest1 = (start_of(ids[:, 1]) + ids[:, 5]).astype(jnp.int32)

    xs = _dispatch(dest0, dest1, hp, n_blocks * ROUTE_BLOCK)
    first = jnp.concatenate([jnp.ones((1,), jnp.int32), (block_e[1:] != block_e[:-1]).astype(jnp.int32)])
    y = _experts(block_e, n_used, first, xs, w1[0], w3[0], w2[0])
    out = _combine(dest0, dest1, h1, route, final_g, y)
    return out.reshape(b, seq, d)
```

```python
import functools

import jax
import jax.numpy as jnp
from jax import lax
from jax.experimental import pallas as pl
from jax.experimental.pallas import tpu as pltpu

N_META = 16
CHUNK = 64
SB_HEADS = 8
SB_HEAD_DIM = 128
MLA_HEADS = 16
MLA_NOPE_DIM = 128
MLA_ROPE_DIM = 64
MLA_V_DIM = 128
MLA_KV_RANK = 512
ROPE_THETA = 10000.0
N_GROUPS = 4
EXPERTS_PER_GROUP = 8
TOP_K = 2
ROUTE_BLOCK = 256
RMS_EPS = 1e-6

TILE = 256
LANES = 128
MLA_QK_PAD = 256
ATTN_Q_TILES = 4
EXPERT_LANE0 = 8
VMEM_LIMIT = 56 * 1024 * 1024
NEG_BIG = -1e30
DMA_ISSUE_UNROLL = 8
DENOM_ROWS = 16
EXPERT_FF_SPLIT = 2
MLA_HEADS_PER_STEP = 2
STICK_GONE_LOG2 = 152.0

F32 = jnp.float32
BF16 = jnp.bfloat16
LOG2E = 1.4426950408889634


def _cparams(sem):
    return pltpu.CompilerParams(dimension_semantics=sem, vmem_limit_bytes=VMEM_LIMIT)


def _rms(v, g):
    ms = jnp.mean(v * v, axis=-1, keepdims=True)
    return v * lax.rsqrt(ms + RMS_EPS) * g


def _row_tile(lp):
    for t in (768, 512, 256):
        if lp % t == 0:
            return t
    raise ValueError(lp)


def _norm1_kernel(x_ref, head_ref, g_ref, o_ref):
    i = pl.program_id(1)

    @pl.when(i == 0)
    def _():
        o_ref[0] = _rms(head_ref[...], g_ref[...]).astype(BF16)

    @pl.when(i > 0)
    def _():
        o_ref[0] = _rms(x_ref[0], g_ref[...]).astype(BF16)


def _norm1(x, head, g):
    b, s, d = x.shape
    nt = s // TILE + 1
    return pl.pallas_call(
        _norm1_kernel,
        out_shape=jax.ShapeDtypeStruct((b, nt * TILE, d), BF16),
        grid=(b, nt),
        in_specs=[
            pl.BlockSpec((1, TILE, d), lambda bi, i: (bi, jnp.maximum(i - 1, 0), 0)),
            pl.BlockSpec((TILE, d), lambda bi, i: (0, 0)),
            pl.BlockSpec((1, d), lambda bi, i: (0, 0)),
        ],
        out_specs=pl.BlockSpec((1, TILE, d), lambda bi, i: (bi, i, 0)),
        compiler_params=_cparams(("parallel", "parallel")),
        name="norm1",
    )(x, head, g.reshape(1, d))


def _rope_rows(r, cos_t, sin_a, sin_b):
    return r * cos_t + pltpu.roll(r, 96, 1) * sin_a + pltpu.roll(r, 32, 1) * sin_b


def _proj_scale_kernel(x_ref, w_ref, s_ref, o_ref):
    acc = jnp.dot(x_ref[...], w_ref[...], preferred_element_type=F32)
    o_ref[...] = (acc * s_ref[...]).astype(o_ref.dtype)


def _proj_gate_kernel(x_ref, w_ref, b_ref, o_ref):
    acc = jnp.dot(x_ref[...], w_ref[...], preferred_element_type=F32) + b_ref[...]
    o_ref[...] = 1.0 / (1.0 + jnp.exp(-acc))


def _proj_mlaq_kernel(x_ref, wn_ref, wr_ref, cos_ref, sa_ref, sb_ref, o_ref, *, scale):
    x = x_ref[...]
    acc_n = jnp.dot(x, wn_ref[...], preferred_element_type=F32)
    acc_r = jnp.dot(x, wr_ref[...], preferred_element_type=F32)
    cos_t, sin_a, sin_b = cos_ref[...], sa_ref[...], sb_ref[...]
    low = lax.broadcasted_iota(jnp.int32, (x.shape[0], LANES), 1) < MLA_ROPE_DIM
    for hh in range(acc_n.shape[1] // MLA_NOPE_DIM):
        pair = acc_r[:, (hh // 2) * LANES:(hh // 2 + 1) * LANES]
        if hh % 2:
            pair = pltpu.roll(pair, MLA_ROPE_DIM, 1)
        rope = _rope_rows(jnp.where(low, pair, 0.0), cos_t, sin_a, sin_b) * scale
        c0 = hh * MLA_QK_PAD
        o_ref[:, c0:c0 + LANES] = (acc_n[:, hh * LANES:(hh + 1) * LANES] * scale).astype(o_ref.dtype)
        o_ref[:, c0 + LANES:c0 + 2 * LANES] = rope.astype(o_ref.dtype)


def _proj_mlaq(hn2d, wn, wr, tables, tm, tiles_per_batch, group, scale):
    m, k = hn2d.shape
    heads = wn.shape[1] // MLA_NOPE_DIM
    tab_spec = pl.BlockSpec((tm, LANES), lambda j, i: (i % tiles_per_batch, 0))
    return pl.pallas_call(
        functools.partial(_proj_mlaq_kernel, scale=scale),
        out_shape=jax.ShapeDtypeStruct((m, heads * MLA_QK_PAD), BF16),
        grid=(heads // group, m // tm),
        in_specs=[
            pl.BlockSpec((tm, k), lambda j, i: (i, 0)),
            pl.BlockSpec((k, group * MLA_NOPE_DIM), lambda j, i: (0, j)),
            pl.BlockSpec((k, group * MLA_ROPE_DIM), lambda j, i: (0, j)),
            tab_spec, tab_spec, tab_spec,
        ],
        out_specs=pl.BlockSpec((tm, group * MLA_QK_PAD), lambda j, i: (i, j)),
        compiler_params=_cparams(("parallel", "parallel")),
        name="proj_mlaq",
    )(hn2d, wn, wr, *tables)


def _store_lane_tiles(o_ref, val_t):
    for c in range(val_t.shape[1] // TILE):
        o_ref[c] = val_t[:, c * TILE:(c + 1) * TILE].astype(o_ref.dtype)


def _proj_t_kernel(x_ref, wt_ref, o_ref):
    acc_t = lax.dot_general(wt_ref[...], x_ref[...], (((1,), (1,)), ((), ())), preferred_element_type=F32)
    _store_lane_tiles(o_ref, acc_t)


def _proj_t(hn2d, wt, tm):
    m, k = hn2d.shape
    n = wt.shape[0]
    return pl.pallas_call(
        _proj_t_kernel,
        out_shape=jax.ShapeDtypeStruct((m // TILE, n, TILE), BF16),
        grid=(m // tm,),
        in_specs=[pl.BlockSpec((tm, k), lambda i: (i, 0)), pl.BlockSpec((n, k), lambda i: (0, 0))],
        out_specs=pl.BlockSpec((tm // TILE, n, TILE), lambda i: (i, 0, 0)),
        compiler_params=_cparams(("parallel",)),
        name="proj_t",
    )(hn2d, wt)


def _proj(kernel, hn2d, w, extras, extra_specs, out_dtype, tm, tn):
    m, k = hn2d.shape
    n = w.shape[1]
    return pl.pallas_call(
        kernel,
        out_shape=jax.ShapeDtypeStruct((m, n), out_dtype),
        grid=(n // tn, m // tm),
        in_specs=[
            pl.BlockSpec((tm, k), lambda j, i: (i, 0)),
            pl.BlockSpec((k, tn), lambda j, i: (0, j)),
        ] + extra_specs,
        out_specs=pl.BlockSpec((tm, tn), lambda j, i: (i, j)),
        compiler_params=_cparams(("parallel", "parallel")),
        name=getattr(kernel, "__name__", None) or kernel.func.__name__,
    )(hn2d, w, *extras)


def _kvup_kernel(c_ref, g_ref, wk_ref, wvt_ref, cos_ref, sa_ref, sb_ref, k_ref, vt_ref, *, rank):
    ckr = c_ref[...]
    cn = _rms(ckr[:, :rank], g_ref[...]).astype(BF16)
    kn = jnp.dot(cn, wk_ref[...], preferred_element_type=F32)
    vv_t = lax.dot_general(wvt_ref[...], cn, (((1,), (1,)), ((), ())), preferred_element_type=F32)
    rope = _rope_rows(ckr[:, rank:rank + LANES], cos_ref[...], sa_ref[...], sb_ref[...]).astype(BF16)
    for h in range(kn.shape[1] // MLA_NOPE_DIM):
        k_ref[:, h * MLA_QK_PAD:h * MLA_QK_PAD + LANES] = kn[:, h * LANES:(h + 1) * LANES].astype(BF16)
        k_ref[:, h * MLA_QK_PAD + LANES:(h + 1) * MLA_QK_PAD] = rope
    _store_lane_tiles(vt_ref, vv_t)


def _kvup(ckr, g, wk, wvt, tables, tm, tiles_per_batch):
    m, cw = ckr.shape
    rank = wk.shape[0]
    nk = wk.shape[1]
    nv = wvt.shape[0]
    heads = nk // MLA_NOPE_DIM
    tab_spec = pl.BlockSpec((tm, LANES), lambda i: (i % tiles_per_batch, 0))
    return pl.pallas_call(
        functools.partial(_kvup_kernel, rank=rank),
        out_shape=(jax.ShapeDtypeStruct((m, heads * MLA_QK_PAD), BF16),
                   jax.ShapeDtypeStruct((m // TILE, nv, TILE), BF16)),
        grid=(m // tm,),
        in_specs=[
            pl.BlockSpec((tm, cw), lambda i: (i, 0)),
            pl.BlockSpec((1, rank), lambda i: (0, 0)),
            pl.BlockSpec(wk.shape, lambda i: (0, 0)),
            pl.BlockSpec(wvt.shape, lambda i: (0, 0)),
            tab_spec, tab_spec, tab_spec,
        ],
        out_specs=(pl.BlockSpec((tm, heads * MLA_QK_PAD), lambda i: (i, 0)),
                   pl.BlockSpec((tm // TILE, nv, TILE), lambda i: (i, 0, 0))),
        compiler_params=_cparams(("parallel",)),
        name="kvup",
    )(ckr, g.reshape(1, rank), wk, wvt, *tables)


def _softplus2(z):
    neg_abs = pltpu.bitcast(pltpu.bitcast(z, jnp.uint32) | jnp.uint32(0x80000000), F32)
    return jnp.maximum(z, 0.0) + jnp.log2(1.0 + jnp.exp2(neg_abs))


def _sb_kernel(*refs, nsub, first_valid):
    q_refs = refs[:nsub]
    k_ref, vt_ref, o_ref, carry_ref, acc_ref = refs[nsub:]
    qs = pl.program_id(2)
    first_tile = 1 + nsub * qs
    carry_ref[...] = jnp.zeros_like(carry_ref)
    acc_ref[...] = jnp.zeros_like(acc_ref)
    trow = lax.broadcasted_iota(jnp.int32, (TILE, 2 * TILE), 0)
    tcol = lax.broadcasted_iota(jnp.int32, (TILE, 2 * TILE), 1) & (TILE - 1)
    tri2 = jnp.where(tcol >= trow, 1.0, 0.0).astype(BF16)

    def update(c0, c1, j, mask):
        lanes = slice(c0 * TILE, c1 * TILE)
        q = jnp.concatenate([q_refs[c][0] for c in range(c0, c1)], axis=0)
        kb = k_ref[0, pl.ds(pl.multiple_of(j * TILE, TILE), TILE), :]
        z = lax.dot_general(kb, q, (((1,), (1,)), ((), ())), preferred_element_type=F32)
        sp = _softplus2(z)
        if mask is not None:
            sp = jnp.where(mask, sp, 0.0)
        hi32 = pltpu.bitcast(pltpu.bitcast(sp, jnp.uint32) & jnp.uint32(0xFFFF0000), F32)
        parts = jnp.concatenate([hi32.astype(BF16), (sp - hi32).astype(BF16)], axis=0)
        cs = jnp.dot(tri2, parts, preferred_element_type=F32)
        carry = carry_ref[:, lanes]
        a = jnp.exp2(z - cs - carry)
        if mask is not None:
            a = jnp.where(mask, a, 0.0)
        acc_ref[:, lanes] = acc_ref[:, lanes] + jnp.dot(vt_ref[j], a.astype(BF16), preferred_element_type=F32)
        carry_ref[:, lanes] = carry + cs[0:1, :]

    def stick_left(c0, c1):
        return jnp.min(carry_ref[:, c0 * TILE:c1 * TILE]) < STICK_GONE_LOG2

    def diag_mask(nq):
        key = lax.broadcasted_iota(jnp.int32, (TILE, nq * TILE), 0)
        qry = lax.broadcasted_iota(jnp.int32, (TILE, nq * TILE), 1)
        return (qry >= TILE) | (key < qry)

    half = nsub // 2
    for c in reversed(range(nsub)):
        near = min(c + half, nsub)
        update(c, near, first_tile + c, diag_mask(near - c))
        if near < nsub:
            @pl.when(stick_left(near, nsub))
            def _():
                update(near, nsub, first_tile + c, None)

    for c0, c1 in ((0, half), (half, nsub)):
        def body(state):
            j, _ = state
            update(c0, c1, j, None)
            return j - 1, stick_left(c0, c1)

        _, alive = lax.while_loop(lambda st: (st[0] > 0) & st[1], body, (first_tile - 1, stick_left(c0, c1)))

        @pl.when(alive)
        def _():
            key0 = lax.broadcasted_iota(jnp.int32, (TILE, (c1 - c0) * TILE), 0)
            update(c0, c1, 0, key0 >= first_valid)

    o_ref[0] = acc_ref[...].T.astype(o_ref.dtype)


def _q_specs(nsub, width, col0):
    return [pl.BlockSpec((1, TILE, width),
                         functools.partial(lambda bi, h, i, c: (bi, nsub * i + 1 + c, col0 + h), c=c))
            for c in range(nsub)]


def _sb_attention(qk, vt, seq):
    b, lp, w2 = qk.shape
    heads = w2 // (2 * SB_HEAD_DIM)
    nsub = ATTN_Q_TILES
    rows = nsub * TILE
    return pl.pallas_call(
        functools.partial(_sb_kernel, nsub=nsub, first_valid=TILE - N_META),
        out_shape=jax.ShapeDtypeStruct((b, seq, heads * SB_HEAD_DIM), BF16),
        grid=(b, heads, seq // rows),
        in_specs=_q_specs(nsub, SB_HEAD_DIM, 0) + [
            pl.BlockSpec((1, lp, SB_HEAD_DIM), lambda bi, h, i: (bi, 0, heads + h)),
            pl.BlockSpec((lp // TILE, SB_HEAD_DIM, TILE), lambda bi, h, i: (bi, h, 0)),
        ],
        out_specs=pl.BlockSpec((1, rows, SB_HEAD_DIM), lambda bi, h, i: (bi, i, h)),
        scratch_shapes=[pltpu.VMEM((1, rows), F32), pltpu.VMEM((SB_HEAD_DIM, rows), F32)],
        compiler_params=_cparams(("parallel", "parallel", "arbitrary")),
        name="sb_attention",
    )(*([qk] * (nsub + 1)), vt)


def _mla_kernel(*refs, nsub, nheads, first_valid):
    nq = nsub * nheads
    q_refs = refs[:nq]
    k_ref, vt_ref, bias_ref, o_ref = refs[nq:nq + 4]
    scratch = refs[nq + 4:]
    qs = pl.program_id(2)
    rows = nsub * TILE
    blk_keys = 2 * TILE
    n_full = (nsub * qs) // 2

    def with_ones(vtb):
        return jnp.concatenate([vtb, jnp.ones((DENOM_ROWS, vtb.shape[1]), BF16)], axis=0)

    class Head:
        def __init__(self, g):
            self.qcols = slice(g * MLA_QK_PAD, (g + 1) * MLA_QK_PAD)
            self.vrows = slice(g * MLA_V_DIM, (g + 1) * MLA_V_DIM)
            self.q_refs = q_refs[g * nsub:(g + 1) * nsub]
            (self.m_ref, self.acc_ref, s0, s1, p0, p1, x0, x1) = scratch[g * 8:(g + 1) * 8]
            self.even = (s0, x0, p0)
            self.odd = (s1, x1, p1)

        def queries(self):
            return jnp.concatenate([r[0] for r in self.q_refs], axis=0)

        def vt(self, j):
            return vt_ref[j, self.vrows, :]

        def scores(self, blk, s_ref, smax_ref):
            kb = k_ref[0, pl.ds(pl.multiple_of((1 + 2 * blk) * TILE, TILE), blk_keys), self.qcols]
            s = lax.dot_general(kb, self.queries(), (((1,), (1,)), ((), ())), preferred_element_type=F32)
            s_ref[...] = s
            smax_ref[...] = jnp.max(s, axis=0, keepdims=True)

        def softmax(self, s_ref, smax_ref, p_ref, bias):
            if bias is None:
                s = s_ref[...]
                smax = smax_ref[...]
            else:
                s = s_ref[...] + bias
                smax = jnp.max(s, axis=0, keepdims=True)
            m_old = self.m_ref[...]
            m_new = jnp.maximum(m_old, smax)
            alpha = jnp.exp2(m_old - m_new)
            p = jnp.exp2(s - m_new)
            self.m_ref[...] = m_new
            p_ref[...] = p.astype(BF16)
            return alpha

        def values(self, blk, p_ref):
            j = 1 + 2 * blk
            vtb = with_ones(jnp.concatenate([self.vt(j), self.vt(j + 1)], axis=1))
            return jnp.dot(vtb, p_ref[...], preferred_element_type=F32)

        def stage(self, blk, cur, nxt, bias=None, lookahead=True):
            (s_cur, x_cur, p_cur), (s_nxt, x_nxt, p_prev) = cur, nxt
            if lookahead:
                self.scores(blk + 1, s_nxt, x_nxt)
            alpha = self.softmax(s_cur, x_cur, p_cur, bias)
            self.acc_ref[...] = alpha * (self.acc_ref[...] + self.values(jnp.maximum(blk - 1, 0), p_prev))

        def start(self):
            s = lax.dot_general(k_ref[0, first_valid:TILE, self.qcols], self.queries(),
                                (((1,), (1,)), ((), ())), preferred_element_type=F32)
            m0 = jnp.max(s, axis=0, keepdims=True)
            p = jnp.exp2(s - m0)
            self.m_ref[...] = m0
            p_tile = jnp.concatenate([jnp.zeros((first_valid, rows), BF16), p.astype(BF16)], axis=0)
            self.acc_ref[...] = jnp.dot(with_ones(self.vt(0)), p_tile, preferred_element_type=F32)
            self.scores(0, self.even[0], self.even[1])
            self.odd[2][...] = jnp.zeros_like(self.odd[2])

        def finish(self):
            acc = self.acc_ref[...] + self.values(n_full + 1, self.odd[2])
            out = (acc[:MLA_V_DIM] / acc[MLA_V_DIM:MLA_V_DIM + 1]).T
            o_ref[0, :, self.vrows] = out.astype(o_ref.dtype)

    heads = [Head(g) for g in range(nheads)]
    for hd in heads:
        hd.start()

    def body(u, carry):
        for hd in heads:
            hd.stage(2 * u, hd.even, hd.odd)
        for hd in heads:
            hd.stage(2 * u + 1, hd.odd, hd.even)
        return carry

    lax.fori_loop(0, n_full // 2, body, 0)
    for hd in heads:
        hd.stage(n_full, hd.even, hd.odd, bias=bias_ref[:blk_keys, :])
    for hd in heads:
        hd.stage(n_full + 1, hd.odd, hd.even, bias=bias_ref[blk_keys:, :], lookahead=False)
    for hd in heads:
        hd.finish()


def _mla_attention(q, k, vt, seq):
    b, lp, _ = q.shape
    heads = vt.shape[1] // MLA_V_DIM
    nsub = ATTN_Q_TILES
    nheads = min(MLA_HEADS_PER_STEP, heads)
    assert nsub == 4, "the kernel visits the query-overlapping keys as exactly two 2-tile blocks"
    assert heads % nheads == 0
    rows = nsub * TILE
    shift = CHUNK.bit_length() - 1
    key = lax.broadcasted_iota(jnp.int32, (rows, rows), 0)
    qry = lax.broadcasted_iota(jnp.int32, (rows, rows), 1)
    bias = jnp.where((key >> shift) <= (qry >> shift), 0.0, NEG_BIG).astype(F32)
    q_specs = [pl.BlockSpec((1, TILE, MLA_QK_PAD),
                            functools.partial(lambda bi, h, i, g, c: (bi, nsub * i + 1 + c, nheads * h + g), g=g, c=c))
               for g in range(nheads) for c in range(nsub)]
    per_head_scratch = [pltpu.VMEM((1, rows), F32),
                        pltpu.VMEM((MLA_V_DIM + DENOM_ROWS, rows), F32),
                        pltpu.VMEM((2 * TILE, rows), F32), pltpu.VMEM((2 * TILE, rows), F32),
                        pltpu.VMEM((2 * TILE, rows), BF16), pltpu.VMEM((2 * TILE, rows), BF16),
                        pltpu.VMEM((1, rows), F32), pltpu.VMEM((1, rows), F32)]
    return pl.pallas_call(
        functools.partial(_mla_kernel, nsub=nsub, nheads=nheads, first_valid=TILE - N_META),
        out_shape=jax.ShapeDtypeStruct((b, seq, heads * MLA_V_DIM), BF16),
        grid=(b, heads // nheads, seq // rows),
        in_specs=q_specs + [
            pl.BlockSpec((1, lp, nheads * MLA_QK_PAD), lambda bi, h, i: (bi, 0, h)),
            pl.BlockSpec((lp // TILE, nheads * MLA_V_DIM, TILE), lambda bi, h, i: (bi, h, 0)),
            pl.BlockSpec((rows, rows), lambda bi, h, i: (0, 0), pipeline_mode=pl.Buffered(1)),
        ],
        out_specs=pl.BlockSpec((1, rows, nheads * MLA_V_DIM), lambda bi, h, i: (bi, i, h)),
        scratch_shapes=per_head_scratch * nheads,
        compiler_params=_cparams(("parallel", "parallel", "arbitrary")),
        name="mla_attention",
    )(*([q] * (nsub * nheads)), k, vt, bias)


def _merge_kernel(oa_ref, ob_ref, ga_ref, gb_ref, x_ref, wpa_ref, wpb_ref, wo_ref, o_ref):
    pa = jnp.dot(oa_ref[0], wpa_ref[...], preferred_element_type=F32)
    pb = jnp.dot(ob_ref[0], wpb_ref[...], preferred_element_type=F32)
    y = ga_ref[0] * pa + gb_ref[0] * pb
    o_ref[0] = x_ref[0] + jnp.dot(y.astype(BF16), wo_ref[...], preferred_element_type=F32)


def _merge(oa, ob, gates, x, wpa, wpb, wo):
    b, s, d = x.shape
    nt = s // TILE
    resident = lambda w: pl.BlockSpec(w.shape, lambda bi, i: (0, 0), pipeline_mode=pl.Buffered(1))
    return pl.pallas_call(
        _merge_kernel,
        out_shape=jax.ShapeDtypeStruct((b, s, d), F32),
        grid=(b, nt),
        in_specs=[
            pl.BlockSpec((1, TILE, oa.shape[2]), lambda bi, i: (bi, i, 0)),
            pl.BlockSpec((1, TILE, ob.shape[2]), lambda bi, i: (bi, i, 0)),
            pl.BlockSpec((1, TILE, d), lambda bi, i: (bi, i + 1, 0)),
            pl.BlockSpec((1, TILE, d), lambda bi, i: (bi, i + 1, 1)),
            pl.BlockSpec((1, TILE, d), lambda bi, i: (bi, i, 0)),
            resident(wpa), resident(wpb), resident(wo),
        ],
        out_specs=pl.BlockSpec((1, TILE, d), lambda bi, i: (bi, i, 0)),
        compiler_params=_cparams(("parallel", "parallel")),
        name="merge",
    )(oa, ob, gates, gates, x, wpa, wpb, wo)


def _split2(a):
    hi = a.astype(BF16)
    return hi, (a - hi.astype(F32)).astype(BF16)


def _route_kernel(h_ref, g_ref, wr_ref, br_ref, hp_ref, r_ref, cnt_ref, carry_ref, *, tm, half):
    i = pl.program_id(0)

    @pl.when(i == 0)
    def _():
        carry_ref[...] = jnp.zeros_like(carry_ref)

    hn = _rms(h_ref[...], g_ref[...])

    lo_bits = pltpu.bitcast(hn[:, :half].astype(BF16).astype(F32), jnp.uint32)
    hi_bits = pltpu.bitcast(hn[:, half:].astype(BF16).astype(F32), jnp.uint32)
    hp_ref[...] = (hi_bits & jnp.uint32(0xFFFF0000)) | (lo_bits >> 16)

    a_hi, a_mid = _split2(hn)
    w_hi, w_mid = wr_ref[0], wr_ref[1]
    dot = lambda a, w: jnp.dot(a, w, preferred_element_type=F32)
    lg = (dot(a_mid, w_hi) + dot(a_hi, w_mid) + dot(a_hi, w_hi)) + br_ref[...]

    lane = lax.broadcasted_iota(jnp.int32, lg.shape, 1)
    rmax = lambda v: jnp.max(v, axis=1, keepdims=True)
    rmin = lambda v: jnp.min(v, axis=1, keepdims=True)
    rsum = lambda v: jnp.sum(v, axis=1, keepdims=True)

    gmask = lane < N_GROUPS
    gl = jnp.where(gmask, lg, -jnp.inf)
    gmax = rmax(gl)
    gsel = rmin(jnp.where(gl == gmax, lane, LANES))
    p_g = 1.0 / rsum(jnp.where(gmask, jnp.exp(lg - gmax), 0.0))

    e_lo = EXPERT_LANE0 + gsel * EXPERTS_PER_GROUP
    emask = (lane >= e_lo) & (lane < e_lo + EXPERTS_PER_GROUP)
    emax = rmax(jnp.where(emask, lg, -jnp.inf))
    ex = jnp.where(emask, jnp.exp(lg - emax), 0.0)
    prob = jnp.where(emask, ex / rsum(ex), -1.0)
    top1 = rmax(prob)
    i1 = rmin(jnp.where(prob == top1, lane, LANES))
    prob2 = jnp.where(lane == i1, -1.0, prob)
    top2 = rmax(prob2)
    i2 = rmin(jnp.where(prob2 == top2, lane, LANES))
    denom = top1 + top2
    w1 = p_g * top1 / denom
    w2 = p_g * top2 / denom

    sel = ((lane == i1) | (lane == i2))
    row = lax.broadcasted_iota(jnp.int32, (tm, tm), 0)
    col = lax.broadcasted_iota(jnp.int32, (tm, tm), 1)
    before = (col < row).astype(BF16)
    prefix = dot(before, jnp.where(sel, 1.0, 0.0).astype(BF16)) + carry_ref[...]
    rank1 = rsum(jnp.where(lane == i1, prefix, 0.0))
    rank2 = rsum(jnp.where(lane == i2, prefix, 0.0))
    carry_ref[...] = carry_ref[...] + jnp.sum(jnp.where(sel, 1.0, 0.0), axis=0, keepdims=True)
    cnt_ref[...] = carry_ref[...]

    e1 = (i1 - EXPERT_LANE0).astype(F32)
    e2 = (i2 - EXPERT_LANE0).astype(F32)
    out = jnp.zeros(lg.shape, F32)
    for k, val in enumerate((e1, e2, w1, w2, rank1, rank2)):
        out = jnp.where(lane == k, val, out)
    r_ref[...] = out


def _route(h1, g, wr3, br):
    n, d = h1.shape
    tm = TILE
    return pl.pallas_call(
        functools.partial(_route_kernel, tm=tm, half=d // 2),
        out_shape=(jax.ShapeDtypeStruct((n, d // 2), jnp.uint32),
                   jax.ShapeDtypeStruct((n, LANES), F32),
                   jax.ShapeDtypeStruct((1, LANES), F32)),
        grid=(n // tm,),
        in_specs=[
            pl.BlockSpec((tm, d), lambda i: (i, 0)),
            pl.BlockSpec((1, d), lambda i: (0, 0)),
            pl.BlockSpec(wr3.shape, lambda i: (0, 0, 0)),
            pl.BlockSpec((1, LANES), lambda i: (0, 0)),
        ],
        out_specs=(pl.BlockSpec((tm, d // 2), lambda i: (i, 0)),
                   pl.BlockSpec((tm, LANES), lambda i: (i, 0)),
                   pl.BlockSpec((1, LANES), lambda i: (0, 0))),
        scratch_shapes=[pltpu.VMEM((1, LANES), F32)],
        compiler_params=_cparams(("arbitrary",)),
        name="route",
    )(h1, g.reshape(1, d), wr3, br)


def _dispatch_kernel(d0_ref, d1_ref, src_ref, init_ref, dst_ref, sem, *, tm):
    del init_ref
    base = pl.program_id(0) * tm

    def copy(r, dest_ref, s):
        return pltpu.make_async_copy(src_ref.at[pl.ds(r, 1)], dst_ref.at[pl.ds(dest_ref[base + r], 1)], sem.at[s])

    def start(r, c):
        copy(r, d0_ref, 0).start()
        copy(r, d1_ref, 1).start()
        return c

    lax.fori_loop(0, tm, start, 0, unroll=DMA_ISSUE_UNROLL)
    for s in range(2):
        pltpu.make_async_copy(src_ref, dst_ref.at[pl.ds(0, tm)], sem.at[s]).wait()


def _dispatch(dest0, dest1, hp, p_rows):
    n, w = hp.shape
    tm = TILE
    init = jnp.zeros((p_rows, w), hp.dtype)
    return pl.pallas_call(
        functools.partial(_dispatch_kernel, tm=tm),
        out_shape=jax.ShapeDtypeStruct((p_rows, w), hp.dtype),
        grid_spec=pltpu.PrefetchScalarGridSpec(
            num_scalar_prefetch=2,
            grid=(n // tm,),
            in_specs=[pl.BlockSpec((tm, w), lambda i, a, b: (i, 0)), pl.BlockSpec(memory_space=pl.ANY)],
            out_specs=pl.BlockSpec(memory_space=pl.ANY),
            scratch_shapes=[pltpu.SemaphoreType.DMA((2,))],
        ),
        input_output_aliases={3: 0},
        compiler_params=_cparams(("arbitrary",)),
        name="dispatch",
    )(dest0, dest1, hp, init)


def _expert_kernel(be_ref, nu_ref, first_ref, x_ref, w1_ref, w3_ref, w2_ref, y_ref, c1_ref, c3_ref, c2_ref):
    del be_ref
    j = pl.program_id(0)
    h = pl.program_id(1)
    live = j < nu_ref[0]

    @pl.when(live & (first_ref[j] == 1))
    def _():
        c1_ref[h] = w1_ref[0].astype(BF16)
        c3_ref[h] = w3_ref[0].astype(BF16)
        c2_ref[h] = w2_ref[0].astype(BF16)

    @pl.when(live)
    def _():
        xw = x_ref[...]
        lo = pltpu.bitcast(xw << 16, F32).astype(BF16)
        hi = pltpu.bitcast(xw & jnp.uint32(0xFFFF0000), F32).astype(BF16)
        xb = jnp.concatenate([lo, hi], axis=1)
        a = jnp.dot(xb, c1_ref[h], preferred_element_type=F32)
        g = jnp.dot(xb, c3_ref[h], preferred_element_type=F32)
        hid = (a * (1.0 / (1.0 + jnp.exp(-a))) * g).astype(BF16)
        part = jnp.dot(hid, c2_ref[h], preferred_element_type=F32)

        @pl.when(h == 0)
        def _():
            y_ref[...] = part

        @pl.when(h > 0)
        def _():
            y_ref[...] = y_ref[...] + part

    @pl.when(jnp.logical_not(live) & (h == 0))
    def _():
        y_ref[...] = jnp.zeros_like(y_ref)


def _experts(block_e, n_used, first, xs, w1, w3, w2):
    p_rows, half = xs.shape
    _, d, ff = w1.shape
    nb = p_rows // ROUTE_BLOCK
    nh = EXPERT_FF_SPLIT
    fh = ff // nh
    last = lambda j, nu: jnp.minimum(j, nu[0] - 1)

    def half_of(j, h, nu, fr):
        return jnp.where((fr[last(j, nu)] == 1) & (j < nu[0]), h, nh - 1)

    return pl.pallas_call(
        _expert_kernel,
        out_shape=jax.ShapeDtypeStruct((p_rows, d), F32),
        grid_spec=pltpu.PrefetchScalarGridSpec(
            num_scalar_prefetch=3,
            grid=(nb, nh),
            in_specs=[
                pl.BlockSpec((ROUTE_BLOCK, half), lambda j, h, be, nu, fr: (last(j, nu), 0)),
                pl.BlockSpec((1, d, fh), lambda j, h, be, nu, fr: (be[last(j, nu)], 0, half_of(j, h, nu, fr))),
                pl.BlockSpec((1, d, fh), lambda j, h, be, nu, fr: (be[last(j, nu)], 0, half_of(j, h, nu, fr))),
                pl.BlockSpec((1, fh, d), lambda j, h, be, nu, fr: (be[last(j, nu)], half_of(j, h, nu, fr), 0)),
            ],
            out_specs=pl.BlockSpec((ROUTE_BLOCK, d), lambda j, h, be, nu, fr: (j, 0)),
            scratch_shapes=[pltpu.VMEM((nh, d, fh), BF16), pltpu.VMEM((nh, d, fh), BF16),
                            pltpu.VMEM((nh, fh, d), BF16)],
        ),
        compiler_params=_cparams(("arbitrary", "arbitrary")),
        name="experts",
    )(block_e, n_used, first, xs, w1, w3, w2)


def _combine_kernel(d0_ref, d1_ref, h_ref, r_ref, g_ref, y_ref, o_ref, ya, yb, sem, *, tm):
    base = pl.program_id(0) * tm

    def copy(r, dest_ref, buf, s):
        return pltpu.make_async_copy(y_ref.at[pl.ds(dest_ref[base + r], 1)], buf.at[pl.ds(r, 1)], sem.at[s])

    def start(r, c):
        copy(r, d0_ref, ya, 0).start()
        copy(r, d1_ref, yb, 1).start()
        return c

    lax.fori_loop(0, tm, start, 0, unroll=DMA_ISSUE_UNROLL)
    pltpu.make_async_copy(y_ref.at[pl.ds(0, tm)], ya, sem.at[0]).wait()
    pltpu.make_async_copy(y_ref.at[pl.ds(0, tm)], yb, sem.at[1]).wait()

    route = r_ref[...]
    lane = lax.broadcasted_iota(jnp.int32, route.shape, 1)
    w1 = jnp.sum(jnp.where(lane == 2, route, 0.0), axis=1, keepdims=True)
    w2 = jnp.sum(jnp.where(lane == 3, route, 0.0), axis=1, keepdims=True)
    h2 = h_ref[...] + (ya[...] * w1 + yb[...] * w2)
    o_ref[...] = _rms(h2, g_ref[...])


def _combine(dest0, dest1, h1, route, g, y):
    n, d = h1.shape
    tm = TILE
    return pl.pallas_call(
        functools.partial(_combine_kernel, tm=tm),
        out_shape=jax.ShapeDtypeStruct((n, d), F32),
        grid_spec=pltpu.PrefetchScalarGridSpec(
            num_scalar_prefetch=2,
            grid=(n // tm,),
            in_specs=[
                pl.BlockSpec((tm, d), lambda i, a, b: (i, 0)),
                pl.BlockSpec((tm, LANES), lambda i, a, b: (i, 0)),
                pl.BlockSpec((1, d), lambda i, a, b: (0, 0)),
                pl.BlockSpec(memory_space=pl.ANY),
            ],
            out_specs=pl.BlockSpec((tm, d), lambda i, a, b: (i, 0)),
            scratch_shapes=[pltpu.VMEM((tm, d), F32), pltpu.VMEM((tm, d), F32),
                            pltpu.SemaphoreType.DMA((2,))],
        ),
        compiler_params=_cparams(("arbitrary",)),
        name="combine",
    )(dest0, dest1, h1, route, g.reshape(1, d), y)


def _rope_tables(lp):
    half = MLA_ROPE_DIM // 2
    inv = ROPE_THETA ** (-jnp.arange(half, dtype=F32) / half)
    pos = (jnp.arange(lp) - (TILE - N_META)).astype(F32)
    ang = pos[:, None] * inv[None, :]
    cos, sin = jnp.cos(ang), jnp.sin(ang)
    z32 = jnp.zeros((lp, half), F32)
    z64 = jnp.zeros((lp, LANES - MLA_ROPE_DIM), F32)
    return (jnp.concatenate([cos, cos, z64], axis=1),
            jnp.concatenate([-sin, z32, z64], axis=1),
            jnp.concatenate([z32, sin, z64], axis=1))


def kernel(x, meta_tokens, norm1_g, w_in, b_gate, kv_norm_g, w_uk, w_uv, w_proj_a, w_proj_b, w_out,
           norm2_g, w_route_group, b_route_group, w_route_expert, b_route_expert, w1, w3, w2, final_g):
    b, seq, d = x.shape
    assert seq % TILE == 0 and TILE % CHUNK == 0 and N_META <= TILE
    lp = TILE + seq
    n_tok = b * lp
    n_real = b * seq
    n_exp = N_GROUPS * EXPERTS_PER_GROUP
    sb_w = SB_HEADS * SB_HEAD_DIM
    qk_dim = MLA_NOPE_DIM + MLA_ROPE_DIM
    mq_w = MLA_HEADS * qk_dim
    sb_scale = SB_HEAD_DIM ** -0.5
    mla_scale = qk_dim ** -0.5

    wi = w_in[0]
    o_q = 3 * sb_w
    o_c = o_q + mq_w
    o_r = o_c + MLA_KV_RANK
    o_g = o_r + MLA_ROPE_DIM
    w_sb = wi[:, :2 * sb_w].astype(BF16)
    w_sbv_t = wi[:, 2 * sb_w:o_q].T.astype(BF16)
    w_mq = wi[:, o_q:o_c].reshape(d, MLA_HEADS, qk_dim)
    w_mq_nope = w_mq[:, :, :MLA_NOPE_DIM].reshape(d, MLA_HEADS * MLA_NOPE_DIM).astype(BF16)
    w_mq_rope = w_mq[:, :, MLA_NOPE_DIM:].reshape(d, MLA_HEADS * MLA_ROPE_DIM).astype(BF16)
    w_c = jnp.pad(wi[:, o_c:o_g], ((0, 0), (0, LANES - MLA_ROPE_DIM))).astype(BF16)
    w_g = wi[:, o_g:].astype(BF16)
    sb_colscale = jnp.concatenate([jnp.full((1, sb_w), sb_scale * LOG2E, F32), jnp.ones((1, sb_w), F32)], axis=1)
    wr = jnp.zeros((d, LANES), F32)
    wr = wr.at[:, :N_GROUPS].set(w_route_group[0]).at[:, EXPERT_LANE0:EXPERT_LANE0 + n_exp].set(w_route_expert[0])
    wr_hi = wr.astype(BF16)
    wr_mid = (wr - wr_hi.astype(F32)).astype(BF16)
    wr3 = jnp.stack([wr_hi, wr_mid])
    br = jnp.zeros((1, LANES), F32)
    br = br.at[0, :N_GROUPS].set(b_route_group[0]).at[0, EXPERT_LANE0:EXPERT_LANE0 + n_exp].set(b_route_expert[0])

    head = jnp.concatenate([jnp.zeros((TILE - N_META, d), F32), meta_tokens.astype(F32)], axis=0)
    hn = _norm1(x, head, norm1_g[0]).reshape(n_tok, d)
    tm = _row_tile(lp)
    tpb = lp // tm
    tables = _rope_tables(lp)
    row_spec = lambda tn: pl.BlockSpec((1, tn), lambda j, i: (0, j))

    sb_qk = _proj(_proj_scale_kernel, hn, w_sb, [sb_colscale], [row_spec(sb_w)], BF16, tm, sb_w)
    sb_vt = _proj_t(hn, w_sbv_t, tm)
    q_mla = _proj_mlaq(hn, w_mq_nope, w_mq_rope, tables, tm, tpb, min(4, MLA_HEADS), mla_scale * LOG2E)
    ckr = _proj(_proj_scale_kernel, hn, w_c, [jnp.ones((1, w_c.shape[1]), F32)],
                [row_spec(w_c.shape[1])], F32, tm, w_c.shape[1])
    gates = _proj(_proj_gate_kernel, hn, w_g, [b_gate[0].reshape(1, 2 * d)], [row_spec(d)], F32, tm, d)
    k_mla, vt_mla = _kvup(ckr, kv_norm_g[0], w_uk[0].astype(BF16), w_uv[0].T.astype(BF16), tables, tm, tpb)

    o_a = _sb_attention(sb_qk.reshape(b, lp, 2 * sb_w), sb_vt, seq)
    o_b = _mla_attention(q_mla.reshape(b, lp, -1), k_mla.reshape(b, lp, -1), vt_mla, seq)
    h1 = _merge(o_a, o_b, gates.reshape(b, lp, 2 * d), x,
                w_proj_a[0].astype(BF16), w_proj_b[0].astype(BF16), w_out[0].astype(BF16)).reshape(n_real, d)

    hp, route, counts = _route(h1, norm2_g[0], wr3, br)
    cnt = counts[0, EXPERT_LANE0:EXPERT_LANE0 + n_exp].astype(jnp.int32)
    padded = (cnt + ROUTE_BLOCK - 1) // ROUTE_BLOCK * ROUTE_BLOCK
    pends = jnp.cumsum(padded)
    pstarts = pends - padded
    n_blocks = (n_real * TOP_K + n_exp * (ROUTE_BLOCK - 1) + ROUTE_BLOCK - 1) // ROUTE_BLOCK
    n_used = (pends[-1] // ROUTE_BLOCK).astype(jnp.int32).reshape(1)
    blk_start = jnp.minimum(jnp.arange(n_blocks), n_used[0] - 1) * ROUTE_BLOCK
    block_e = jnp.minimum(jnp.sum(pends[None, :] <= blk_start[:, None], axis=1), n_exp - 1).astype(jnp.int32)
    ids = route[:, :6].astype(jnp.int32)
    expert_iota = jnp.arange(n_exp, dtype=jnp.int32)[None, :]
    start_of = lambda e: jnp.sum(jnp.where(e[:, None] == expert_iota, pstarts[None, :], 0), axis=1)
    dest0 = (start_of(ids[:, 0]) + ids[:, 4]).astype(jnp.int32)
    dest1 = (start_of(ids[:, 1]) + ids[:, 5]).astype(jnp.int32)

    xs = _dispatch(dest0, dest1, hp, n_blocks * ROUTE_BLOCK)
    first = jnp.concatenate([jnp.ones((1,), jnp.int32), (block_e[1:] != block_e[:-1]).astype(jnp.int32)])
    y = _experts(block_e, n_used, first, xs, w1[0], w3[0], w2[0])
    out = _combine(dest0, dest1, h1, route, final_g, y)
    return out.reshape(b, seq, d)
```

```python
import functools

import jax
import jax.numpy as jnp
from jax import lax
from jax.experimental import pallas as pl
from jax.experimental.pallas import tpu as pltpu

N_META = 16
CHUNK = 64
SB_HEADS = 8
SB_HEAD_DIM = 128
MLA_HEADS = 16
MLA_NOPE_DIM = 128
MLA_ROPE_DIM = 64
MLA_V_DIM = 128
MLA_KV_RANK = 512
ROPE_THETA = 10000.0
N_GROUPS = 4
EXPERTS_PER_GROUP = 8
TOP_K = 2
ROUTE_BLOCK = 256
RMS_EPS = 1e-6

TILE = 256
LANES = 128
MLA_QK_PAD = 256
ATTN_Q_TILES = 4
EXPERT_LANE0 = 8
VMEM_LIMIT = 56 * 1024 * 1024
NEG_BIG = -1e30
DMA_ISSUE_UNROLL = 8
COMBINE_CHUNK = 64
DENOM_ROWS = 16
EXPERT_FF_SPLIT = 2
MLA_HEADS_PER_STEP = 2
STICK_GONE_LOG2 = 152.0

F32 = jnp.float32
BF16 = jnp.bfloat16
LOG2E = 1.4426950408889634


def _cparams(sem):
    return pltpu.CompilerParams(dimension_semantics=sem, vmem_limit_bytes=VMEM_LIMIT)


def _rms(v, g):
    ms = jnp.mean(v * v, axis=-1, keepdims=True)
    return v * lax.rsqrt(ms + RMS_EPS) * g


def _row_tile(lp):
    for t in (768, 512, 256):
        if lp % t == 0:
            return t
    raise ValueError(lp)


def _norm1_kernel(x_ref, head_ref, g_ref, o_ref):
    i = pl.program_id(1)

    @pl.when(i == 0)
    def _():
        o_ref[0] = _rms(head_ref[...], g_ref[...]).astype(BF16)

    @pl.when(i > 0)
    def _():
        o_ref[0] = _rms(x_ref[0], g_ref[...]).astype(BF16)


def _norm1(x, head, g):
    b, s, d = x.shape
    nt = s // TILE + 1
    return pl.pallas_call(
        _norm1_kernel,
        out_shape=jax.ShapeDtypeStruct((b, nt * TILE, d), BF16),
        grid=(b, nt),
        in_specs=[
            pl.BlockSpec((1, TILE, d), lambda bi, i: (bi, jnp.maximum(i - 1, 0), 0)),
            pl.BlockSpec((TILE, d), lambda bi, i: (0, 0)),
            pl.BlockSpec((1, d), lambda bi, i: (0, 0)),
        ],
        out_specs=pl.BlockSpec((1, TILE, d), lambda bi, i: (bi, i, 0)),
        compiler_params=_cparams(("parallel", "parallel")),
        name="norm1",
    )(x, head, g.reshape(1, d))


def _rope_rows(r, cos_t, sin_a, sin_b):
    return r * cos_t + pltpu.roll(r, 96, 1) * sin_a + pltpu.roll(r, 32, 1) * sin_b


def _proj_scale_kernel(x_ref, w_ref, s_ref, o_ref):
    acc = jnp.dot(x_ref[...], w_ref[...], preferred_element_type=F32)
    o_ref[...] = (acc * s_ref[...]).astype(o_ref.dtype)


def _proj_gate_kernel(x_ref, w_ref, b_ref, o_ref):
    acc = jnp.dot(x_ref[...], w_ref[...], preferred_element_type=F32) + b_ref[...]
    o_ref[...] = 1.0 / (1.0 + jnp.exp(-acc))


def _proj_mlaq_kernel(x_ref, wn_ref, wr_ref, cos_ref, sa_ref, sb_ref, o_ref, *, scale):
    tm = x_ref.shape[0]
    parts = 3 if tm % 48 == 0 else 1
    rows = tm // parts
    low = lax.broadcasted_iota(jnp.int32, (rows, LANES), 1) < MLA_ROPE_DIM
    for part in range(parts):
        rs = slice(part * rows, (part + 1) * rows)
        x = x_ref[rs, :]
        acc_n = jnp.dot(x, wn_ref[...], preferred_element_type=F32)
        acc_r = jnp.dot(x, wr_ref[...], preferred_element_type=F32)
        cos_t, sin_a, sin_b = cos_ref[rs, :], sa_ref[rs, :], sb_ref[rs, :]
        for hh in range(acc_n.shape[1] // MLA_NOPE_DIM):
            pair = acc_r[:, (hh // 2) * LANES:(hh // 2 + 1) * LANES]
            if hh % 2:
                pair = pltpu.roll(pair, MLA_ROPE_DIM, 1)
            rope = _rope_rows(jnp.where(low, pair, 0.0), cos_t, sin_a, sin_b) * scale
            c0 = hh * MLA_QK_PAD
            o_ref[rs, c0:c0 + LANES] = (acc_n[:, hh * LANES:(hh + 1) * LANES] * scale).astype(o_ref.dtype)
            o_ref[rs, c0 + LANES:c0 + 2 * LANES] = rope.astype(o_ref.dtype)


def _proj_mlaq(hn2d, wn, wr, tables, tm, tiles_per_batch, group, scale):
    m, k = hn2d.shape
    heads = wn.shape[1] // MLA_NOPE_DIM
    tab_spec = pl.BlockSpec((tm, LANES), lambda j, i: (i % tiles_per_batch, 0))
    return pl.pallas_call(
        functools.partial(_proj_mlaq_kernel, scale=scale),
        out_shape=jax.ShapeDtypeStruct((m, heads * MLA_QK_PAD), BF16),
        grid=(heads // group, m // tm),
        in_specs=[
            pl.BlockSpec((tm, k), lambda j, i: (i, 0)),
            pl.BlockSpec((k, group * MLA_NOPE_DIM), lambda j, i: (0, j)),
            pl.BlockSpec((k, group * MLA_ROPE_DIM), lambda j, i: (0, j)),
            tab_spec, tab_spec, tab_spec,
        ],
        out_specs=pl.BlockSpec((tm, group * MLA_QK_PAD), lambda j, i: (i, j)),
        compiler_params=_cparams(("parallel", "parallel")),
        name="proj_mlaq",
    )(hn2d, wn, wr, *tables)


def _store_lane_tiles(o_ref, val_t):
    for c in range(val_t.shape[1] // TILE):
        o_ref[c] = val_t[:, c * TILE:(c + 1) * TILE].astype(o_ref.dtype)


def _proj_t_kernel(x_ref, wt_ref, o_ref):
    acc_t = lax.dot_general(wt_ref[...], x_ref[...], (((1,), (1,)), ((), ())), preferred_element_type=F32)
    _store_lane_tiles(o_ref, acc_t)


def _proj_t(hn2d, wt, tm):
    m, k = hn2d.shape
    n = wt.shape[0]
    return pl.pallas_call(
        _proj_t_kernel,
        out_shape=jax.ShapeDtypeStruct((m // TILE, n, TILE), BF16),
        grid=(m // tm,),
        in_specs=[pl.BlockSpec((tm, k), lambda i: (i, 0)), pl.BlockSpec((n, k), lambda i: (0, 0))],
        out_specs=pl.BlockSpec((tm // TILE, n, TILE), lambda i: (i, 0, 0)),
        compiler_params=_cparams(("parallel",)),
        name="proj_t",
    )(hn2d, wt)


def _proj(kernel, hn2d, w, extras, extra_specs, out_dtype, tm, tn):
    m, k = hn2d.shape
    n = w.shape[1]
    return pl.pallas_call(
        kernel,
        out_shape=jax.ShapeDtypeStruct((m, n), out_dtype),
        grid=(n // tn, m // tm),
        in_specs=[
            pl.BlockSpec((tm, k), lambda j, i: (i, 0)),
            pl.BlockSpec((k, tn), lambda j, i: (0, j)),
        ] + extra_specs,
        out_specs=pl.BlockSpec((tm, tn), lambda j, i: (i, j)),
        compiler_params=_cparams(("parallel", "parallel")),
        name=getattr(kernel, "__name__", None) or kernel.func.__name__,
    )(hn2d, w, *extras)


def _kvup_kernel(c_ref, g_ref, wk_ref, wvt_ref, cos_ref, sa_ref, sb_ref, k_ref, vt_ref, *, rank):
    ckr = c_ref[...]
    cn = _rms(ckr[:, :rank], g_ref[...]).astype(BF16)
    kn = jnp.dot(cn, wk_ref[...], preferred_element_type=F32)
    vv_t = lax.dot_general(wvt_ref[...], cn, (((1,), (1,)), ((), ())), preferred_element_type=F32)
    rope = _rope_rows(ckr[:, rank:rank + LANES], cos_ref[...], sa_ref[...], sb_ref[...]).astype(BF16)
    for h in range(kn.shape[1] // MLA_NOPE_DIM):
        k_ref[:, h * MLA_QK_PAD:h * MLA_QK_PAD + LANES] = kn[:, h * LANES:(h + 1) * LANES].astype(BF16)
        k_ref[:, h * MLA_QK_PAD + LANES:(h + 1) * MLA_QK_PAD] = rope
    _store_lane_tiles(vt_ref, vv_t)


def _kvup(ckr, g, wk, wvt, tables, tm, tiles_per_batch):
    m, cw = ckr.shape
    rank = wk.shape[0]
    nk = wk.shape[1]
    nv = wvt.shape[0]
    heads = nk // MLA_NOPE_DIM
    tab_spec = pl.BlockSpec((tm, LANES), lambda i: (i % tiles_per_batch, 0))
    return pl.pallas_call(
        functools.partial(_kvup_kernel, rank=rank),
        out_shape=(jax.ShapeDtypeStruct((m, heads * MLA_QK_PAD), BF16),
                   jax.ShapeDtypeStruct((m // TILE, nv, TILE), BF16)),
        grid=(m // tm,),
        in_specs=[
            pl.BlockSpec((tm, cw), lambda i: (i, 0)),
            pl.BlockSpec((1, rank), lambda i: (0, 0)),
            pl.BlockSpec(wk.shape, lambda i: (0, 0)),
            pl.BlockSpec(wvt.shape, lambda i: (0, 0)),
            tab_spec, tab_spec, tab_spec,
        ],
        out_specs=(pl.BlockSpec((tm, heads * MLA_QK_PAD), lambda i: (i, 0)),
                   pl.BlockSpec((tm // TILE, nv, TILE), lambda i: (i, 0, 0))),
        compiler_params=_cparams(("parallel",)),
        name="kvup",
    )(ckr, g.reshape(1, rank), wk, wvt, *tables)


def _softplus2(z):
    neg_abs = pltpu.bitcast(pltpu.bitcast(z, jnp.uint32) | jnp.uint32(0x80000000), F32)
    return jnp.maximum(z, 0.0) + jnp.log2(1.0 + jnp.exp2(neg_abs))


def _sb_kernel(*refs, nsub, first_valid):
    q_refs = refs[:nsub]
    k_ref, vt_ref, o_ref, carry_ref, acc_ref = refs[nsub:]
    qs = pl.program_id(2)
    first_tile = 1 + nsub * qs
    carry_ref[...] = jnp.zeros_like(carry_ref)
    acc_ref[...] = jnp.zeros_like(acc_ref)
    trow = lax.broadcasted_iota(jnp.int32, (TILE, 2 * TILE), 0)
    tcol = lax.broadcasted_iota(jnp.int32, (TILE, 2 * TILE), 1) & (TILE - 1)
    tri2 = jnp.where(tcol >= trow, 1.0, 0.0).astype(BF16)

    def update(c0, c1, j, mask):
        lanes = slice(c0 * TILE, c1 * TILE)
        q = jnp.concatenate([q_refs[c][0] for c in range(c0, c1)], axis=0)
        kb = k_ref[0, pl.ds(pl.multiple_of(j * TILE, TILE), TILE), :]
        z = lax.dot_general(kb, q, (((1,), (1,)), ((), ())), preferred_element_type=F32)
        sp = _softplus2(z)
        if mask is not None:
            sp = jnp.where(mask, sp, 0.0)
        hi32 = pltpu.bitcast(pltpu.bitcast(sp, jnp.uint32) & jnp.uint32(0xFFFF0000), F32)
        parts = jnp.concatenate([hi32.astype(BF16), (sp - hi32).astype(BF16)], axis=0)
        cs = jnp.dot(tri2, parts, preferred_element_type=F32)
        carry = carry_ref[:, lanes]
        a = jnp.exp2(z - cs - carry)
        if mask is not None:
            a = jnp.where(mask, a, 0.0)
        acc_ref[:, lanes] = acc_ref[:, lanes] + jnp.dot(vt_ref[j], a.astype(BF16), preferred_element_type=F32)
        carry_ref[:, lanes] = carry + cs[0:1, :]

    def stick_left(c0, c1):
        return jnp.min(carry_ref[:, c0 * TILE:c1 * TILE]) < STICK_GONE_LOG2

    def diag_mask(nq):
        key = lax.broadcasted_iota(jnp.int32, (TILE, nq * TILE), 0)
        qry = lax.broadcasted_iota(jnp.int32, (TILE, nq * TILE), 1)
        return (qry >= TILE) | (key < qry)

    half = nsub // 2
    for c in reversed(range(nsub)):
        near = min(c + half, nsub)
        update(c, near, first_tile + c, diag_mask(near - c))
        if near < nsub:
            @pl.when(stick_left(near, nsub))
            def _():
                update(near, nsub, first_tile + c, None)

    for c0, c1 in ((0, half), (half, nsub)):
        def body(state):
            j, _ = state
            update(c0, c1, j, None)
            return j - 1, stick_left(c0, c1)

        _, alive = lax.while_loop(lambda st: (st[0] > 0) & st[1], body, (first_tile - 1, stick_left(c0, c1)))

        @pl.when(alive)
        def _():
            key0 = lax.broadcasted_iota(jnp.int32, (TILE, (c1 - c0) * TILE), 0)
            update(c0, c1, 0, key0 >= first_valid)

    o_ref[0] = acc_ref[...].T.astype(o_ref.dtype)


def _q_specs(nsub, width, col0):
    return [pl.BlockSpec((1, TILE, width),
                         functools.partial(lambda bi, h, i, c: (bi, nsub * i + 1 + c, col0 + h), c=c))
            for c in range(nsub)]


def _sb_attention(qk, vt, seq):
    b, lp, w2 = qk.shape
    heads = w2 // (2 * SB_HEAD_DIM)
    nsub = ATTN_Q_TILES
    rows = nsub * TILE
    return pl.pallas_call(
        functools.partial(_sb_kernel, nsub=nsub, first_valid=TILE - N_META),
        out_shape=jax.ShapeDtypeStruct((b, seq, heads * SB_HEAD_DIM), BF16),
        grid=(b, heads, seq // rows),
        in_specs=_q_specs(nsub, SB_HEAD_DIM, 0) + [
            pl.BlockSpec((1, lp, SB_HEAD_DIM), lambda bi, h, i: (bi, 0, heads + h)),
            pl.BlockSpec((lp // TILE, SB_HEAD_DIM, TILE), lambda bi, h, i: (bi, h, 0)),
        ],
        out_specs=pl.BlockSpec((1, rows, SB_HEAD_DIM), lambda bi, h, i: (bi, i, h)),
        scratch_shapes=[pltpu.VMEM((1, rows), F32), pltpu.VMEM((SB_HEAD_DIM, rows), F32)],
        compiler_params=_cparams(("parallel", "parallel", "arbitrary")),
        name="sb_attention",
    )(*([qk] * (nsub + 1)), vt)


def _mla_kernel(*refs, nsub, nheads, first_valid):
    nq = nsub * nheads
    q_refs = refs[:nq]
    k_ref, vt_ref, bias_ref, o_ref = refs[nq:nq + 4]
    scratch = refs[nq + 4:]
    qs = pl.program_id(2)
    rows = nsub * TILE
    blk_keys = 2 * TILE
    n_full = (nsub * qs) // 2

    lower = slice(0, rows // 2)
    upper = slice(rows // 2, rows)

    def with_ones(vtb):
        return jnp.concatenate([vtb, jnp.ones((DENOM_ROWS, vtb.shape[1]), BF16)], axis=0)

    class Head:
        def __init__(self, g):
            self.qcols = slice(g * MLA_QK_PAD, (g + 1) * MLA_QK_PAD)
            self.vrows = slice(g * MLA_V_DIM, (g + 1) * MLA_V_DIM)
            self.q_refs = q_refs[g * nsub:(g + 1) * nsub]
            (self.m_ref, self.acc_ref, s0, s1, p0, p1, x0, x1) = scratch[g * 8:(g + 1) * 8]
            self.even = (s0, x0, p0)
            self.odd = (s1, x1, p1)

        def queries(self, lanes=slice(None)):
            tiles = self.q_refs[lanes.start // TILE:] if lanes.start else self.q_refs
            return jnp.concatenate([r[0] for r in tiles], axis=0)

        def vt(self, j):
            return vt_ref[j, self.vrows, :]

        def scores(self, blk, s_ref, smax_ref, lanes=slice(None)):
            kb = k_ref[0, pl.ds(pl.multiple_of((1 + 2 * blk) * TILE, TILE), blk_keys), self.qcols]
            s = lax.dot_general(kb, self.queries(lanes), (((1,), (1,)), ((), ())), preferred_element_type=F32)
            s_ref[:, lanes] = s
            smax_ref[:, lanes] = jnp.max(s, axis=0, keepdims=True)

        def softmax(self, s_ref, smax_ref, p_ref, bias, lanes=slice(None)):
            if bias is None:
                s = s_ref[:, lanes]
                smax = smax_ref[:, lanes]
            else:
                s = s_ref[:, lanes] + bias
                smax = jnp.max(s, axis=0, keepdims=True)
            m_old = self.m_ref[:, lanes]
            m_new = jnp.maximum(m_old, smax)
            alpha = jnp.exp2(m_old - m_new)
            p = jnp.exp2(s - m_new)
            self.m_ref[:, lanes] = m_new
            p_ref[:, lanes] = p.astype(BF16)
            return alpha

        def values(self, blk, p_ref, lanes=slice(None)):
            j = 1 + 2 * blk
            vtb = with_ones(jnp.concatenate([self.vt(j), self.vt(j + 1)], axis=1))
            return jnp.dot(vtb, p_ref[:, lanes], preferred_element_type=F32)

        def stage(self, blk, cur, nxt, bias=None, ahead=slice(None)):
            (s_cur, x_cur, p_cur), (s_nxt, x_nxt, p_prev) = cur, nxt
            self.scores(blk + 1, s_nxt, x_nxt, ahead)
            alpha = self.softmax(s_cur, x_cur, p_cur, bias)
            self.acc_ref[...] = alpha * (self.acc_ref[...] + self.values(jnp.maximum(blk - 1, 0), p_prev))

        def last_stage(self, blk, cur, nxt, bias):
            (s_cur, x_cur, p_cur), (_, _, p_prev) = cur, nxt
            alpha = self.softmax(s_cur, x_cur, p_cur, bias[:, upper], upper)
            pv = self.values(blk - 1, p_prev)
            self.acc_ref[:, lower] = self.acc_ref[:, lower] + pv[:, lower]
            self.acc_ref[:, upper] = alpha * (self.acc_ref[:, upper] + pv[:, upper])

        def start(self):
            s = lax.dot_general(k_ref[0, first_valid:TILE, self.qcols], self.queries(),
                                (((1,), (1,)), ((), ())), preferred_element_type=F32)
            m0 = jnp.max(s, axis=0, keepdims=True)
            p = jnp.exp2(s - m0)
            self.m_ref[...] = m0
            p_tile = jnp.concatenate([jnp.zeros((first_valid, rows), BF16), p.astype(BF16)], axis=0)
            self.acc_ref[...] = jnp.dot(with_ones(self.vt(0)), p_tile, preferred_element_type=F32)
            self.scores(0, self.even[0], self.even[1])
            self.odd[2][...] = jnp.zeros_like(self.odd[2])

        def finish(self):
            acc = jnp.concatenate([self.acc_ref[:, lower],
                                   self.acc_ref[:, upper] + self.values(n_full + 1, self.odd[2], upper)], axis=1)
            out = (acc[:MLA_V_DIM] / acc[MLA_V_DIM:MLA_V_DIM + 1]).T
            o_ref[0, :, self.vrows] = out.astype(o_ref.dtype)

    heads = [Head(g) for g in range(nheads)]
    for hd in heads:
        hd.start()

    def body(u, carry):
        for hd in heads:
            hd.stage(2 * u, hd.even, hd.odd)
        for hd in heads:
            hd.stage(2 * u + 1, hd.odd, hd.even)
        return carry

    lax.fori_loop(0, n_full // 2, body, 0)
    for hd in heads:
        hd.stage(n_full, hd.even, hd.odd, bias=bias_ref[:blk_keys, :], ahead=upper)
    for hd in heads:
        hd.last_stage(n_full + 1, hd.odd, hd.even, bias_ref[blk_keys:, :])
    for hd in heads:
        hd.finish()


def _mla_attention(q, k, vt, seq):
    b, lp, _ = q.shape
    heads = vt.shape[1] // MLA_V_DIM
    nsub = ATTN_Q_TILES
    nheads = min(MLA_HEADS_PER_STEP, heads)
    assert nsub == 4, "the kernel visits the query-overlapping keys as exactly two 2-tile blocks"
    assert heads % nheads == 0
    rows = nsub * TILE
    shift = CHUNK.bit_length() - 1
    key = lax.broadcasted_iota(jnp.int32, (rows, rows), 0)
    qry = lax.broadcasted_iota(jnp.int32, (rows, rows), 1)
    bias = jnp.where((key >> shift) <= (qry >> shift), 0.0, NEG_BIG).astype(F32)
    q_specs = [pl.BlockSpec((1, TILE, MLA_QK_PAD),
                            functools.partial(lambda bi, h, i, g, c: (bi, nsub * i + 1 + c, nheads * h + g), g=g, c=c))
               for g in range(nheads) for c in range(nsub)]
    per_head_scratch = [pltpu.VMEM((1, rows), F32),
                        pltpu.VMEM((MLA_V_DIM + DENOM_ROWS, rows), F32),
                        pltpu.VMEM((2 * TILE, rows), F32), pltpu.VMEM((2 * TILE, rows), F32),
                        pltpu.VMEM((2 * TILE, rows), BF16), pltpu.VMEM((2 * TILE, rows), BF16),
                        pltpu.VMEM((1, rows), F32), pltpu.VMEM((1, rows), F32)]
    return pl.pallas_call(
        functools.partial(_mla_kernel, nsub=nsub, nheads=nheads, first_valid=TILE - N_META),
        out_shape=jax.ShapeDtypeStruct((b, seq, heads * MLA_V_DIM), BF16),
        grid=(b, heads // nheads, seq // rows),
        in_specs=q_specs + [
            pl.BlockSpec((1, lp, nheads * MLA_QK_PAD), lambda bi, h, i: (bi, 0, h)),
            pl.BlockSpec((lp // TILE, nheads * MLA_V_DIM, TILE), lambda bi, h, i: (bi, h, 0)),
            pl.BlockSpec((rows, rows), lambda bi, h, i: (0, 0), pipeline_mode=pl.Buffered(1)),
        ],
        out_specs=pl.BlockSpec((1, rows, nheads * MLA_V_DIM), lambda bi, h, i: (bi, i, h)),
        scratch_shapes=per_head_scratch * nheads,
        compiler_params=_cparams(("parallel", "parallel", "arbitrary")),
        name="mla_attention",
    )(*([q] * (nsub * nheads)), k, vt, bias)


def _merge_kernel(oa_ref, ob_ref, ga_ref, gb_ref, x_ref, wpa_ref, wpb_ref, wo_ref, o_ref):
    pa = jnp.dot(oa_ref[0], wpa_ref[...], preferred_element_type=F32)
    pb = jnp.dot(ob_ref[0], wpb_ref[...], preferred_element_type=F32)
    y = ga_ref[0] * pa + gb_ref[0] * pb
    o_ref[0] = x_ref[0] + jnp.dot(y.astype(BF16), wo_ref[...], preferred_element_type=F32)


def _merge(oa, ob, gates, x, wpa, wpb, wo):
    b, s, d = x.shape
    nt = s // TILE
    resident = lambda w: pl.BlockSpec(w.shape, lambda bi, i: (0, 0), pipeline_mode=pl.Buffered(1))
    return pl.pallas_call(
        _merge_kernel,
        out_shape=jax.ShapeDtypeStruct((b, s, d), F32),
        grid=(b, nt),
        in_specs=[
            pl.BlockSpec((1, TILE, oa.shape[2]), lambda bi, i: (bi, i, 0)),
            pl.BlockSpec((1, TILE, ob.shape[2]), lambda bi, i: (bi, i, 0)),
            pl.BlockSpec((1, TILE, d), lambda bi, i: (bi, i + 1, 0)),
            pl.BlockSpec((1, TILE, d), lambda bi, i: (bi, i + 1, 1)),
            pl.BlockSpec((1, TILE, d), lambda bi, i: (bi, i, 0)),
            resident(wpa), resident(wpb), resident(wo),
        ],
        out_specs=pl.BlockSpec((1, TILE, d), lambda bi, i: (bi, i, 0)),
        compiler_params=_cparams(("parallel", "parallel")),
        name="merge",
    )(oa, ob, gates, gates, x, wpa, wpb, wo)


def _split2(a):
    hi = a.astype(BF16)
    return hi, (a - hi.astype(F32)).astype(BF16)


def _route_kernel(h_ref, g_ref, wr_ref, br_ref, hp_ref, r_ref, cnt_ref, carry_ref, *, tm, half):
    i = pl.program_id(0)

    @pl.when(i == 0)
    def _():
        carry_ref[...] = jnp.zeros_like(carry_ref)

    hn = _rms(h_ref[...], g_ref[...])

    lo_bits = pltpu.bitcast(hn[:, :half].astype(BF16).astype(F32), jnp.uint32)
    hi_bits = pltpu.bitcast(hn[:, half:].astype(BF16).astype(F32), jnp.uint32)
    hp_ref[...] = (hi_bits & jnp.uint32(0xFFFF0000)) | (lo_bits >> 16)

    a_hi, a_mid = _split2(hn)
    w_hi, w_mid = wr_ref[0], wr_ref[1]
    dot = lambda a, w: jnp.dot(a, w, preferred_element_type=F32)
    lg = (dot(a_mid, w_hi) + dot(a_hi, w_mid) + dot(a_hi, w_hi)) + br_ref[...]

    lane = lax.broadcasted_iota(jnp.int32, lg.shape, 1)
    rmax = lambda v: jnp.max(v, axis=1, keepdims=True)
    rmin = lambda v: jnp.min(v, axis=1, keepdims=True)
    rsum = lambda v: jnp.sum(v, axis=1, keepdims=True)

    gmask = lane < N_GROUPS
    gl = jnp.where(gmask, lg, -jnp.inf)
    gmax = rmax(gl)
    gsel = rmin(jnp.where(gl == gmax, lane, LANES))
    p_g = 1.0 / rsum(jnp.where(gmask, jnp.exp(lg - gmax), 0.0))

    e_lo = EXPERT_LANE0 + gsel * EXPERTS_PER_GROUP
    emask = (lane >= e_lo) & (lane < e_lo + EXPERTS_PER_GROUP)
    emax = rmax(jnp.where(emask, lg, -jnp.inf))
    ex = jnp.where(emask, jnp.exp(lg - emax), 0.0)
    prob = jnp.where(emask, ex / rsum(ex), -1.0)
    top1 = rmax(prob)
    i1 = rmin(jnp.where(prob == top1, lane, LANES))
    prob2 = jnp.where(lane == i1, -1.0, prob)
    top2 = rmax(prob2)
    i2 = rmin(jnp.where(prob2 == top2, lane, LANES))
    denom = top1 + top2
    w1 = p_g * top1 / denom
    w2 = p_g * top2 / denom

    sel = ((lane == i1) | (lane == i2))
    row = lax.broadcasted_iota(jnp.int32, (tm, tm), 0)
    col = lax.broadcasted_iota(jnp.int32, (tm, tm), 1)
    before = (col < row).astype(BF16)
    prefix = dot(before, jnp.where(sel, 1.0, 0.0).astype(BF16)) + carry_ref[...]
    rank1 = rsum(jnp.where(lane == i1, prefix, 0.0))
    rank2 = rsum(jnp.where(lane == i2, prefix, 0.0))
    carry_ref[...] = carry_ref[...] + jnp.sum(jnp.where(sel, 1.0, 0.0), axis=0, keepdims=True)
    cnt_ref[...] = carry_ref[...]

    e1 = (i1 - EXPERT_LANE0).astype(F32)
    e2 = (i2 - EXPERT_LANE0).astype(F32)
    out = jnp.zeros(lg.shape, F32)
    for k, val in enumerate((e1, e2, w1, w2, rank1, rank2)):
        out = jnp.where(lane == k, val, out)
    r_ref[...] = out


def _route(h1, g, wr3, br):
    n, d = h1.shape
    tm = TILE
    return pl.pallas_call(
        functools.partial(_route_kernel, tm=tm, half=d // 2),
        out_shape=(jax.ShapeDtypeStruct((n, d // 2), jnp.uint32),
                   jax.ShapeDtypeStruct((n, LANES), F32),
                   jax.ShapeDtypeStruct((1, LANES), F32)),
        grid=(n // tm,),
        in_specs=[
            pl.BlockSpec((tm, d), lambda i: (i, 0)),
            pl.BlockSpec((1, d), lambda i: (0, 0)),
            pl.BlockSpec(wr3.shape, lambda i: (0, 0, 0)),
            pl.BlockSpec((1, LANES), lambda i: (0, 0)),
        ],
        out_specs=(pl.BlockSpec((tm, d // 2), lambda i: (i, 0)),
                   pl.BlockSpec((tm, LANES), lambda i: (i, 0)),
                   pl.BlockSpec((1, LANES), lambda i: (0, 0))),
        scratch_shapes=[pltpu.VMEM((1, LANES), F32)],
        compiler_params=_cparams(("arbitrary",)),
        name="route",
    )(h1, g.reshape(1, d), wr3, br)


def _dispatch_kernel(d0_ref, d1_ref, src_ref, init_ref, dst_ref, sem, *, tm):
    del init_ref
    base = pl.program_id(0) * tm

    def copy(r, dest_ref, s):
        return pltpu.make_async_copy(src_ref.at[pl.ds(r, 1)], dst_ref.at[pl.ds(dest_ref[base + r], 1)], sem.at[s])

    def start(r, c):
        copy(r, d0_ref, 0).start()
        copy(r, d1_ref, 1).start()
        return c

    lax.fori_loop(0, tm, start, 0, unroll=DMA_ISSUE_UNROLL)
    for s in range(2):
        pltpu.make_async_copy(src_ref, dst_ref.at[pl.ds(0, tm)], sem.at[s]).wait()


def _dispatch(dest0, dest1, hp, p_rows):
    n, w = hp.shape
    tm = TILE
    init = jnp.zeros((p_rows, w), hp.dtype)
    return pl.pallas_call(
        functools.partial(_dispatch_kernel, tm=tm),
        out_shape=jax.ShapeDtypeStruct((p_rows, w), hp.dtype),
        grid_spec=pltpu.PrefetchScalarGridSpec(
            num_scalar_prefetch=2,
            grid=(n // tm,),
            in_specs=[pl.BlockSpec((tm, w), lambda i, a, b: (i, 0)), pl.BlockSpec(memory_space=pl.ANY)],
            out_specs=pl.BlockSpec(memory_space=pl.ANY),
            scratch_shapes=[pltpu.SemaphoreType.DMA((2,))],
        ),
        input_output_aliases={3: 0},
        compiler_params=_cparams(("arbitrary",)),
        name="dispatch",
    )(dest0, dest1, hp, init)


def _expert_kernel(be_ref, nu_ref, first_ref, x_ref, w1_ref, w3_ref, w2_ref, y_ref, c1_ref, c3_ref, c2_ref):
    del be_ref
    j = pl.program_id(0)
    h = pl.program_id(1)
    live = j < nu_ref[0]

    @pl.when(live & (first_ref[j] == 1))
    def _():
        c1_ref[h] = w1_ref[0].astype(BF16)
        c3_ref[h] = w3_ref[0].astype(BF16)
        c2_ref[h] = w2_ref[0].astype(BF16)

    @pl.when(live)
    def _():
        xw = x_ref[...]
        lo = pltpu.bitcast(xw << 16, F32).astype(BF16)
        hi = pltpu.bitcast(xw & jnp.uint32(0xFFFF0000), F32).astype(BF16)
        xb = jnp.concatenate([lo, hi], axis=1)
        a = jnp.dot(xb, c1_ref[h], preferred_element_type=F32)
        g = jnp.dot(xb, c3_ref[h], preferred_element_type=F32)
        hid = (a * (1.0 / (1.0 + jnp.exp(-a))) * g).astype(BF16)
        part = jnp.dot(hid, c2_ref[h], preferred_element_type=F32)

        @pl.when(h == 0)
        def _():
            y_ref[...] = part

        @pl.when(h > 0)
        def _():
            y_ref[...] = y_ref[...] + part

    @pl.when(jnp.logical_not(live) & (h == 0))
    def _():
        y_ref[...] = jnp.zeros_like(y_ref)


def _experts(block_e, n_used, first, xs, w1, w3, w2):
    p_rows, half = xs.shape
    _, d, ff = w1.shape
    nb = p_rows // ROUTE_BLOCK
    nh = EXPERT_FF_SPLIT
    fh = ff // nh
    last = lambda j, nu: jnp.minimum(j, nu[0] - 1)

    def half_of(j, h, nu, fr):
        return jnp.where((fr[last(j, nu)] == 1) & (j < nu[0]), h, nh - 1)

    return pl.pallas_call(
        _expert_kernel,
        out_shape=jax.ShapeDtypeStruct((p_rows, d), F32),
        grid_spec=pltpu.PrefetchScalarGridSpec(
            num_scalar_prefetch=3,
            grid=(nb, nh),
            in_specs=[
                pl.BlockSpec((ROUTE_BLOCK, half), lambda j, h, be, nu, fr: (last(j, nu), 0)),
                pl.BlockSpec((1, d, fh), lambda j, h, be, nu, fr: (be[last(j, nu)], 0, half_of(j, h, nu, fr))),
                pl.BlockSpec((1, d, fh), lambda j, h, be, nu, fr: (be[last(j, nu)], 0, half_of(j, h, nu, fr))),
                pl.BlockSpec((1, fh, d), lambda j, h, be, nu, fr: (be[last(j, nu)], half_of(j, h, nu, fr), 0)),
            ],
            out_specs=pl.BlockSpec((ROUTE_BLOCK, d), lambda j, h, be, nu, fr: (j, 0)),
            scratch_shapes=[pltpu.VMEM((nh, d, fh), BF16), pltpu.VMEM((nh, d, fh), BF16),
                            pltpu.VMEM((nh, fh, d), BF16)],
        ),
        compiler_params=_cparams(("arbitrary", "arbitrary")),
        name="experts",
    )(block_e, n_used, first, xs, w1, w3, w2)


def _combine_kernel(d0_ref, d1_ref, h_ref, r_ref, g_ref, y_ref, o_ref, ya, yb, sem, *, tm):
    i = pl.program_id(0)
    last = pl.num_programs(0) - 1
    slot = i % 2
    ahead = jnp.minimum(i + 1, last)

    def issue(tile, sl, r):
        row = tile * tm + r
        pltpu.make_async_copy(y_ref.at[pl.ds(d0_ref[row], 1)], ya.at[sl, pl.ds(r, 1)], sem.at[sl, 0]).start()
        pltpu.make_async_copy(y_ref.at[pl.ds(d1_ref[row], 1)], yb.at[sl, pl.ds(r, 1)], sem.at[sl, 1]).start()

    def wait(sl):
        pltpu.make_async_copy(y_ref.at[pl.ds(0, tm)], ya.at[sl], sem.at[sl, 0]).wait()
        pltpu.make_async_copy(y_ref.at[pl.ds(0, tm)], yb.at[sl], sem.at[sl, 1]).wait()

    @pl.when(i == 0)
    def _():
        def first(r, c):
            issue(0, 0, r)
            return c
        lax.fori_loop(0, tm, first, 0, unroll=DMA_ISSUE_UNROLL)

    wait(slot)
    lane = lax.broadcasted_iota(jnp.int32, (COMBINE_CHUNK, LANES), 1)

    def chunk(c, carry):
        r0 = pl.multiple_of(c * COMBINE_CHUNK, COMBINE_CHUNK)
        for k in range(COMBINE_CHUNK):
            issue(ahead, 1 - slot, r0 + k)
        rows = pl.ds(r0, COMBINE_CHUNK)
        route = r_ref[rows, :]
        w1 = jnp.sum(jnp.where(lane == 2, route, 0.0), axis=1, keepdims=True)
        w2 = jnp.sum(jnp.where(lane == 3, route, 0.0), axis=1, keepdims=True)
        h2 = h_ref[rows, :] + (ya[slot, rows, :] * w1 + yb[slot, rows, :] * w2)
        o_ref[rows, :] = _rms(h2, g_ref[...])
        return carry

    lax.fori_loop(0, tm // COMBINE_CHUNK, chunk, 0)

    @pl.when(i == last)
    def _():
        wait(1 - slot)


def _combine(dest0, dest1, h1, route, g, y):
    n, d = h1.shape
    tm = TILE
    return pl.pallas_call(
        functools.partial(_combine_kernel, tm=tm),
        out_shape=jax.ShapeDtypeStruct((n, d), F32),
        grid_spec=pltpu.PrefetchScalarGridSpec(
            num_scalar_prefetch=2,
            grid=(n // tm,),
            in_specs=[
                pl.BlockSpec((tm, d), lambda i, a, b: (i, 0)),
                pl.BlockSpec((tm, LANES), lambda i, a, b: (i, 0)),
                pl.BlockSpec((1, d), lambda i, a, b: (0, 0)),
                pl.BlockSpec(memory_space=pl.ANY),
            ],
            out_specs=pl.BlockSpec((tm, d), lambda i, a, b: (i, 0)),
            scratch_shapes=[pltpu.VMEM((2, tm, d), F32), pltpu.VMEM((2, tm, d), F32),
                            pltpu.SemaphoreType.DMA((2, 2))],
        ),
        compiler_params=_cparams(("arbitrary",)),
        name="combine",
    )(dest0, dest1, h1, route, g.reshape(1, d), y)


def _rope_tables(lp):
    half = MLA_ROPE_DIM // 2
    inv = ROPE_THETA ** (-jnp.arange(half, dtype=F32) / half)
    pos = (jnp.arange(lp) - (TILE - N_META)).astype(F32)
    ang = pos[:, None] * inv[None, :]
    cos, sin = jnp.cos(ang), jnp.sin(ang)
    z32 = jnp.zeros((lp, half), F32)
    z64 = jnp.zeros((lp, LANES - MLA_ROPE_DIM), F32)
    return (jnp.concatenate([cos, cos, z64], axis=1),
            jnp.concatenate([-sin, z32, z64], axis=1),
            jnp.concatenate([z32, sin, z64], axis=1))


def kernel(x, meta_tokens, norm1_g, w_in, b_gate, kv_norm_g, w_uk, w_uv, w_proj_a, w_proj_b, w_out,
           norm2_g, w_route_group, b_route_group, w_route_expert, b_route_expert, w1, w3, w2, final_g):
    b, seq, d = x.shape
    assert seq % TILE == 0 and TILE % CHUNK == 0 and N_META <= TILE
    lp = TILE + seq
    n_tok = b * lp
    n_real = b * seq
    n_exp = N_GROUPS * EXPERTS_PER_GROUP
    sb_w = SB_HEADS * SB_HEAD_DIM
    qk_dim = MLA_NOPE_DIM + MLA_ROPE_DIM
    mq_w = MLA_HEADS * qk_dim
    sb_scale = SB_HEAD_DIM ** -0.5
    mla_scale = qk_dim ** -0.5

    wi = w_in[0]
    o_q = 3 * sb_w
    o_c = o_q + mq_w
    o_r = o_c + MLA_KV_RANK
    o_g = o_r + MLA_ROPE_DIM
    w_sb = wi[:, :2 * sb_w].astype(BF16)
    w_sbv_t = wi[:, 2 * sb_w:o_q].T.astype(BF16)
    w_mq = wi[:, o_q:o_c].reshape(d, MLA_HEADS, qk_dim)
    w_mq_nope = w_mq[:, :, :MLA_NOPE_DIM].reshape(d, MLA_HEADS * MLA_NOPE_DIM).astype(BF16)
    w_mq_rope = w_mq[:, :, MLA_NOPE_DIM:].reshape(d, MLA_HEADS * MLA_ROPE_DIM).astype(BF16)
    w_c = jnp.pad(wi[:, o_c:o_g], ((0, 0), (0, LANES - MLA_ROPE_DIM))).astype(BF16)
    w_g = wi[:, o_g:].astype(BF16)
    sb_colscale = jnp.concatenate([jnp.full((1, sb_w), sb_scale * LOG2E, F32), jnp.ones((1, sb_w), F32)], axis=1)
    wr = jnp.zeros((d, LANES), F32)
    wr = wr.at[:, :N_GROUPS].set(w_route_group[0]).at[:, EXPERT_LANE0:EXPERT_LANE0 + n_exp].set(w_route_expert[0])
    wr_hi = wr.astype(BF16)
    wr_mid = (wr - wr_hi.astype(F32)).astype(BF16)
    wr3 = jnp.stack([wr_hi, wr_mid])
    br = jnp.zeros((1, LANES), F32)
    br = br.at[0, :N_GROUPS].set(b_route_group[0]).at[0, EXPERT_LANE0:EXPERT_LANE0 + n_exp].set(b_route_expert[0])

    head = jnp.concatenate([jnp.zeros((TILE - N_META, d), F32), meta_tokens.astype(F32)], axis=0)
    hn = _norm1(x, head, norm1_g[0]).reshape(n_tok, d)
    tm = _row_tile(lp)
    tpb = lp // tm
    tables = _rope_tables(lp)
    row_spec = lambda tn: pl.BlockSpec((1, tn), lambda j, i: (0, j))

    sb_qk = _proj(_proj_scale_kernel, hn, w_sb, [sb_colscale], [row_spec(sb_w)], BF16, tm, sb_w)
    sb_vt = _proj_t(hn, w_sbv_t, tm)
    q_mla = _proj_mlaq(hn, w_mq_nope, w_mq_rope, tables, tm, tpb, min(4, MLA_HEADS), mla_scale * LOG2E)
    ckr = _proj(_proj_scale_kernel, hn, w_c, [jnp.ones((1, w_c.shape[1]), F32)],
                [row_spec(w_c.shape[1])], F32, tm, w_c.shape[1])
    gates = _proj(_proj_gate_kernel, hn, w_g, [b_gate[0].reshape(1, 2 * d)], [row_spec(d)], F32, tm, d)
    k_mla, vt_mla = _kvup(ckr, kv_norm_g[0], w_uk[0].astype(BF16), w_uv[0].T.astype(BF16), tables, tm, tpb)

    o_a = _sb_attention(sb_qk.reshape(b, lp, 2 * sb_w), sb_vt, seq)
    o_b = _mla_attention(q_mla.reshape(b, lp, -1), k_mla.reshape(b, lp, -1), vt_mla, seq)
    h1 = _merge(o_a, o_b, gates.reshape(b, lp, 2 * d), x,
                w_proj_a[0].astype(BF16), w_proj_b[0].astype(BF16), w_out[0].astype(BF16)).reshape(n_real, d)

    hp, route, counts = _route(h1, norm2_g[0], wr3, br)
    cnt = counts[0, EXPERT_LANE0:EXPERT_LANE0 + n_exp].astype(jnp.int32)
    padded = (cnt + ROUTE_BLOCK - 1) // ROUTE_BLOCK * ROUTE_BLOCK
    pends = jnp.cumsum(padded)
    pstarts = pends - padded
    n_blocks = (n_real * TOP_K + n_exp * (ROUTE_BLOCK - 1) + ROUTE_BLOCK - 1) // ROUTE_BLOCK
    n_used = (pends[-1] // ROUTE_BLOCK).astype(jnp.int32).reshape(1)
    blk_start = jnp.minimum(jnp.arange(n_blocks), n_used[0] - 1) * ROUTE_BLOCK
    block_e = jnp.minimum(jnp.sum(pends[None, :] <= blk_start[:, None], axis=1), n_exp - 1).astype(jnp.int32)
    ids = route[:, :6].astype(jnp.int32)
    expert_iota = jnp.arange(n_exp, dtype=jnp.int32)[None, :]
    start_of = lambda e: jnp.sum(jnp.where(e[:, None] == expert_iota, pstarts[None, :], 0), axis=1)
    dest0 = (start_of(ids[:, 0]) + ids[:, 4]).astype(jnp.int32)
    dest1 = (start_of(ids[:, 1]) + ids[:, 5]).astype(jnp.int32)

    xs = _dispatch(dest0, dest1, hp, n_blocks * ROUTE_BLOCK)
    first = jnp.concatenate([jnp.ones((1,), jnp.int32), (block_e[1:] != block_e[:-1]).astype(jnp.int32)])
    y = _experts(block_e, n_used, first, xs, w1[0], w3[0], w2[0])
    out = _combine(dest0, dest1, h1, route, final_g, y)
    return out.reshape(b, seq, d)
```

```python
import functools

import jax
import jax.numpy as jnp
from jax import lax
from jax.experimental import pallas as pl
from jax.experimental.pallas import tpu as pltpu

N_META = 16
CHUNK = 64
SB_HEADS = 8
SB_HEAD_DIM = 128
MLA_HEADS = 16
MLA_NOPE_DIM = 128
MLA_ROPE_DIM = 64
MLA_V_DIM = 128
MLA_KV_RANK = 512
ROPE_THETA = 10000.0
N_GROUPS = 4
EXPERTS_PER_GROUP = 8
TOP_K = 2
ROUTE_BLOCK = 256
RMS_EPS = 1e-6

TILE = 256
LANES = 128
MLA_QK_PAD = 256
ATTN_Q_TILES = 4
EXPERT_LANE0 = 8
VMEM_LIMIT = 56 * 1024 * 1024
EXPERT_VMEM_LIMIT = 62 * 1024 * 1024
NEG_BIG = -1e30
DMA_ISSUE_UNROLL = 8
COMBINE_CHUNK = 64
DENOM_ROWS = 16
EXPERT_FF_SPLIT = 2
MLA_HEADS_PER_STEP = 2
STICK_GONE_LOG2 = 152.0

F32 = jnp.float32
BF16 = jnp.bfloat16
LOG2E = 1.4426950408889634


def _cparams(sem):
    return pltpu.CompilerParams(dimension_semantics=sem, vmem_limit_bytes=VMEM_LIMIT)


def _rms(v, g):
    ms = jnp.mean(v * v, axis=-1, keepdims=True)
    return v * lax.rsqrt(ms + RMS_EPS) * g


def _row_tile(lp):
    for t in (768, 512, 256):
        if lp % t == 0:
            return t
    raise ValueError(lp)


def _norm1_kernel(x_ref, head_ref, g_ref, o_ref):
    i = pl.program_id(1)

    @pl.when(i == 0)
    def _():
        o_ref[0] = _rms(head_ref[...], g_ref[...]).astype(BF16)

    @pl.when(i > 0)
    def _():
        o_ref[0] = _rms(x_ref[0], g_ref[...]).astype(BF16)


def _norm1(x, head, g):
    b, s, d = x.shape
    nt = s // TILE + 1
    return pl.pallas_call(
        _norm1_kernel,
        out_shape=jax.ShapeDtypeStruct((b, nt * TILE, d), BF16),
        grid=(b, nt),
        in_specs=[
            pl.BlockSpec((1, TILE, d), lambda bi, i: (bi, jnp.maximum(i - 1, 0), 0)),
            pl.BlockSpec((TILE, d), lambda bi, i: (0, 0)),
            pl.BlockSpec((1, d), lambda bi, i: (0, 0)),
        ],
        out_specs=pl.BlockSpec((1, TILE, d), lambda bi, i: (bi, i, 0)),
        compiler_params=_cparams(("parallel", "parallel")),
        name="norm1",
    )(x, head, g.reshape(1, d))


def _rope_rows(r, cos_t, sin_a, sin_b):
    return r * cos_t + pltpu.roll(r, 96, 1) * sin_a + pltpu.roll(r, 32, 1) * sin_b


def _proj_scale_kernel(x_ref, w_ref, s_ref, o_ref):
    acc = jnp.dot(x_ref[...], w_ref[...], preferred_element_type=F32)
    o_ref[...] = (acc * s_ref[...]).astype(o_ref.dtype)


def _proj_gate_kernel(x_ref, w_ref, b_ref, o_ref):
    acc = jnp.dot(x_ref[...], w_ref[...], preferred_element_type=F32) + b_ref[...]
    o_ref[...] = 1.0 / (1.0 + jnp.exp(-acc))


def _proj_mlaq_kernel(x_ref, wn_ref, wr_ref, cos_ref, sa_ref, sb_ref, o_ref, *, scale):
    tm = x_ref.shape[0]
    parts = 3 if tm % 48 == 0 else 1
    rows = tm // parts
    low = lax.broadcasted_iota(jnp.int32, (rows, LANES), 1) < MLA_ROPE_DIM
    for part in range(parts):
        rs = slice(part * rows, (part + 1) * rows)
        x = x_ref[rs, :]
        acc_n = jnp.dot(x, wn_ref[...], preferred_element_type=F32)
        acc_r = jnp.dot(x, wr_ref[...], preferred_element_type=F32)
        cos_t, sin_a, sin_b = cos_ref[rs, :], sa_ref[rs, :], sb_ref[rs, :]
        for hh in range(acc_n.shape[1] // MLA_NOPE_DIM):
            pair = acc_r[:, (hh // 2) * LANES:(hh // 2 + 1) * LANES]
            if hh % 2:
                pair = pltpu.roll(pair, MLA_ROPE_DIM, 1)
            rope = _rope_rows(jnp.where(low, pair, 0.0), cos_t, sin_a, sin_b) * scale
            c0 = hh * MLA_QK_PAD
            o_ref[rs, c0:c0 + LANES] = (acc_n[:, hh * LANES:(hh + 1) * LANES] * scale).astype(o_ref.dtype)
            o_ref[rs, c0 + LANES:c0 + 2 * LANES] = rope.astype(o_ref.dtype)


def _proj_mlaq(hn2d, wn, wr, tables, tm, tiles_per_batch, group, scale):
    m, k = hn2d.shape
    heads = wn.shape[1] // MLA_NOPE_DIM
    tab_spec = pl.BlockSpec((tm, LANES), lambda j, i: (i % tiles_per_batch, 0))
    return pl.pallas_call(
        functools.partial(_proj_mlaq_kernel, scale=scale),
        out_shape=jax.ShapeDtypeStruct((m, heads * MLA_QK_PAD), BF16),
        grid=(heads // group, m // tm),
        in_specs=[
            pl.BlockSpec((tm, k), lambda j, i: (i, 0)),
            pl.BlockSpec((k, group * MLA_NOPE_DIM), lambda j, i: (0, j)),
            pl.BlockSpec((k, group * MLA_ROPE_DIM), lambda j, i: (0, j)),
            tab_spec, tab_spec, tab_spec,
        ],
        out_specs=pl.BlockSpec((tm, group * MLA_QK_PAD), lambda j, i: (i, j)),
        compiler_params=_cparams(("parallel", "parallel")),
        name="proj_mlaq",
    )(hn2d, wn, wr, *tables)


def _store_lane_tiles(o_ref, val_t):
    for c in range(val_t.shape[1] // TILE):
        o_ref[c] = val_t[:, c * TILE:(c + 1) * TILE].astype(o_ref.dtype)


def _proj_t_kernel(x_ref, wt_ref, o_ref):
    acc_t = lax.dot_general(wt_ref[...], x_ref[...], (((1,), (1,)), ((), ())), preferred_element_type=F32)
    _store_lane_tiles(o_ref, acc_t)


def _proj_t(hn2d, wt, tm):
    m, k = hn2d.shape
    n = wt.shape[0]
    return pl.pallas_call(
        _proj_t_kernel,
        out_shape=jax.ShapeDtypeStruct((m // TILE, n, TILE), BF16),
        grid=(m // tm,),
        in_specs=[pl.BlockSpec((tm, k), lambda i: (i, 0)), pl.BlockSpec((n, k), lambda i: (0, 0))],
        out_specs=pl.BlockSpec((tm // TILE, n, TILE), lambda i: (i, 0, 0)),
        compiler_params=_cparams(("parallel",)),
        name="proj_t",
    )(hn2d, wt)


def _proj(kernel, hn2d, w, extras, extra_specs, out_dtype, tm, tn):
    m, k = hn2d.shape
    n = w.shape[1]
    return pl.pallas_call(
        kernel,
        out_shape=jax.ShapeDtypeStruct((m, n), out_dtype),
        grid=(n // tn, m // tm),
        in_specs=[
            pl.BlockSpec((tm, k), lambda j, i: (i, 0)),
            pl.BlockSpec((k, tn), lambda j, i: (0, j)),
        ] + extra_specs,
        out_specs=pl.BlockSpec((tm, tn), lambda j, i: (i, j)),
        compiler_params=_cparams(("parallel", "parallel")),
        name=getattr(kernel, "__name__", None) or kernel.func.__name__,
    )(hn2d, w, *extras)


def _kvup_kernel(c_ref, g_ref, wk_ref, wvt_ref, cos_ref, sa_ref, sb_ref, k_ref, vt_ref, *, rank):
    ckr = c_ref[...]
    cn = _rms(ckr[:, :rank], g_ref[...]).astype(BF16)
    kn = jnp.dot(cn, wk_ref[...], preferred_element_type=F32)
    vv_t = lax.dot_general(wvt_ref[...], cn, (((1,), (1,)), ((), ())), preferred_element_type=F32)
    rope = _rope_rows(ckr[:, rank:rank + LANES], cos_ref[...], sa_ref[...], sb_ref[...]).astype(BF16)
    for h in range(kn.shape[1] // MLA_NOPE_DIM):
        k_ref[:, h * MLA_QK_PAD:h * MLA_QK_PAD + LANES] = kn[:, h * LANES:(h + 1) * LANES].astype(BF16)
        k_ref[:, h * MLA_QK_PAD + LANES:(h + 1) * MLA_QK_PAD] = rope
    _store_lane_tiles(vt_ref, vv_t)


def _kvup(ckr, g, wk, wvt, tables, tm, tiles_per_batch):
    m, cw = ckr.shape
    rank = wk.shape[0]
    nk = wk.shape[1]
    nv = wvt.shape[0]
    heads = nk // MLA_NOPE_DIM
    tab_spec = pl.BlockSpec((tm, LANES), lambda i: (i % tiles_per_batch, 0))
    return pl.pallas_call(
        functools.partial(_kvup_kernel, rank=rank),
        out_shape=(jax.ShapeDtypeStruct((m, heads * MLA_QK_PAD), BF16),
                   jax.ShapeDtypeStruct((m // TILE, nv, TILE), BF16)),
        grid=(m // tm,),
        in_specs=[
            pl.BlockSpec((tm, cw), lambda i: (i, 0)),
            pl.BlockSpec((1, rank), lambda i: (0, 0)),
            pl.BlockSpec(wk.shape, lambda i: (0, 0)),
            pl.BlockSpec(wvt.shape, lambda i: (0, 0)),
            tab_spec, tab_spec, tab_spec,
        ],
        out_specs=(pl.BlockSpec((tm, heads * MLA_QK_PAD), lambda i: (i, 0)),
                   pl.BlockSpec((tm // TILE, nv, TILE), lambda i: (i, 0, 0))),
        compiler_params=_cparams(("parallel",)),
        name="kvup",
    )(ckr, g.reshape(1, rank), wk, wvt, *tables)


def _softplus2(z):
    neg_abs = pltpu.bitcast(pltpu.bitcast(z, jnp.uint32) | jnp.uint32(0x80000000), F32)
    return jnp.maximum(z, 0.0) + jnp.log2(1.0 + jnp.exp2(neg_abs))


def _sb_kernel(*refs, nsub, first_valid):
    q_refs = refs[:nsub]
    k_ref, vt_ref, o_ref, carry_ref, acc_ref = refs[nsub:]
    qs = pl.program_id(2)
    first_tile = 1 + nsub * qs
    carry_ref[...] = jnp.zeros_like(carry_ref)
    acc_ref[...] = jnp.zeros_like(acc_ref)
    trow = lax.broadcasted_iota(jnp.int32, (TILE, 2 * TILE), 0)
    tcol = lax.broadcasted_iota(jnp.int32, (TILE, 2 * TILE), 1) & (TILE - 1)
    tri2 = jnp.where(tcol >= trow, 1.0, 0.0).astype(BF16)

    def update(c0, c1, j, mask):
        lanes = slice(c0 * TILE, c1 * TILE)
        q = jnp.concatenate([q_refs[c][0] for c in range(c0, c1)], axis=0)
        kb = k_ref[0, pl.ds(pl.multiple_of(j * TILE, TILE), TILE), :]
        z = lax.dot_general(kb, q, (((1,), (1,)), ((), ())), preferred_element_type=F32)
        sp = _softplus2(z)
        if mask is not None:
            sp = jnp.where(mask, sp, 0.0)
        hi32 = pltpu.bitcast(pltpu.bitcast(sp, jnp.uint32) & jnp.uint32(0xFFFF0000), F32)
        parts = jnp.concatenate([hi32.astype(BF16), (sp - hi32).astype(BF16)], axis=0)
        cs = jnp.dot(tri2, parts, preferred_element_type=F32)
        carry = carry_ref[:, lanes]
        a = jnp.exp2(z - cs - carry)
        if mask is not None:
            a = jnp.where(mask, a, 0.0)
        acc_ref[:, lanes] = acc_ref[:, lanes] + jnp.dot(vt_ref[j], a.astype(BF16), preferred_element_type=F32)
        carry_ref[:, lanes] = carry + cs[0:1, :]

    def stick_left(c0, c1):
        return jnp.min(carry_ref[:, c0 * TILE:c1 * TILE]) < STICK_GONE_LOG2

    def diag_mask(nq):
        key = lax.broadcasted_iota(jnp.int32, (TILE, nq * TILE), 0)
        qry = lax.broadcasted_iota(jnp.int32, (TILE, nq * TILE), 1)
        return (qry >= TILE) | (key < qry)

    half = nsub // 2
    for c in reversed(range(nsub)):
        near = min(c + half, nsub)
        update(c, near, first_tile + c, diag_mask(near - c))
        if near < nsub:
            @pl.when(stick_left(near, nsub))
            def _():
                update(near, nsub, first_tile + c, None)

    for c0, c1 in ((0, half), (half, nsub)):
        def body(state):
            j, _ = state
            update(c0, c1, j, None)
            return j - 1, stick_left(c0, c1)

        _, alive = lax.while_loop(lambda st: (st[0] > 0) & st[1], body, (first_tile - 1, stick_left(c0, c1)))

        @pl.when(alive)
        def _():
            key0 = lax.broadcasted_iota(jnp.int32, (TILE, (c1 - c0) * TILE), 0)
            update(c0, c1, 0, key0 >= first_valid)

    o_ref[0] = acc_ref[...].T.astype(o_ref.dtype)


def _q_specs(nsub, width, col0):
    return [pl.BlockSpec((1, TILE, width),
                         functools.partial(lambda bi, h, i, c: (bi, nsub * i + 1 + c, col0 + h), c=c))
            for c in range(nsub)]


def _sb_attention(qk, vt, seq):
    b, lp, w2 = qk.shape
    heads = w2 // (2 * SB_HEAD_DIM)
    nsub = ATTN_Q_TILES
    rows = nsub * TILE
    return pl.pallas_call(
        functools.partial(_sb_kernel, nsub=nsub, first_valid=TILE - N_META),
        out_shape=jax.ShapeDtypeStruct((b, seq, heads * SB_HEAD_DIM), BF16),
        grid=(b, heads, seq // rows),
        in_specs=_q_specs(nsub, SB_HEAD_DIM, 0) + [
            pl.BlockSpec((1, lp, SB_HEAD_DIM), lambda bi, h, i: (bi, 0, heads + h)),
            pl.BlockSpec((lp // TILE, SB_HEAD_DIM, TILE), lambda bi, h, i: (bi, h, 0)),
        ],
        out_specs=pl.BlockSpec((1, rows, SB_HEAD_DIM), lambda bi, h, i: (bi, i, h)),
        scratch_shapes=[pltpu.VMEM((1, rows), F32), pltpu.VMEM((SB_HEAD_DIM, rows), F32)],
        compiler_params=_cparams(("parallel", "parallel", "arbitrary")),
        name="sb_attention",
    )(*([qk] * (nsub + 1)), vt)


def _mla_kernel(*refs, nsub, nheads, first_valid):
    nq = nsub * nheads
    q_refs = refs[:nq]
    k_ref, vt_ref, bias_ref, o_ref = refs[nq:nq + 4]
    scratch = refs[nq + 4:]
    qs = pl.program_id(2)
    rows = nsub * TILE
    blk_keys = 2 * TILE
    n_full = (nsub * qs) // 2

    lower = slice(0, rows // 2)
    upper = slice(rows // 2, rows)

    def with_ones(vtb):
        return jnp.concatenate([vtb, jnp.ones((DENOM_ROWS, vtb.shape[1]), BF16)], axis=0)

    class Head:
        def __init__(self, g):
            self.qcols = slice(g * MLA_QK_PAD, (g + 1) * MLA_QK_PAD)
            self.vrows = slice(g * MLA_V_DIM, (g + 1) * MLA_V_DIM)
            self.q_refs = q_refs[g * nsub:(g + 1) * nsub]
            (self.m_ref, self.acc_ref, s0, s1, p0, p1, x0, x1) = scratch[g * 8:(g + 1) * 8]
            self.even = (s0, x0, p0)
            self.odd = (s1, x1, p1)

        def queries(self, lanes=slice(None)):
            tiles = self.q_refs[lanes.start // TILE:] if lanes.start else self.q_refs
            return jnp.concatenate([r[0] for r in tiles], axis=0)

        def vt(self, j):
            return vt_ref[j, self.vrows, :]

        def scores(self, blk, s_ref, smax_ref, lanes=slice(None)):
            kb = k_ref[0, pl.ds(pl.multiple_of((1 + 2 * blk) * TILE, TILE), blk_keys), self.qcols]
            s = lax.dot_general(kb, self.queries(lanes), (((1,), (1,)), ((), ())), preferred_element_type=F32)
            s_ref[:, lanes] = s
            smax_ref[:, lanes] = jnp.max(s, axis=0, keepdims=True)

        def softmax(self, s_ref, smax_ref, p_ref, bias, lanes=slice(None)):
            if bias is None:
                s = s_ref[:, lanes]
                smax = smax_ref[:, lanes]
            else:
                s = s_ref[:, lanes] + bias
                smax = jnp.max(s, axis=0, keepdims=True)
            m_old = self.m_ref[:, lanes]
            m_new = jnp.maximum(m_old, smax)
            alpha = jnp.exp2(m_old - m_new)
            p = jnp.exp2(s - m_new)
            self.m_ref[:, lanes] = m_new
            p_ref[:, lanes] = p.astype(BF16)
            return alpha

        def values(self, blk, p_ref, lanes=slice(None)):
            j = 1 + 2 * blk
            vtb = with_ones(jnp.concatenate([self.vt(j), self.vt(j + 1)], axis=1))
            return jnp.dot(vtb, p_ref[:, lanes], preferred_element_type=F32)

        def stage(self, blk, cur, nxt, bias=None, ahead=slice(None)):
            (s_cur, x_cur, p_cur), (s_nxt, x_nxt, p_prev) = cur, nxt
            self.scores(blk + 1, s_nxt, x_nxt, ahead)
            alpha = self.softmax(s_cur, x_cur, p_cur, bias)
            self.acc_ref[...] = alpha * (self.acc_ref[...] + self.values(jnp.maximum(blk - 1, 0), p_prev))

        def last_stage(self, blk, cur, nxt, bias):
            (s_cur, x_cur, p_cur), (_, _, p_prev) = cur, nxt
            alpha = self.softmax(s_cur, x_cur, p_cur, bias[:, upper], upper)
            pv = self.values(blk - 1, p_prev)
            self.acc_ref[:, lower] = self.acc_ref[:, lower] + pv[:, lower]
            self.acc_ref[:, upper] = alpha * (self.acc_ref[:, upper] + pv[:, upper])

        def start(self):
            s = lax.dot_general(k_ref[0, first_valid:TILE, self.qcols], self.queries(),
                                (((1,), (1,)), ((), ())), preferred_element_type=F32)
            m0 = jnp.max(s, axis=0, keepdims=True)
            p = jnp.exp2(s - m0)
            self.m_ref[...] = m0
            p_tile = jnp.concatenate([jnp.zeros((first_valid, rows), BF16), p.astype(BF16)], axis=0)
            self.acc_ref[...] = jnp.dot(with_ones(self.vt(0)), p_tile, preferred_element_type=F32)
            self.scores(0, self.even[0], self.even[1])
            self.odd[2][...] = jnp.zeros_like(self.odd[2])

        def finish(self):
            acc = jnp.concatenate([self.acc_ref[:, lower],
                                   self.acc_ref[:, upper] + self.values(n_full + 1, self.odd[2], upper)], axis=1)
            out = (acc[:MLA_V_DIM] / acc[MLA_V_DIM:MLA_V_DIM + 1]).T
            o_ref[0, :, self.vrows] = out.astype(o_ref.dtype)

    heads = [Head(g) for g in range(nheads)]
    for hd in heads:
        hd.start()

    def body(u, carry):
        for hd in heads:
            hd.stage(2 * u, hd.even, hd.odd)
        for hd in heads:
            hd.stage(2 * u + 1, hd.odd, hd.even)
        return carry

    lax.fori_loop(0, n_full // 2, body, 0)
    for hd in heads:
        hd.stage(n_full, hd.even, hd.odd, bias=bias_ref[:blk_keys, :], ahead=upper)
    for hd in heads:
        hd.last_stage(n_full + 1, hd.odd, hd.even, bias_ref[blk_keys:, :])
    for hd in heads:
        hd.finish()


def _mla_attention(q, k, vt, seq):
    b, lp, _ = q.shape
    heads = vt.shape[1] // MLA_V_DIM
    nsub = ATTN_Q_TILES
    nheads = min(MLA_HEADS_PER_STEP, heads)
    assert nsub == 4, "the kernel visits the query-overlapping keys as exactly two 2-tile blocks"
    assert heads % nheads == 0
    rows = nsub * TILE
    shift = CHUNK.bit_length() - 1
    key = lax.broadcasted_iota(jnp.int32, (rows, rows), 0)
    qry = lax.broadcasted_iota(jnp.int32, (rows, rows), 1)
    bias = jnp.where((key >> shift) <= (qry >> shift), 0.0, NEG_BIG).astype(F32)
    q_specs = [pl.BlockSpec((1, TILE, MLA_QK_PAD),
                            functools.partial(lambda bi, h, i, g, c: (bi, nsub * i + 1 + c, nheads * h + g), g=g, c=c))
               for g in range(nheads) for c in range(nsub)]
    per_head_scratch = [pltpu.VMEM((1, rows), F32),
                        pltpu.VMEM((MLA_V_DIM + DENOM_ROWS, rows), F32),
                        pltpu.VMEM((2 * TILE, rows), F32), pltpu.VMEM((2 * TILE, rows), F32),
                        pltpu.VMEM((2 * TILE, rows), BF16), pltpu.VMEM((2 * TILE, rows), BF16),
                        pltpu.VMEM((1, rows), F32), pltpu.VMEM((1, rows), F32)]
    return pl.pallas_call(
        functools.partial(_mla_kernel, nsub=nsub, nheads=nheads, first_valid=TILE - N_META),
        out_shape=jax.ShapeDtypeStruct((b, seq, heads * MLA_V_DIM), BF16),
        grid=(b, heads // nheads, seq // rows),
        in_specs=q_specs + [
            pl.BlockSpec((1, lp, nheads * MLA_QK_PAD), lambda bi, h, i: (bi, 0, h)),
            pl.BlockSpec((lp // TILE, nheads * MLA_V_DIM, TILE), lambda bi, h, i: (bi, h, 0)),
            pl.BlockSpec((rows, rows), lambda bi, h, i: (0, 0), pipeline_mode=pl.Buffered(1)),
        ],
        out_specs=pl.BlockSpec((1, rows, nheads * MLA_V_DIM), lambda bi, h, i: (bi, i, h)),
        scratch_shapes=per_head_scratch * nheads,
        compiler_params=_cparams(("parallel", "parallel", "arbitrary")),
        name="mla_attention",
    )(*([q] * (nsub * nheads)), k, vt, bias)


def _merge_kernel(oa_ref, ob_ref, ga_ref, gb_ref, x_ref, wpa_ref, wpb_ref, wo_ref, o_ref):
    pa = jnp.dot(oa_ref[0], wpa_ref[...], preferred_element_type=F32)
    pb = jnp.dot(ob_ref[0], wpb_ref[...], preferred_element_type=F32)
    y = ga_ref[0] * pa + gb_ref[0] * pb
    o_ref[0] = x_ref[0] + jnp.dot(y.astype(BF16), wo_ref[...], preferred_element_type=F32)


def _merge(oa, ob, gates, x, wpa, wpb, wo):
    b, s, d = x.shape
    nt = s // TILE
    resident = lambda w: pl.BlockSpec(w.shape, lambda bi, i: (0, 0), pipeline_mode=pl.Buffered(1))
    return pl.pallas_call(
        _merge_kernel,
        out_shape=jax.ShapeDtypeStruct((b, s, d), F32),
        grid=(b, nt),
        in_specs=[
            pl.BlockSpec((1, TILE, oa.shape[2]), lambda bi, i: (bi, i, 0)),
            pl.BlockSpec((1, TILE, ob.shape[2]), lambda bi, i: (bi, i, 0)),
            pl.BlockSpec((1, TILE, d), lambda bi, i: (bi, i + 1, 0)),
            pl.BlockSpec((1, TILE, d), lambda bi, i: (bi, i + 1, 1)),
            pl.BlockSpec((1, TILE, d), lambda bi, i: (bi, i, 0)),
            resident(wpa), resident(wpb), resident(wo),
        ],
        out_specs=pl.BlockSpec((1, TILE, d), lambda bi, i: (bi, i, 0)),
        compiler_params=_cparams(("parallel", "parallel")),
        name="merge",
    )(oa, ob, gates, gates, x, wpa, wpb, wo)


def _split2(a):
    hi = a.astype(BF16)
    return hi, (a - hi.astype(F32)).astype(BF16)


def _route_kernel(h_ref, g_ref, wr_ref, br_ref, hp_ref, r_ref, cnt_ref, carry_ref, *, tm, half):
    i = pl.program_id(0)

    @pl.when(i == 0)
    def _():
        carry_ref[...] = jnp.zeros_like(carry_ref)

    hn = _rms(h_ref[...], g_ref[...])

    lo_bits = pltpu.bitcast(hn[:, :half].astype(BF16).astype(F32), jnp.uint32)
    hi_bits = pltpu.bitcast(hn[:, half:].astype(BF16).astype(F32), jnp.uint32)
    hp_ref[...] = (hi_bits & jnp.uint32(0xFFFF0000)) | (lo_bits >> 16)

    a_hi, a_mid = _split2(hn)
    w_hi, w_mid = wr_ref[0], wr_ref[1]
    dot = lambda a, w: jnp.dot(a, w, preferred_element_type=F32)
    lg = (dot(a_mid, w_hi) + dot(a_hi, w_mid) + dot(a_hi, w_hi)) + br_ref[...]

    lane = lax.broadcasted_iota(jnp.int32, lg.shape, 1)
    rmax = lambda v: jnp.max(v, axis=1, keepdims=True)
    rmin = lambda v: jnp.min(v, axis=1, keepdims=True)
    rsum = lambda v: jnp.sum(v, axis=1, keepdims=True)

    gmask = lane < N_GROUPS
    gl = jnp.where(gmask, lg, -jnp.inf)
    gmax = rmax(gl)
    gsel = rmin(jnp.where(gl == gmax, lane, LANES))
    p_g = 1.0 / rsum(jnp.where(gmask, jnp.exp(lg - gmax), 0.0))

    e_lo = EXPERT_LANE0 + gsel * EXPERTS_PER_GROUP
    emask = (lane >= e_lo) & (lane < e_lo + EXPERTS_PER_GROUP)
    emax = rmax(jnp.where(emask, lg, -jnp.inf))
    ex = jnp.where(emask, jnp.exp(lg - emax), 0.0)
    prob = jnp.where(emask, ex / rsum(ex), -1.0)
    top1 = rmax(prob)
    i1 = rmin(jnp.where(prob == top1, lane, LANES))
    prob2 = jnp.where(lane == i1, -1.0, prob)
    top2 = rmax(prob2)
    i2 = rmin(jnp.where(prob2 == top2, lane, LANES))
    denom = top1 + top2
    w1 = p_g * top1 / denom
    w2 = p_g * top2 / denom

    sel = ((lane == i1) | (lane == i2))
    row = lax.broadcasted_iota(jnp.int32, (tm, tm), 0)
    col = lax.broadcasted_iota(jnp.int32, (tm, tm), 1)
    before = (col < row).astype(BF16)
    prefix = dot(before, jnp.where(sel, 1.0, 0.0).astype(BF16)) + carry_ref[...]
    rank1 = rsum(jnp.where(lane == i1, prefix, 0.0))
    rank2 = rsum(jnp.where(lane == i2, prefix, 0.0))
    carry_ref[...] = carry_ref[...] + jnp.sum(jnp.where(sel, 1.0, 0.0), axis=0, keepdims=True)
    cnt_ref[...] = carry_ref[...]

    e1 = (i1 - EXPERT_LANE0).astype(F32)
    e2 = (i2 - EXPERT_LANE0).astype(F32)
    out = jnp.zeros(lg.shape, F32)
    for k, val in enumerate((e1, e2, w1, w2, rank1, rank2)):
        out = jnp.where(lane == k, val, out)
    r_ref[...] = out


def _route(h1, g, wr3, br):
    n, d = h1.shape
    tm = TILE
    return pl.pallas_call(
        functools.partial(_route_kernel, tm=tm, half=d // 2),
        out_shape=(jax.ShapeDtypeStruct((n, d // 2), jnp.uint32),
                   jax.ShapeDtypeStruct((n, LANES), F32),
                   jax.ShapeDtypeStruct((1, LANES), F32)),
        grid=(n // tm,),
        in_specs=[
            pl.BlockSpec((tm, d), lambda i: (i, 0)),
            pl.BlockSpec((1, d), lambda i: (0, 0)),
            pl.BlockSpec(wr3.shape, lambda i: (0, 0, 0)),
            pl.BlockSpec((1, LANES), lambda i: (0, 0)),
        ],
        out_specs=(pl.BlockSpec((tm, d // 2), lambda i: (i, 0)),
                   pl.BlockSpec((tm, LANES), lambda i: (i, 0)),
                   pl.BlockSpec((1, LANES), lambda i: (0, 0))),
        scratch_shapes=[pltpu.VMEM((1, LANES), F32)],
        compiler_params=_cparams(("arbitrary",)),
        name="route",
    )(h1, g.reshape(1, d), wr3, br)


def _dispatch_kernel(d0_ref, d1_ref, src_ref, init_ref, dst_ref, sem, *, tm):
    del init_ref
    base = pl.program_id(0) * tm

    def copy(r, dest_ref, s):
        return pltpu.make_async_copy(src_ref.at[pl.ds(r, 1)], dst_ref.at[pl.ds(dest_ref[base + r], 1)], sem.at[s])

    def start(r, c):
        copy(r, d0_ref, 0).start()
        copy(r, d1_ref, 1).start()
        return c

    lax.fori_loop(0, tm, start, 0, unroll=DMA_ISSUE_UNROLL)
    for s in range(2):
        pltpu.make_async_copy(src_ref, dst_ref.at[pl.ds(0, tm)], sem.at[s]).wait()


def _dispatch(dest0, dest1, hp, p_rows):
    n, w = hp.shape
    tm = TILE
    init = jnp.zeros((p_rows, w), hp.dtype)
    return pl.pallas_call(
        functools.partial(_dispatch_kernel, tm=tm),
        out_shape=jax.ShapeDtypeStruct((p_rows, w), hp.dtype),
        grid_spec=pltpu.PrefetchScalarGridSpec(
            num_scalar_prefetch=2,
            grid=(n // tm,),
            in_specs=[pl.BlockSpec((tm, w), lambda i, a, b: (i, 0)), pl.BlockSpec(memory_space=pl.ANY)],
            out_specs=pl.BlockSpec(memory_space=pl.ANY),
            scratch_shapes=[pltpu.SemaphoreType.DMA((2,))],
        ),
        input_output_aliases={3: 0},
        compiler_params=_cparams(("arbitrary",)),
        name="dispatch",
    )(dest0, dest1, hp, init)


def _expert_kernel(nu_ref, fe_ref, fh_ref, new_ref, cslot_ref, uslot_ref,
                   x_ref, w1_ref, w3_ref, w2_ref, y_ref, c1_ref, c3_ref, c2_ref):
    del fe_ref
    j = pl.program_id(0)
    h = pl.program_id(1)
    t = j * EXPERT_FF_SPLIT + h
    live = j < nu_ref[0]

    @pl.when(new_ref[t] == 1)
    def _():
        dst = cslot_ref[t] * EXPERT_FF_SPLIT + fh_ref[t]
        c1_ref[dst] = w1_ref[0].astype(BF16)
        c3_ref[dst] = w3_ref[0].astype(BF16)
        c2_ref[dst] = w2_ref[0].astype(BF16)

    @pl.when(live)
    def _():
        src = uslot_ref[j] * EXPERT_FF_SPLIT + h
        xw = x_ref[...]
        lo = pltpu.bitcast(xw << 16, F32).astype(BF16)
        hi = pltpu.bitcast(xw & jnp.uint32(0xFFFF0000), F32).astype(BF16)
        xb = jnp.concatenate([lo, hi], axis=1)
        a = jnp.dot(xb, c1_ref[src], preferred_element_type=F32)
        g = jnp.dot(xb, c3_ref[src], preferred_element_type=F32)
        hid = (a * (1.0 / (1.0 + jnp.exp(-a))) * g).astype(BF16)
        part = jnp.dot(hid, c2_ref[src], preferred_element_type=F32)

        @pl.when(h == 0)
        def _():
            y_ref[...] = part

        @pl.when(h > 0)
        def _():
            y_ref[...] = y_ref[...] + part

    @pl.when(jnp.logical_not(live) & (h == 0))
    def _():
        y_ref[...] = jnp.zeros_like(y_ref)


def _expert_stream_plan(block_e, n_used, n_exp):
    nb = block_e.shape[0]
    nh = EXPERT_FF_SPLIT
    live = jnp.arange(nb) < n_used[0]
    first = jnp.concatenate([jnp.ones((1,), bool), block_e[1:] != block_e[:-1]]) & live
    seg = jnp.cumsum(first.astype(jnp.int32)) - 1
    n_seg = jnp.sum(first.astype(jnp.int32))
    seg_ids = jnp.arange(n_exp, dtype=jnp.int32)
    seg_expert = jnp.sum(jnp.where(first[None, :] & (seg[None, :] == seg_ids[:, None]), block_e[None, :], 0), axis=1)
    t = jnp.arange(nb * nh, dtype=jnp.int32)
    limit = jnp.where(live[t // nh], nh * seg[t // nh] + 2 * nh - 1, nh * n_seg - 1)
    pos = jnp.minimum(t + jnp.minimum(lax.cummin(limit - t), 0), nh * n_seg - 1)
    new = jnp.concatenate([jnp.ones((1,), jnp.int32), (pos[1:] != pos[:-1]).astype(jnp.int32)])
    item_seg = pos // nh
    fetch_e = jnp.sum(jnp.where(item_seg[:, None] == seg_ids[None, :], seg_expert[None, :], 0), axis=1)
    return (fetch_e.astype(jnp.int32), (pos % nh).astype(jnp.int32), new,
            (item_seg % 2).astype(jnp.int32), (seg % 2).astype(jnp.int32))


def _experts(block_e, n_used, xs, w1, w3, w2):
    p_rows, half = xs.shape
    n_exp, d, ff = w1.shape
    nb = p_rows // ROUTE_BLOCK
    nh = EXPERT_FF_SPLIT
    fh = ff // nh
    fetch_e, fetch_h, new, cslot, uslot = _expert_stream_plan(block_e, n_used, n_exp)
    last = lambda j, nu: jnp.minimum(j, nu[0] - 1)
    step = lambda j, h: j * nh + h
    return pl.pallas_call(
        _expert_kernel,
        out_shape=jax.ShapeDtypeStruct((p_rows, d), F32),
        grid_spec=pltpu.PrefetchScalarGridSpec(
            num_scalar_prefetch=6,
            grid=(nb, nh),
            in_specs=[
                pl.BlockSpec((ROUTE_BLOCK, half), lambda j, h, nu, fe, fhh, *_: (last(j, nu), 0)),
                pl.BlockSpec((1, d, fh), lambda j, h, nu, fe, fhh, *_: (fe[step(j, h)], 0, fhh[step(j, h)])),
                pl.BlockSpec((1, d, fh), lambda j, h, nu, fe, fhh, *_: (fe[step(j, h)], 0, fhh[step(j, h)])),
                pl.BlockSpec((1, fh, d), lambda j, h, nu, fe, fhh, *_: (fe[step(j, h)], fhh[step(j, h)], 0)),
            ],
            out_specs=pl.BlockSpec((ROUTE_BLOCK, d), lambda j, h, *_: (j, 0)),
            scratch_shapes=[pltpu.VMEM((2 * nh, d, fh), BF16), pltpu.VMEM((2 * nh, d, fh), BF16),
                            pltpu.VMEM((2 * nh, fh, d), BF16)],
        ),
        compiler_params=pltpu.CompilerParams(dimension_semantics=("arbitrary", "arbitrary"),
                                             vmem_limit_bytes=EXPERT_VMEM_LIMIT),
        name="experts",
    )(n_used, fetch_e, fetch_h, new, cslot, uslot, xs, w1, w3, w2)


def _combine_kernel(d0_ref, d1_ref, h_ref, r_ref, g_ref, y_ref, o_ref, ya, yb, sem, *, tm):
    i = pl.program_id(0)
    last = pl.num_programs(0) - 1
    slot = i % 2
    ahead = jnp.minimum(i + 1, last)

    def issue(tile, sl, r):
        row = tile * tm + r
        pltpu.make_async_copy(y_ref.at[pl.ds(d0_ref[row], 1)], ya.at[sl, pl.ds(r, 1)], sem.at[sl, 0]).start()
        pltpu.make_async_copy(y_ref.at[pl.ds(d1_ref[row], 1)], yb.at[sl, pl.ds(r, 1)], sem.at[sl, 1]).start()

    def wait(sl):
        pltpu.make_async_copy(y_ref.at[pl.ds(0, tm)], ya.at[sl], sem.at[sl, 0]).wait()
        pltpu.make_async_copy(y_ref.at[pl.ds(0, tm)], yb.at[sl], sem.at[sl, 1]).wait()

    @pl.when(i == 0)
    def _():
        def first(r, c):
            issue(0, 0, r)
            return c
        lax.fori_loop(0, tm, first, 0, unroll=DMA_ISSUE_UNROLL)

    wait(slot)
    lane = lax.broadcasted_iota(jnp.int32, (COMBINE_CHUNK, LANES), 1)

    def chunk(c, carry):
        r0 = pl.multiple_of(c * COMBINE_CHUNK, COMBINE_CHUNK)
        for k in range(COMBINE_CHUNK):
            issue(ahead, 1 - slot, r0 + k)
        rows = pl.ds(r0, COMBINE_CHUNK)
        route = r_ref[rows, :]
        w1 = jnp.sum(jnp.where(lane == 2, route, 0.0), axis=1, keepdims=True)
        w2 = jnp.sum(jnp.where(lane == 3, route, 0.0), axis=1, keepdims=True)
        h2 = h_ref[rows, :] + (ya[slot, rows, :] * w1 + yb[slot, rows, :] * w2)
        o_ref[rows, :] = _rms(h2, g_ref[...])
        return carry

    lax.fori_loop(0, tm // COMBINE_CHUNK, chunk, 0)

    @pl.when(i == last)
    def _():
        wait(1 - slot)


def _combine(dest0, dest1, h1, route, g, y):
    n, d = h1.shape
    tm = TILE
    return pl.pallas_call(
        functools.partial(_combine_kernel, tm=tm),
        out_shape=jax.ShapeDtypeStruct((n, d), F32),
        grid_spec=pltpu.PrefetchScalarGridSpec(
            num_scalar_prefetch=2,
            grid=(n // tm,),
            in_specs=[
                pl.BlockSpec((tm, d), lambda i, a, b: (i, 0)),
                pl.BlockSpec((tm, LANES), lambda i, a, b: (i, 0)),
                pl.BlockSpec((1, d), lambda i, a, b: (0, 0)),
                pl.BlockSpec(memory_space=pl.ANY),
            ],
            out_specs=pl.BlockSpec((tm, d), lambda i, a, b: (i, 0)),
            scratch_shapes=[pltpu.VMEM((2, tm, d), F32), pltpu.VMEM((2, tm, d), F32),
                            pltpu.SemaphoreType.DMA((2, 2))],
        ),
        compiler_params=_cparams(("arbitrary",)),
        name="combine",
    )(dest0, dest1, h1, route, g.reshape(1, d), y)


def _rope_tables(lp):
    half = MLA_ROPE_DIM // 2
    inv = ROPE_THETA ** (-jnp.arange(half, dtype=F32) / half)
    pos = (jnp.arange(lp) - (TILE - N_META)).astype(F32)
    ang = pos[:, None] * inv[None, :]
    cos, sin = jnp.cos(ang), jnp.sin(ang)
    z32 = jnp.zeros((lp, half), F32)
    z64 = jnp.zeros((lp, LANES - MLA_ROPE_DIM), F32)
    return (jnp.concatenate([cos, cos, z64], axis=1),
            jnp.concatenate([-sin, z32, z64], axis=1),
            jnp.concatenate([z32, sin, z64], axis=1))


def kernel(x, meta_tokens, norm1_g, w_in, b_gate, kv_norm_g, w_uk, w_uv, w_proj_a, w_proj_b, w_out,
           norm2_g, w_route_group, b_route_group, w_route_expert, b_route_expert, w1, w3, w2, final_g):
    b, seq, d = x.shape
    assert seq % TILE == 0 and TILE % CHUNK == 0 and N_META <= TILE
    lp = TILE + seq
    n_tok = b * lp
    n_real = b * seq
    n_exp = N_GROUPS * EXPERTS_PER_GROUP
    sb_w = SB_HEADS * SB_HEAD_DIM
    qk_dim = MLA_NOPE_DIM + MLA_ROPE_DIM
    mq_w = MLA_HEADS * qk_dim
    sb_scale = SB_HEAD_DIM ** -0.5
    mla_scale = qk_dim ** -0.5

    wi = w_in[0]
    o_q = 3 * sb_w
    o_c = o_q + mq_w
    o_r = o_c + MLA_KV_RANK
    o_g = o_r + MLA_ROPE_DIM
    w_sb = wi[:, :2 * sb_w].astype(BF16)
    w_sbv_t = wi[:, 2 * sb_w:o_q].T.astype(BF16)
    w_mq = wi[:, o_q:o_c].reshape(d, MLA_HEADS, qk_dim)
    w_mq_nope = w_mq[:, :, :MLA_NOPE_DIM].reshape(d, MLA_HEADS * MLA_NOPE_DIM).astype(BF16)
    w_mq_rope = w_mq[:, :, MLA_NOPE_DIM:].reshape(d, MLA_HEADS * MLA_ROPE_DIM).astype(BF16)
    w_c = jnp.pad(wi[:, o_c:o_g], ((0, 0), (0, LANES - MLA_ROPE_DIM))).astype(BF16)
    w_g = wi[:, o_g:].astype(BF16)
    sb_colscale = jnp.concatenate([jnp.full((1, sb_w), sb_scale * LOG2E, F32), jnp.ones((1, sb_w), F32)], axis=1)
    wr = jnp.zeros((d, LANES), F32)
    wr = wr.at[:, :N_GROUPS].set(w_route_group[0]).at[:, EXPERT_LANE0:EXPERT_LANE0 + n_exp].set(w_route_expert[0])
    wr_hi = wr.astype(BF16)
    wr_mid = (wr - wr_hi.astype(F32)).astype(BF16)
    wr3 = jnp.stack([wr_hi, wr_mid])
    br = jnp.zeros((1, LANES), F32)
    br = br.at[0, :N_GROUPS].set(b_route_group[0]).at[0, EXPERT_LANE0:EXPERT_LANE0 + n_exp].set(b_route_expert[0])

    head = jnp.concatenate([jnp.zeros((TILE - N_META, d), F32), meta_tokens.astype(F32)], axis=0)
    hn = _norm1(x, head, norm1_g[0]).reshape(n_tok, d)
    tm = _row_tile(lp)
    tpb = lp // tm
    tables = _rope_tables(lp)
    row_spec = lambda tn: pl.BlockSpec((1, tn), lambda j, i: (0, j))

    sb_qk = _proj(_proj_scale_kernel, hn, w_sb, [sb_colscale], [row_spec(sb_w)], BF16, tm, sb_w)
    sb_vt = _proj_t(hn, w_sbv_t, tm)
    q_mla = _proj_mlaq(hn, w_mq_nope, w_mq_rope, tables, tm, tpb, min(4, MLA_HEADS), mla_scale * LOG2E)
    ckr = _proj(_proj_scale_kernel, hn, w_c, [jnp.ones((1, w_c.shape[1]), F32)],
                [row_spec(w_c.shape[1])], F32, tm, w_c.shape[1])
    gates = _proj(_proj_gate_kernel, hn, w_g, [b_gate[0].reshape(1, 2 * d)], [row_spec(d)], F32, tm, d)
    k_mla, vt_mla = _kvup(ckr, kv_norm_g[0], w_uk[0].astype(BF16), w_uv[0].T.astype(BF16), tables, tm, tpb)

    o_a = _sb_attention(sb_qk.reshape(b, lp, 2 * sb_w), sb_vt, seq)
    o_b = _mla_attention(q_mla.reshape(b, lp, -1), k_mla.reshape(b, lp, -1), vt_mla, seq)
    h1 = _merge(o_a, o_b, gates.reshape(b, lp, 2 * d), x,
                w_proj_a[0].astype(BF16), w_proj_b[0].astype(BF16), w_out[0].astype(BF16)).reshape(n_real, d)

    hp, route, counts = _route(h1, norm2_g[0], wr3, br)
    cnt = counts[0, EXPERT_LANE0:EXPERT_LANE0 + n_exp].astype(jnp.int32)
    padded = (cnt + ROUTE_BLOCK - 1) // ROUTE_BLOCK * ROUTE_BLOCK
    pends = jnp.cumsum(padded)
    pstarts = pends - padded
    n_blocks = (n_real * TOP_K + n_exp * (ROUTE_BLOCK - 1) + ROUTE_BLOCK - 1) // ROUTE_BLOCK
    n_used = (pends[-1] // ROUTE_BLOCK).astype(jnp.int32).reshape(1)
    blk_start = jnp.minimum(jnp.arange(n_blocks), n_used[0] - 1) * ROUTE_BLOCK
    block_e = jnp.minimum(jnp.sum(pends[None, :] <= blk_start[:, None], axis=1), n_exp - 1).astype(jnp.int32)
    ids = route[:, :6].astype(jnp.int32)
    expert_iota = jnp.arange(n_exp, dtype=jnp.int32)[None, :]
    start_of = lambda e: jnp.sum(jnp.where(e[:, None] == expert_iota, pstarts[None, :], 0), axis=1)
    dest0 = (start_of(ids[:, 0]) + ids[:, 4]).astype(jnp.int32)
    dest1 = (start_of(ids[:, 1]) + ids[:, 5]).astype(jnp.int32)

    xs = _dispatch(dest0, dest1, hp, n_blocks * ROUTE_BLOCK)
    y = _experts(block_e, n_used, xs, w1[0], w3[0], w2[0])
    out = _combine(dest0, dest1, h1, route, final_g, y)
    return out.reshape(b, seq, d)
```

```python
import functools

import jax
import jax.numpy as jnp
from jax import lax
from jax.experimental import pallas as pl
from jax.experimental.pallas import tpu as pltpu

N_META = 16
CHUNK = 64
SB_HEADS = 8
SB_HEAD_DIM = 128
MLA_HEADS = 16
MLA_NOPE_DIM = 128
MLA_ROPE_DIM = 64
MLA_V_DIM = 128
MLA_KV_RANK = 512
ROPE_THETA = 10000.0
N_GROUPS = 4
EXPERTS_PER_GROUP = 8
TOP_K = 2
ROUTE_BLOCK = 256
RMS_EPS = 1e-6

TILE = 256
LANES = 128
MLA_QK_PAD = 256
ATTN_Q_TILES = 4
EXPERT_LANE0 = 8
VMEM_LIMIT = 56 * 1024 * 1024
EXPERT_VMEM_LIMIT = 62 * 1024 * 1024
NEG_BIG = -1e30
DMA_ISSUE_UNROLL = 8
COMBINE_CHUNK = 64
DENOM_ROWS = 16
EXPERT_FF_SPLIT = 2
MLA_HEADS_PER_STEP = 2
STICK_GONE_LOG2 = 152.0

F32 = jnp.float32
BF16 = jnp.bfloat16
LOG2E = 1.4426950408889634


def _cparams(sem):
    return pltpu.CompilerParams(dimension_semantics=sem, vmem_limit_bytes=VMEM_LIMIT)


def _rms(v, g):
    ms = jnp.mean(v * v, axis=-1, keepdims=True)
    return v * lax.rsqrt(ms + RMS_EPS) * g


def _row_tile(lp):
    for t in (768, 512, 256):
        if lp % t == 0:
            return t
    raise ValueError(lp)


def _norm1_kernel(x_ref, head_ref, g_ref, o_ref):
    i = pl.program_id(1)

    @pl.when(i == 0)
    def _():
        o_ref[0] = _rms(head_ref[...], g_ref[...]).astype(BF16)

    @pl.when(i > 0)
    def _():
        o_ref[0] = _rms(x_ref[0], g_ref[...]).astype(BF16)


def _norm1(x, head, g):
    b, s, d = x.shape
    nt = s // TILE + 1
    return pl.pallas_call(
        _norm1_kernel,
        out_shape=jax.ShapeDtypeStruct((b, nt * TILE, d), BF16),
        grid=(b, nt),
        in_specs=[
            pl.BlockSpec((1, TILE, d), lambda bi, i: (bi, jnp.maximum(i - 1, 0), 0)),
            pl.BlockSpec((TILE, d), lambda bi, i: (0, 0)),
            pl.BlockSpec((1, d), lambda bi, i: (0, 0)),
        ],
        out_specs=pl.BlockSpec((1, TILE, d), lambda bi, i: (bi, i, 0)),
        compiler_params=_cparams(("parallel", "parallel")),
        name="norm1",
    )(x, head, g.reshape(1, d))


def _rope_rows(r, cos_t, sin_a, sin_b):
    return r * cos_t + pltpu.roll(r, 96, 1) * sin_a + pltpu.roll(r, 32, 1) * sin_b


def _proj_scale_kernel(x_ref, w_ref, s_ref, o_ref):
    acc = jnp.dot(x_ref[...], w_ref[...], preferred_element_type=F32)
    o_ref[...] = (acc * s_ref[...]).astype(o_ref.dtype)


def _proj_gate_kernel(x_ref, w_ref, b_ref, o_ref):
    acc = jnp.dot(x_ref[...], w_ref[...], preferred_element_type=F32) + b_ref[...]
    o_ref[...] = 1.0 / (1.0 + jnp.exp(-acc))


def _proj_mlaq_kernel(x_ref, wn_ref, wr_ref, cos_ref, sa_ref, sb_ref, o_ref, *, scale):
    tm = x_ref.shape[0]
    parts = 3 if tm % 48 == 0 else 1
    rows = tm // parts
    low = lax.broadcasted_iota(jnp.int32, (rows, LANES), 1) < MLA_ROPE_DIM
    for part in range(parts):
        rs = slice(part * rows, (part + 1) * rows)
        x = x_ref[rs, :]
        acc_n = jnp.dot(x, wn_ref[...], preferred_element_type=F32)
        acc_r = jnp.dot(x, wr_ref[...], preferred_element_type=F32)
        cos_t, sin_a, sin_b = cos_ref[rs, :], sa_ref[rs, :], sb_ref[rs, :]
        for hh in range(acc_n.shape[1] // MLA_NOPE_DIM):
            pair = acc_r[:, (hh // 2) * LANES:(hh // 2 + 1) * LANES]
            if hh % 2:
                pair = pltpu.roll(pair, MLA_ROPE_DIM, 1)
            rope = _rope_rows(jnp.where(low, pair, 0.0), cos_t, sin_a, sin_b) * scale
            c0 = hh * MLA_QK_PAD
            o_ref[rs, c0:c0 + LANES] = (acc_n[:, hh * LANES:(hh + 1) * LANES] * scale).astype(o_ref.dtype)
            o_ref[rs, c0 + LANES:c0 + 2 * LANES] = rope.astype(o_ref.dtype)


def _proj_mlaq(hn2d, wn, wr, tables, tm, tiles_per_batch, group, scale):
    m, k = hn2d.shape
    heads = wn.shape[1] // MLA_NOPE_DIM
    tab_spec = pl.BlockSpec((tm, LANES), lambda j, i: (i % tiles_per_batch, 0))
    return pl.pallas_call(
        functools.partial(_proj_mlaq_kernel, scale=scale),
        out_shape=jax.ShapeDtypeStruct((m, heads * MLA_QK_PAD), BF16),
        grid=(heads // group, m // tm),
        in_specs=[
            pl.BlockSpec((tm, k), lambda j, i: (i, 0)),
            pl.BlockSpec((k, group * MLA_NOPE_DIM), lambda j, i: (0, j)),
            pl.BlockSpec((k, group * MLA_ROPE_DIM), lambda j, i: (0, j)),
            tab_spec, tab_spec, tab_spec,
        ],
        out_specs=pl.BlockSpec((tm, group * MLA_QK_PAD), lambda j, i: (i, j)),
        compiler_params=_cparams(("parallel", "parallel")),
        name="proj_mlaq",
    )(hn2d, wn, wr, *tables)


def _store_lane_tiles(o_ref, val_t):
    for c in range(val_t.shape[1] // TILE):
        o_ref[c] = val_t[:, c * TILE:(c + 1) * TILE].astype(o_ref.dtype)


def _proj_t_kernel(x_ref, wt_ref, o_ref):
    acc_t = lax.dot_general(wt_ref[...], x_ref[...], (((1,), (1,)), ((), ())), preferred_element_type=F32)
    _store_lane_tiles(o_ref, acc_t)


def _proj_t(hn2d, wt, tm):
    m, k = hn2d.shape
    n = wt.shape[0]
    return pl.pallas_call(
        _proj_t_kernel,
        out_shape=jax.ShapeDtypeStruct((m // TILE, n, TILE), BF16),
        grid=(m // tm,),
        in_specs=[pl.BlockSpec((tm, k), lambda i: (i, 0)), pl.BlockSpec((n, k), lambda i: (0, 0))],
        out_specs=pl.BlockSpec((tm // TILE, n, TILE), lambda i: (i, 0, 0)),
        compiler_params=_cparams(("parallel",)),
        name="proj_t",
    )(hn2d, wt)


def _proj(kernel, hn2d, w, extras, extra_specs, out_dtype, tm, tn):
    m, k = hn2d.shape
    n = w.shape[1]
    return pl.pallas_call(
        kernel,
        out_shape=jax.ShapeDtypeStruct((m, n), out_dtype),
        grid=(n // tn, m // tm),
        in_specs=[
            pl.BlockSpec((tm, k), lambda j, i: (i, 0)),
            pl.BlockSpec((k, tn), lambda j, i: (0, j)),
        ] + extra_specs,
        out_specs=pl.BlockSpec((tm, tn), lambda j, i: (i, j)),
        compiler_params=_cparams(("parallel", "parallel")),
        name=getattr(kernel, "__name__", None) or kernel.func.__name__,
    )(hn2d, w, *extras)


def _kvup_kernel(c_ref, g_ref, wk_ref, wvt_ref, cos_ref, sa_ref, sb_ref, k_ref, vt_ref, *, rank):
    ckr = c_ref[...]
    cn = _rms(ckr[:, :rank], g_ref[...]).astype(BF16)
    kn = jnp.dot(cn, wk_ref[...], preferred_element_type=F32)
    vv_t = lax.dot_general(wvt_ref[...], cn, (((1,), (1,)), ((), ())), preferred_element_type=F32)
    rope = _rope_rows(ckr[:, rank:rank + LANES], cos_ref[...], sa_ref[...], sb_ref[...]).astype(BF16)
    for h in range(kn.shape[1] // MLA_NOPE_DIM):
        k_ref[:, h * MLA_QK_PAD:h * MLA_QK_PAD + LANES] = kn[:, h * LANES:(h + 1) * LANES].astype(BF16)
        k_ref[:, h * MLA_QK_PAD + LANES:(h + 1) * MLA_QK_PAD] = rope
    _store_lane_tiles(vt_ref, vv_t)


def _kvup(ckr, g, wk, wvt, tables, tm, tiles_per_batch):
    m, cw = ckr.shape
    rank = wk.shape[0]
    nk = wk.shape[1]
    nv = wvt.shape[0]
    heads = nk // MLA_NOPE_DIM
    tab_spec = pl.BlockSpec((tm, LANES), lambda i: (i % tiles_per_batch, 0))
    return pl.pallas_call(
        functools.partial(_kvup_kernel, rank=rank),
        out_shape=(jax.ShapeDtypeStruct((m, heads * MLA_QK_PAD), BF16),
                   jax.ShapeDtypeStruct((m // TILE, nv, TILE), BF16)),
        grid=(m // tm,),
        in_specs=[
            pl.BlockSpec((tm, cw), lambda i: (i, 0)),
            pl.BlockSpec((1, rank), lambda i: (0, 0)),
            pl.BlockSpec(wk.shape, lambda i: (0, 0)),
            pl.BlockSpec(wvt.shape, lambda i: (0, 0)),
            tab_spec, tab_spec, tab_spec,
        ],
        out_specs=(pl.BlockSpec((tm, heads * MLA_QK_PAD), lambda i: (i, 0)),
                   pl.BlockSpec((tm // TILE, nv, TILE), lambda i: (i, 0, 0))),
        compiler_params=_cparams(("parallel",)),
        name="kvup",
    )(ckr, g.reshape(1, rank), wk, wvt, *tables)


def _softplus2(z):
    neg_abs = pltpu.bitcast(pltpu.bitcast(z, jnp.uint32) | jnp.uint32(0x80000000), F32)
    return jnp.maximum(z, 0.0) + jnp.log2(1.0 + jnp.exp2(neg_abs))


def _sb_kernel(*refs, nsub, first_valid):
    q_refs = refs[:nsub]
    k_ref, vt_ref, o_ref, carry_ref, acc_ref = refs[nsub:]
    qs = pl.program_id(2)
    first_tile = 1 + nsub * qs
    carry_ref[...] = jnp.zeros_like(carry_ref)
    acc_ref[...] = jnp.zeros_like(acc_ref)
    trow = lax.broadcasted_iota(jnp.int32, (TILE, 2 * TILE), 0)
    tcol = lax.broadcasted_iota(jnp.int32, (TILE, 2 * TILE), 1) & (TILE - 1)
    tri2 = jnp.where(tcol >= trow, 1.0, 0.0).astype(BF16)

    def update(c0, c1, j, mask):
        lanes = slice(c0 * TILE, c1 * TILE)
        q = jnp.concatenate([q_refs[c][0] for c in range(c0, c1)], axis=0)
        kb = k_ref[0, pl.ds(pl.multiple_of(j * TILE, TILE), TILE), :]
        z = lax.dot_general(kb, q, (((1,), (1,)), ((), ())), preferred_element_type=F32)
        sp = _softplus2(z)
        if mask is not None:
            sp = jnp.where(mask, sp, 0.0)
        hi32 = pltpu.bitcast(pltpu.bitcast(sp, jnp.uint32) & jnp.uint32(0xFFFF0000), F32)
        parts = jnp.concatenate([hi32.astype(BF16), (sp - hi32).astype(BF16)], axis=0)
        cs = jnp.dot(tri2, parts, preferred_element_type=F32)
        carry = carry_ref[:, lanes]
        a = jnp.exp2(z - cs - carry)
        if mask is not None:
            a = jnp.where(mask, a, 0.0)
        acc_ref[:, lanes] = acc_ref[:, lanes] + jnp.dot(vt_ref[j], a.astype(BF16), preferred_element_type=F32)
        carry_ref[:, lanes] = carry + cs[0:1, :]

    def stick_left(c0, c1):
        return jnp.min(carry_ref[:, c0 * TILE:c1 * TILE]) < STICK_GONE_LOG2

    def diag_mask(nq):
        key = lax.broadcasted_iota(jnp.int32, (TILE, nq * TILE), 0)
        qry = lax.broadcasted_iota(jnp.int32, (TILE, nq * TILE), 1)
        return (qry >= TILE) | (key < qry)

    half = nsub // 2
    for c in reversed(range(nsub)):
        near = min(c + half, nsub)
        update(c, near, first_tile + c, diag_mask(near - c))
        if near < nsub:
            @pl.when(stick_left(near, nsub))
            def _():
                update(near, nsub, first_tile + c, None)

    for c0, c1 in ((0, half), (half, nsub)):
        def body(state):
            j, _ = state
            update(c0, c1, j, None)
            return j - 1, stick_left(c0, c1)

        _, alive = lax.while_loop(lambda st: (st[0] > 0) & st[1], body, (first_tile - 1, stick_left(c0, c1)))

        @pl.when(alive)
        def _():
            key0 = lax.broadcasted_iota(jnp.int32, (TILE, (c1 - c0) * TILE), 0)
            update(c0, c1, 0, key0 >= first_valid)

    o_ref[0] = acc_ref[...].T.astype(o_ref.dtype)


def _q_specs(nsub, width, col0):
    return [pl.BlockSpec((1, TILE, width),
                         functools.partial(lambda bi, h, i, c: (bi, nsub * i + 1 + c, col0 + h), c=c))
            for c in range(nsub)]


def _sb_attention(qk, vt, seq):
    b, lp, w2 = qk.shape
    heads = w2 // (2 * SB_HEAD_DIM)
    nsub = ATTN_Q_TILES
    rows = nsub * TILE
    return pl.pallas_call(
        functools.partial(_sb_kernel, nsub=nsub, first_valid=TILE - N_META),
        out_shape=jax.ShapeDtypeStruct((b, seq, heads * SB_HEAD_DIM), BF16),
        grid=(b, heads, seq // rows),
        in_specs=_q_specs(nsub, SB_HEAD_DIM, 0) + [
            pl.BlockSpec((1, lp, SB_HEAD_DIM), lambda bi, h, i: (bi, 0, heads + h)),
            pl.BlockSpec((lp // TILE, SB_HEAD_DIM, TILE), lambda bi, h, i: (bi, h, 0)),
        ],
        out_specs=pl.BlockSpec((1, rows, SB_HEAD_DIM), lambda bi, h, i: (bi, i, h)),
        scratch_shapes=[pltpu.VMEM((1, rows), F32), pltpu.VMEM((SB_HEAD_DIM, rows), F32)],
        compiler_params=_cparams(("parallel", "parallel", "arbitrary")),
        name="sb_attention",
    )(*([qk] * (nsub + 1)), vt)


def _mla_kernel(*refs, nsub, nheads, first_valid):
    nq = nsub * nheads
    q_refs = refs[:nq]
    k_ref, vt_ref, bias_ref, o_ref = refs[nq:nq + 4]
    scratch = refs[nq + 4:]
    qs = pl.program_id(2)
    rows = nsub * TILE
    blk_keys = 2 * TILE
    n_full = (nsub * qs) // 2

    lower = slice(0, rows // 2)
    upper = slice(rows // 2, rows)

    def with_ones(vtb):
        return jnp.concatenate([vtb, jnp.ones((DENOM_ROWS, vtb.shape[1]), BF16)], axis=0)

    class Head:
        def __init__(self, g):
            self.qcols = slice(g * MLA_QK_PAD, (g + 1) * MLA_QK_PAD)
            self.vrows = slice(g * MLA_V_DIM, (g + 1) * MLA_V_DIM)
            self.q_refs = q_refs[g * nsub:(g + 1) * nsub]
            (self.m_ref, self.acc_ref, s0, s1, p0, p1, x0, x1) = scratch[g * 8:(g + 1) * 8]
            self.even = (s0, x0, p0)
            self.odd = (s1, x1, p1)

        def queries(self, lanes=slice(None)):
            tiles = self.q_refs[lanes.start // TILE:] if lanes.start else self.q_refs
            return jnp.concatenate([r[0] for r in tiles], axis=0)

        def vt(self, j):
            return vt_ref[j, self.vrows, :]

        def scores(self, blk, s_ref, smax_ref, lanes=slice(None)):
            kb = k_ref[0, pl.ds(pl.multiple_of((1 + 2 * blk) * TILE, TILE), blk_keys), self.qcols]
            s = lax.dot_general(kb, self.queries(lanes), (((1,), (1,)), ((), ())), preferred_element_type=F32)
            s_ref[:, lanes] = s
            smax_ref[:, lanes] = jnp.max(s, axis=0, keepdims=True)

        def softmax(self, s_ref, smax_ref, p_ref, bias, lanes=slice(None)):
            if bias is None:
                s = s_ref[:, lanes]
                smax = smax_ref[:, lanes]
            else:
                s = s_ref[:, lanes] + bias
                smax = jnp.max(s, axis=0, keepdims=True)
            m_old = self.m_ref[:, lanes]
            m_new = jnp.maximum(m_old, smax)
            alpha = jnp.exp2(m_old - m_new)
            p = jnp.exp2(s - m_new)
            self.m_ref[:, lanes] = m_new
            p_ref[:, lanes] = p.astype(BF16)
            return alpha

        def values(self, blk, p_ref, lanes=slice(None)):
            j = 1 + 2 * blk
            vtb = with_ones(jnp.concatenate([self.vt(j), self.vt(j + 1)], axis=1))
            return jnp.dot(vtb, p_ref[:, lanes], preferred_element_type=F32)

        def stage(self, blk, cur, nxt, bias=None, ahead=slice(None)):
            (s_cur, x_cur, p_cur), (s_nxt, x_nxt, p_prev) = cur, nxt
            self.scores(blk + 1, s_nxt, x_nxt, ahead)
            alpha = self.softmax(s_cur, x_cur, p_cur, bias)
            self.acc_ref[...] = alpha * (self.acc_ref[...] + self.values(jnp.maximum(blk - 1, 0), p_prev))

        def last_stage(self, blk, cur, nxt, bias):
            (s_cur, x_cur, p_cur), (_, _, p_prev) = cur, nxt
            alpha = self.softmax(s_cur, x_cur, p_cur, bias[:, upper], upper)
            pv = self.values(blk - 1, p_prev)
            self.acc_ref[:, lower] = self.acc_ref[:, lower] + pv[:, lower]
            self.acc_ref[:, upper] = alpha * (self.acc_ref[:, upper] + pv[:, upper])

        def start(self):
            s = lax.dot_general(k_ref[0, first_valid:TILE, self.qcols], self.queries(),
                                (((1,), (1,)), ((), ())), preferred_element_type=F32)
            m0 = jnp.max(s, axis=0, keepdims=True)
            p = jnp.exp2(s - m0)
            self.m_ref[...] = m0
            p_tile = jnp.concatenate([jnp.zeros((first_valid, rows), BF16), p.astype(BF16)], axis=0)
            self.acc_ref[...] = jnp.dot(with_ones(self.vt(0)), p_tile, preferred_element_type=F32)
            self.scores(0, self.even[0], self.even[1])
            self.odd[2][...] = jnp.zeros_like(self.odd[2])

        def finish(self):
            acc = jnp.concatenate([self.acc_ref[:, lower],
                                   self.acc_ref[:, upper] + self.values(n_full + 1, self.odd[2], upper)], axis=1)
            out = (acc[:MLA_V_DIM] / acc[MLA_V_DIM:MLA_V_DIM + 1]).T
            o_ref[0, :, self.vrows] = out.astype(o_ref.dtype)

    heads = [Head(g) for g in range(nheads)]
    for hd in heads:
        hd.start()

    def body(u, carry):
        for hd in heads:
            hd.stage(2 * u, hd.even, hd.odd)
        for hd in heads:
            hd.stage(2 * u + 1, hd.odd, hd.even)
        return carry

    lax.fori_loop(0, n_full // 2, body, 0)
    for hd in heads:
        hd.stage(n_full, hd.even, hd.odd, bias=bias_ref[:blk_keys, :], ahead=upper)
    for hd in heads:
        hd.last_stage(n_full + 1, hd.odd, hd.even, bias_ref[blk_keys:, :])
    for hd in heads:
        hd.finish()


def _mla_attention(q, k, vt, seq):
    b, lp, _ = q.shape
    heads = vt.shape[1] // MLA_V_DIM
    nsub = ATTN_Q_TILES
    nheads = min(MLA_HEADS_PER_STEP, heads)
    assert nsub == 4, "the kernel visits the query-overlapping keys as exactly two 2-tile blocks"
    assert heads % nheads == 0
    rows = nsub * TILE
    shift = CHUNK.bit_length() - 1
    key = lax.broadcasted_iota(jnp.int32, (rows, rows), 0)
    qry = lax.broadcasted_iota(jnp.int32, (rows, rows), 1)
    bias = jnp.where((key >> shift) <= (qry >> shift), 0.0, NEG_BIG).astype(F32)
    q_specs = [pl.BlockSpec((1, TILE, MLA_QK_PAD),
                            functools.partial(lambda bi, h, i, g, c: (bi, nsub * i + 1 + c, nheads * h + g), g=g, c=c))
               for g in range(nheads) for c in range(nsub)]
    per_head_scratch = [pltpu.VMEM((1, rows), F32),
                        pltpu.VMEM((MLA_V_DIM + DENOM_ROWS, rows), F32),
                        pltpu.VMEM((2 * TILE, rows), F32), pltpu.VMEM((2 * TILE, rows), F32),
                        pltpu.VMEM((2 * TILE, rows), BF16), pltpu.VMEM((2 * TILE, rows), BF16),
                        pltpu.VMEM((1, rows), F32), pltpu.VMEM((1, rows), F32)]
    return pl.pallas_call(
        functools.partial(_mla_kernel, nsub=nsub, nheads=nheads, first_valid=TILE - N_META),
        out_shape=jax.ShapeDtypeStruct((b, seq, heads * MLA_V_DIM), BF16),
        grid=(b, heads // nheads, seq // rows),
        in_specs=q_specs + [
            pl.BlockSpec((1, lp, nheads * MLA_QK_PAD), lambda bi, h, i: (bi, 0, h)),
            pl.BlockSpec((lp // TILE, nheads * MLA_V_DIM, TILE), lambda bi, h, i: (bi, h, 0)),
            pl.BlockSpec((rows, rows), lambda bi, h, i: (0, 0), pipeline_mode=pl.Buffered(1)),
        ],
        out_specs=pl.BlockSpec((1, rows, nheads * MLA_V_DIM), lambda bi, h, i: (bi, i, h)),
        scratch_shapes=per_head_scratch * nheads,
        compiler_params=_cparams(("parallel", "parallel", "arbitrary")),
        name="mla_attention",
    )(*([q] * (nsub * nheads)), k, vt, bias)


def _merge_kernel(oa_ref, ob_ref, ga_ref, gb_ref, x_ref, wpa_ref, wpb_ref, wo_ref, o_ref):
    pa = jnp.dot(oa_ref[0], wpa_ref[...], preferred_element_type=F32)
    pb = jnp.dot(ob_ref[0], wpb_ref[...], preferred_element_type=F32)
    y = ga_ref[0] * pa + gb_ref[0] * pb
    o_ref[0] = x_ref[0] + jnp.dot(y.astype(BF16), wo_ref[...], preferred_element_type=F32)


def _merge(oa, ob, gates, x, wpa, wpb, wo):
    b, s, d = x.shape
    nt = s // TILE
    resident = lambda w: pl.BlockSpec(w.shape, lambda bi, i: (0, 0), pipeline_mode=pl.Buffered(1))
    return pl.pallas_call(
        _merge_kernel,
        out_shape=jax.ShapeDtypeStruct((b, s, d), F32),
        grid=(b, nt),
        in_specs=[
            pl.BlockSpec((1, TILE, oa.shape[2]), lambda bi, i: (bi, i, 0)),
            pl.BlockSpec((1, TILE, ob.shape[2]), lambda bi, i: (bi, i, 0)),
            pl.BlockSpec((1, TILE, d), lambda bi, i: (bi, i + 1, 0)),
            pl.BlockSpec((1, TILE, d), lambda bi, i: (bi, i + 1, 1)),
            pl.BlockSpec((1, TILE, d), lambda bi, i: (bi, i, 0)),
            resident(wpa), resident(wpb), resident(wo),
        ],
        out_specs=pl.BlockSpec((1, TILE, d), lambda bi, i: (bi, i, 0)),
        compiler_params=_cparams(("parallel", "parallel")),
        name="merge",
    )(oa, ob, gates, gates, x, wpa, wpb, wo)


def _split2(a):
    hi = a.astype(BF16)
    return hi, (a - hi.astype(F32)).astype(BF16)


def _route_kernel(h_ref, g_ref, wr_ref, br_ref, hp_ref, r_ref, cnt_ref, carry_ref, *, tm, half):
    i = pl.program_id(0)

    @pl.when(i == 0)
    def _():
        carry_ref[...] = jnp.zeros_like(carry_ref)

    hn = _rms(h_ref[...], g_ref[...])

    lo_bits = pltpu.bitcast(hn[:, :half].astype(BF16).astype(F32), jnp.uint32)
    hi_bits = pltpu.bitcast(hn[:, half:].astype(BF16).astype(F32), jnp.uint32)
    hp_ref[...] = (hi_bits & jnp.uint32(0xFFFF0000)) | (lo_bits >> 16)

    a_hi, a_mid = _split2(hn)
    w_hi, w_mid = wr_ref[0], wr_ref[1]
    dot = lambda a, w: jnp.dot(a, w, preferred_element_type=F32)
    lg = (dot(a_mid, w_hi) + dot(a_hi, w_mid) + dot(a_hi, w_hi)) + br_ref[...]

    lane = lax.broadcasted_iota(jnp.int32, lg.shape, 1)
    rmax = lambda v: jnp.max(v, axis=1, keepdims=True)
    rmin = lambda v: jnp.min(v, axis=1, keepdims=True)
    rsum = lambda v: jnp.sum(v, axis=1, keepdims=True)

    gmask = lane < N_GROUPS
    gl = jnp.where(gmask, lg, -jnp.inf)
    gmax = rmax(gl)
    gsel = rmin(jnp.where(gl == gmax, lane, LANES))
    p_g = 1.0 / rsum(jnp.where(gmask, jnp.exp(lg - gmax), 0.0))

    e_lo = EXPERT_LANE0 + gsel * EXPERTS_PER_GROUP
    emask = (lane >= e_lo) & (lane < e_lo + EXPERTS_PER_GROUP)
    emax = rmax(jnp.where(emask, lg, -jnp.inf))
    ex = jnp.where(emask, jnp.exp(lg - emax), 0.0)
    prob = jnp.where(emask, ex / rsum(ex), -1.0)
    top1 = rmax(prob)
    i1 = rmin(jnp.where(prob == top1, lane, LANES))
    prob2 = jnp.where(lane == i1, -1.0, prob)
    top2 = rmax(prob2)
    i2 = rmin(jnp.where(prob2 == top2, lane, LANES))
    denom = top1 + top2
    w1 = p_g * top1 / denom
    w2 = p_g * top2 / denom

    sel = ((lane == i1) | (lane == i2))
    row = lax.broadcasted_iota(jnp.int32, (tm, tm), 0)
    col = lax.broadcasted_iota(jnp.int32, (tm, tm), 1)
    before = (col < row).astype(BF16)
    prefix = dot(before, jnp.where(sel, 1.0, 0.0).astype(BF16)) + carry_ref[...]
    rank1 = rsum(jnp.where(lane == i1, prefix, 0.0))
    rank2 = rsum(jnp.where(lane == i2, prefix, 0.0))
    carry_ref[...] = carry_ref[...] + jnp.sum(jnp.where(sel, 1.0, 0.0), axis=0, keepdims=True)
    cnt_ref[...] = carry_ref[...]

    e1 = (i1 - EXPERT_LANE0).astype(F32)
    e2 = (i2 - EXPERT_LANE0).astype(F32)
    out = jnp.zeros(lg.shape, F32)
    for k, val in enumerate((e1, e2, w1, w2, rank1, rank2)):
        out = jnp.where(lane == k, val, out)
    r_ref[...] = out


def _route(h1, g, wr3, br):
    n, d = h1.shape
    tm = TILE
    return pl.pallas_call(
        functools.partial(_route_kernel, tm=tm, half=d // 2),
        out_shape=(jax.ShapeDtypeStruct((n, d // 2), jnp.uint32),
                   jax.ShapeDtypeStruct((n, LANES), F32),
                   jax.ShapeDtypeStruct((1, LANES), F32)),
        grid=(n // tm,),
        in_specs=[
            pl.BlockSpec((tm, d), lambda i: (i, 0)),
            pl.BlockSpec((1, d), lambda i: (0, 0)),
            pl.BlockSpec(wr3.shape, lambda i: (0, 0, 0)),
            pl.BlockSpec((1, LANES), lambda i: (0, 0)),
        ],
        out_specs=(pl.BlockSpec((tm, d // 2), lambda i: (i, 0)),
                   pl.BlockSpec((tm, LANES), lambda i: (i, 0)),
                   pl.BlockSpec((1, LANES), lambda i: (0, 0))),
        scratch_shapes=[pltpu.VMEM((1, LANES), F32)],
        compiler_params=_cparams(("arbitrary",)),
        name="route",
    )(h1, g.reshape(1, d), wr3, br)


def _dispatch_kernel(d0_ref, d1_ref, src_ref, init_ref, dst_ref, sem, *, tm):
    del init_ref
    base = pl.program_id(0) * tm

    def copy(r, dest_ref, s):
        return pltpu.make_async_copy(src_ref.at[pl.ds(r, 1)], dst_ref.at[pl.ds(dest_ref[base + r], 1)], sem.at[s])

    def start(r, c):
        copy(r, d0_ref, 0).start()
        copy(r, d1_ref, 1).start()
        return c

    lax.fori_loop(0, tm, start, 0, unroll=DMA_ISSUE_UNROLL)
    for s in range(2):
        pltpu.make_async_copy(src_ref, dst_ref.at[pl.ds(0, tm)], sem.at[s]).wait()


def _dispatch(dest0, dest1, hp, p_rows):
    n, w = hp.shape
    tm = TILE
    init = jnp.zeros((p_rows, w), hp.dtype)
    return pl.pallas_call(
        functools.partial(_dispatch_kernel, tm=tm),
        out_shape=jax.ShapeDtypeStruct((p_rows, w), hp.dtype),
        grid_spec=pltpu.PrefetchScalarGridSpec(
            num_scalar_prefetch=2,
            grid=(n // tm,),
            in_specs=[pl.BlockSpec((tm, w), lambda i, a, b: (i, 0)), pl.BlockSpec(memory_space=pl.ANY)],
            out_specs=pl.BlockSpec(memory_space=pl.ANY),
            scratch_shapes=[pltpu.SemaphoreType.DMA((2,))],
        ),
        input_output_aliases={3: 0},
        compiler_params=_cparams(("arbitrary",)),
        name="dispatch",
    )(dest0, dest1, hp, init)


def _expert_kernel(nu_ref, fe_ref, fh_ref, new_ref, cslot_ref, uslot_ref, full_ref,
                   x_ref, w1_ref, w3_ref, w2_ref, y_ref, c1_ref, c3_ref, c2_ref):
    del fe_ref
    j = pl.program_id(0)
    h = pl.program_id(1)
    t = j * EXPERT_FF_SPLIT + h
    live = j < nu_ref[0]

    @pl.when(new_ref[t] == 1)
    def _():
        dst = cslot_ref[t] * EXPERT_FF_SPLIT + fh_ref[t]
        c1_ref[dst] = w1_ref[0].astype(BF16)
        c3_ref[dst] = w3_ref[0].astype(BF16)
        c2_ref[dst] = w2_ref[0].astype(BF16)

    def rows():
        xw = x_ref[...]
        lo = pltpu.bitcast(xw << 16, F32).astype(BF16)
        hi = pltpu.bitcast(xw & jnp.uint32(0xFFFF0000), F32).astype(BF16)
        return jnp.concatenate([lo, hi], axis=1)

    def part(xb, hh):
        src = uslot_ref[j] * EXPERT_FF_SPLIT + hh
        a = jnp.dot(xb, c1_ref[src], preferred_element_type=F32)
        g = jnp.dot(xb, c3_ref[src], preferred_element_type=F32)
        hid = (a * (1.0 / (1.0 + jnp.exp(-a))) * g).astype(BF16)
        return jnp.dot(hid, c2_ref[src], preferred_element_type=F32)

    whole = full_ref[j] == 1

    @pl.when(live & whole & (h == 0))
    def _():
        xb = rows()
        acc = part(xb, 0)
        for hh in range(1, EXPERT_FF_SPLIT):
            acc = acc + part(xb, hh)
        y_ref[...] = acc

    @pl.when(live & jnp.logical_not(whole))
    def _():
        contrib = part(rows(), h)

        @pl.when(h == 0)
        def _():
            y_ref[...] = contrib

        @pl.when(h > 0)
        def _():
            y_ref[...] = y_ref[...] + contrib

    @pl.when(jnp.logical_not(live) & (h == 0))
    def _():
        y_ref[...] = jnp.zeros_like(y_ref)


def _expert_stream_plan(block_e, n_used, n_exp):
    nb = block_e.shape[0]
    nh = EXPERT_FF_SPLIT
    live = jnp.arange(nb) < n_used[0]
    first = jnp.concatenate([jnp.ones((1,), bool), block_e[1:] != block_e[:-1]]) & live
    seg = jnp.cumsum(first.astype(jnp.int32)) - 1
    n_seg = jnp.sum(first.astype(jnp.int32))
    seg_ids = jnp.arange(n_exp, dtype=jnp.int32)
    seg_expert = jnp.sum(jnp.where(first[None, :] & (seg[None, :] == seg_ids[:, None]), block_e[None, :], 0), axis=1)
    t = jnp.arange(nb * nh, dtype=jnp.int32)
    limit = jnp.where(live[t // nh], nh * seg[t // nh] + 2 * nh - 1, nh * n_seg - 1)
    pos = jnp.minimum(t + jnp.minimum(lax.cummin(limit - t), 0), nh * n_seg - 1)
    new = jnp.concatenate([jnp.ones((1,), jnp.int32), (pos[1:] != pos[:-1]).astype(jnp.int32)])
    item_seg = pos // nh
    fetch_e = jnp.sum(jnp.where(item_seg[:, None] == seg_ids[None, :], seg_expert[None, :], 0), axis=1)
    full = (pos[::nh] >= nh * seg + nh - 1).astype(jnp.int32)
    return (fetch_e.astype(jnp.int32), (pos % nh).astype(jnp.int32), new,
            (item_seg % 2).astype(jnp.int32), (seg % 2).astype(jnp.int32), full)


def _experts(block_e, n_used, xs, w1, w3, w2):
    p_rows, half = xs.shape
    n_exp, d, ff = w1.shape
    nb = p_rows // ROUTE_BLOCK
    nh = EXPERT_FF_SPLIT
    fh = ff // nh
    plan = _expert_stream_plan(block_e, n_used, n_exp)
    last = lambda j, nu: jnp.minimum(j, nu[0] - 1)
    step = lambda j, h: j * nh + h
    return pl.pallas_call(
        _expert_kernel,
        out_shape=jax.ShapeDtypeStruct((p_rows, d), F32),
        grid_spec=pltpu.PrefetchScalarGridSpec(
            num_scalar_prefetch=1 + len(plan),
            grid=(nb, nh),
            in_specs=[
                pl.BlockSpec((ROUTE_BLOCK, half), lambda j, h, nu, fe, fhh, *_: (last(j, nu), 0)),
                pl.BlockSpec((1, d, fh), lambda j, h, nu, fe, fhh, *_: (fe[step(j, h)], 0, fhh[step(j, h)])),
                pl.BlockSpec((1, d, fh), lambda j, h, nu, fe, fhh, *_: (fe[step(j, h)], 0, fhh[step(j, h)])),
                pl.BlockSpec((1, fh, d), lambda j, h, nu, fe, fhh, *_: (fe[step(j, h)], fhh[step(j, h)], 0)),
            ],
            out_specs=pl.BlockSpec((ROUTE_BLOCK, d), lambda j, h, *_: (j, 0)),
            scratch_shapes=[pltpu.VMEM((2 * nh, d, fh), BF16), pltpu.VMEM((2 * nh, d, fh), BF16),
                            pltpu.VMEM((2 * nh, fh, d), BF16)],
        ),
        compiler_params=pltpu.CompilerParams(dimension_semantics=("arbitrary", "arbitrary"),
                                             vmem_limit_bytes=EXPERT_VMEM_LIMIT),
        name="experts",
    )(n_used, *plan, xs, w1, w3, w2)


def _combine_kernel(d0_ref, d1_ref, h_ref, r_ref, g_ref, y_ref, o_ref, ya, yb, sem, *, tm):
    i = pl.program_id(0)
    last = pl.num_programs(0) - 1
    slot = i % 2
    ahead = jnp.minimum(i + 1, last)

    def issue(tile, sl, r):
        row = tile * tm + r
        pltpu.make_async_copy(y_ref.at[pl.ds(d0_ref[row], 1)], ya.at[sl, pl.ds(r, 1)], sem.at[sl, 0]).start()
        pltpu.make_async_copy(y_ref.at[pl.ds(d1_ref[row], 1)], yb.at[sl, pl.ds(r, 1)], sem.at[sl, 1]).start()

    def wait(sl):
        pltpu.make_async_copy(y_ref.at[pl.ds(0, tm)], ya.at[sl], sem.at[sl, 0]).wait()
        pltpu.make_async_copy(y_ref.at[pl.ds(0, tm)], yb.at[sl], sem.at[sl, 1]).wait()

    @pl.when(i == 0)
    def _():
        def first(r, c):
            issue(0, 0, r)
            return c
        lax.fori_loop(0, tm, first, 0, unroll=DMA_ISSUE_UNROLL)

    wait(slot)
    lane = lax.broadcasted_iota(jnp.int32, (COMBINE_CHUNK, LANES), 1)

    def chunk(c, carry):
        r0 = pl.multiple_of(c * COMBINE_CHUNK, COMBINE_CHUNK)
        for k in range(COMBINE_CHUNK):
            issue(ahead, 1 - slot, r0 + k)
        rows = pl.ds(r0, COMBINE_CHUNK)
        route = r_ref[rows, :]
        w1 = jnp.sum(jnp.where(lane == 2, route, 0.0), axis=1, keepdims=True)
        w2 = jnp.sum(jnp.where(lane == 3, route, 0.0), axis=1, keepdims=True)
        h2 = h_ref[rows, :] + (ya[slot, rows, :] * w1 + yb[slot, rows, :] * w2)
        o_ref[rows, :] = _rms(h2, g_ref[...])
        return carry

    lax.fori_loop(0, tm // COMBINE_CHUNK, chunk, 0)

    @pl.when(i == last)
    def _():
        wait(1 - slot)


def _combine(dest0, dest1, h1, route, g, y):
    n, d = h1.shape
    tm = TILE
    return pl.pallas_call(
        functools.partial(_combine_kernel, tm=tm),
        out_shape=jax.ShapeDtypeStruct((n, d), F32),
        grid_spec=pltpu.PrefetchScalarGridSpec(
            num_scalar_prefetch=2,
            grid=(n // tm,),
            in_specs=[
                pl.BlockSpec((tm, d), lambda i, a, b: (i, 0)),
                pl.BlockSpec((tm, LANES), lambda i, a, b: (i, 0)),
                pl.BlockSpec((1, d), lambda i, a, b: (0, 0)),
                pl.BlockSpec(memory_space=pl.ANY),
            ],
            out_specs=pl.BlockSpec((tm, d), lambda i, a, b: (i, 0)),
            scratch_shapes=[pltpu.VMEM((2, tm, d), F32), pltpu.VMEM((2, tm, d), F32),
                            pltpu.SemaphoreType.DMA((2, 2))],
        ),
        compiler_params=_cparams(("arbitrary",)),
        name="combine",
    )(dest0, dest1, h1, route, g.reshape(1, d), y)


def _rope_tables(lp):
    half = MLA_ROPE_DIM // 2
    inv = ROPE_THETA ** (-jnp.arange(half, dtype=F32) / half)
    pos = (jnp.arange(lp) - (TILE - N_META)).astype(F32)
    ang = pos[:, None] * inv[None, :]
    cos, sin = jnp.cos(ang), jnp.sin(ang)
    z32 = jnp.zeros((lp, half), F32)
    z64 = jnp.zeros((lp, LANES - MLA_ROPE_DIM), F32)
    return (jnp.concatenate([cos, cos, z64], axis=1),
            jnp.concatenate([-sin, z32, z64], axis=1),
            jnp.concatenate([z32, sin, z64], axis=1))


def kernel(x, meta_tokens, norm1_g, w_in, b_gate, kv_norm_g, w_uk, w_uv, w_proj_a, w_proj_b, w_out,
           norm2_g, w_route_group, b_route_group, w_route_expert, b_route_expert, w1, w3, w2, final_g):
    b, seq, d = x.shape
    assert seq % TILE == 0 and TILE % CHUNK == 0 and N_META <= TILE
    lp = TILE + seq
    n_tok = b * lp
    n_real = b * seq
    n_exp = N_GROUPS * EXPERTS_PER_GROUP
    sb_w = SB_HEADS * SB_HEAD_DIM
    qk_dim = MLA_NOPE_DIM + MLA_ROPE_DIM
    mq_w = MLA_HEADS * qk_dim
    sb_scale = SB_HEAD_DIM ** -0.5
    mla_scale = qk_dim ** -0.5

    wi = w_in[0]
    o_q = 3 * sb_w
    o_c = o_q + mq_w
    o_r = o_c + MLA_KV_RANK
    o_g = o_r + MLA_ROPE_DIM
    w_sb = wi[:, :2 * sb_w].astype(BF16)
    w_sbv_t = wi[:, 2 * sb_w:o_q].T.astype(BF16)
    w_mq = wi[:, o_q:o_c].reshape(d, MLA_HEADS, qk_dim)
    w_mq_nope = w_mq[:, :, :MLA_NOPE_DIM].reshape(d, MLA_HEADS * MLA_NOPE_DIM).astype(BF16)
    w_mq_rope = w_mq[:, :, MLA_NOPE_DIM:].reshape(d, MLA_HEADS * MLA_ROPE_DIM).astype(BF16)
    w_c = jnp.pad(wi[:, o_c:o_g], ((0, 0), (0, LANES - MLA_ROPE_DIM))).astype(BF16)
    w_g = wi[:, o_g:].astype(BF16)
    sb_colscale = jnp.concatenate([jnp.full((1, sb_w), sb_scale * LOG2E, F32), jnp.ones((1, sb_w), F32)], axis=1)
    wr = jnp.zeros((d, LANES), F32)
    wr = wr.at[:, :N_GROUPS].set(w_route_group[0]).at[:, EXPERT_LANE0:EXPERT_LANE0 + n_exp].set(w_route_expert[0])
    wr_hi = wr.astype(BF16)
    wr_mid = (wr - wr_hi.astype(F32)).astype(BF16)
    wr3 = jnp.stack([wr_hi, wr_mid])
    br = jnp.zeros((1, LANES), F32)
    br = br.at[0, :N_GROUPS].set(b_route_group[0]).at[0, EXPERT_LANE0:EXPERT_LANE0 + n_exp].set(b_route_expert[0])

    head = jnp.concatenate([jnp.zeros((TILE - N_META, d), F32), meta_tokens.astype(F32)], axis=0)
    hn = _norm1(x, head, norm1_g[0]).reshape(n_tok, d)
    tm = _row_tile(lp)
    tpb = lp // tm
    tables = _rope_tables(lp)
    row_spec = lambda tn: pl.BlockSpec((1, tn), lambda j, i: (0, j))

    sb_qk = _proj(_proj_scale_kernel, hn, w_sb, [sb_colscale], [row_spec(sb_w)], BF16, tm, sb_w)
    sb_vt = _proj_t(hn, w_sbv_t, tm)
    q_mla = _proj_mlaq(hn, w_mq_nope, w_mq_rope, tables, tm, tpb, min(4, MLA_HEADS), mla_scale * LOG2E)
    ckr = _proj(_proj_scale_kernel, hn, w_c, [jnp.ones((1, w_c.shape[1]), F32)],
                [row_spec(w_c.shape[1])], F32, tm, w_c.shape[1])
    gates = _proj(_proj_gate_kernel, hn, w_g, [b_gate[0].reshape(1, 2 * d)], [row_spec(d)], F32, tm, d)
    k_mla, vt_mla = _kvup(ckr, kv_norm_g[0], w_uk[0].astype(BF16), w_uv[0].T.astype(BF16), tables, tm, tpb)

    o_a = _sb_attention(sb_qk.reshape(b, lp, 2 * sb_w), sb_vt, seq)
    o_b = _mla_attention(q_mla.reshape(b, lp, -1), k_mla.reshape(b, lp, -1), vt_mla, seq)
    h1 = _merge(o_a, o_b, gates.reshape(b, lp, 2 * d), x,
                w_proj_a[0].astype(BF16), w_proj_b[0].astype(BF16), w_out[0].astype(BF16)).reshape(n_real, d)

    hp, route, counts = _route(h1, norm2_g[0], wr3, br)
    cnt = counts[0, EXPERT_LANE0:EXPERT_LANE0 + n_exp].astype(jnp.int32)
    padded = (cnt + ROUTE_BLOCK - 1) // ROUTE_BLOCK * ROUTE_BLOCK
    pends = jnp.cumsum(padded)
    pstarts = pends - padded
    n_blocks = (n_real * TOP_K + n_exp * (ROUTE_BLOCK - 1) + ROUTE_BLOCK - 1) // ROUTE_BLOCK
    n_used = (pends[-1] // ROUTE_BLOCK).astype(jnp.int32).reshape(1)
    blk_start = jnp.minimum(jnp.arange(n_blocks), n_used[0] - 1) * ROUTE_BLOCK
    block_e = jnp.minimum(jnp.sum(pends[None, :] <= blk_start[:, None], axis=1), n_exp - 1).astype(jnp.int32)
    ids = route[:, :6].astype(jnp.int32)
    expert_iota = jnp.arange(n_exp, dtype=jnp.int32)[None, :]
    start_of = lambda e: jnp.sum(jnp.where(e[:, None] == expert_iota, pstarts[None, :], 0), axis=1)
    dest0 = (start_of(ids[:, 0]) + ids[:, 4]).astype(jnp.int32)
    dest1 = (start_of(ids[:, 1]) + ids[:, 5]).astype(jnp.int32)

    xs = _dispatch(dest0, dest1, hp, n_blocks * ROUTE_BLOCK)
    y = _experts(block_e, n_used, xs, w1[0], w3[0], w2[0])
    out = _combine(dest0, dest1, h1, route, final_g, y)
    return out.reshape(b, seq, d)
```

```python
import functools

import jax
import jax.numpy as jnp
from jax import lax
from jax.experimental import pallas as pl
from jax.experimental.pallas import tpu as pltpu

N_META = 16
CHUNK = 64
SB_HEADS = 8
SB_HEAD_DIM = 128
MLA_HEADS = 16
MLA_NOPE_DIM = 128
MLA_ROPE_DIM = 64
MLA_V_DIM = 128
MLA_KV_RANK = 512
ROPE_THETA = 10000.0
N_GROUPS = 4
EXPERTS_PER_GROUP = 8
TOP_K = 2
ROUTE_BLOCK = 256
RMS_EPS = 1e-6

TILE = 256
LANES = 128
MLA_QK_PAD = 256
ATTN_Q_TILES = 4
EXPERT_LANE0 = 8
VMEM_LIMIT = 56 * 1024 * 1024
EXPERT_VMEM_LIMIT = 62 * 1024 * 1024
NEG_BIG = -1e30
DMA_ISSUE_UNROLL = 8
COMBINE_CHUNK = 64
DENOM_ROWS = 16
EXPERT_FF_SPLIT = 2
MLA_HEADS_PER_STEP = 2
STICK_GONE_LOG2 = 152.0

F32 = jnp.float32
BF16 = jnp.bfloat16
LOG2E = 1.4426950408889634


def _cparams(sem):
    return pltpu.CompilerParams(dimension_semantics=sem, vmem_limit_bytes=VMEM_LIMIT)


def _rms(v, g):
    ms = jnp.mean(v * v, axis=-1, keepdims=True)
    return v * lax.rsqrt(ms + RMS_EPS) * g


def _row_tile(lp):
    for t in (768, 512, 256):
        if lp % t == 0:
            return t
    raise ValueError(lp)


def _norm1_kernel(x_ref, head_ref, g_ref, o_ref):
    i = pl.program_id(1)

    @pl.when(i == 0)
    def _():
        o_ref[0] = _rms(head_ref[...], g_ref[...]).astype(BF16)

    @pl.when(i > 0)
    def _():
        o_ref[0] = _rms(x_ref[0], g_ref[...]).astype(BF16)


def _norm1(x, head, g):
    b, s, d = x.shape
    nt = s // TILE + 1
    return pl.pallas_call(
        _norm1_kernel,
        out_shape=jax.ShapeDtypeStruct((b, nt * TILE, d), BF16),
        grid=(b, nt),
        in_specs=[
            pl.BlockSpec((1, TILE, d), lambda bi, i: (bi, jnp.maximum(i - 1, 0), 0)),
            pl.BlockSpec((TILE, d), lambda bi, i: (0, 0)),
            pl.BlockSpec((1, d), lambda bi, i: (0, 0)),
        ],
        out_specs=pl.BlockSpec((1, TILE, d), lambda bi, i: (bi, i, 0)),
        compiler_params=_cparams(("parallel", "parallel")),
        name="norm1",
    )(x, head, g.reshape(1, d))


def _rope_rows(r, cos_t, sin_a, sin_b):
    return r * cos_t + pltpu.roll(r, 96, 1) * sin_a + pltpu.roll(r, 32, 1) * sin_b


def _proj_scale_kernel(x_ref, w_ref, s_ref, o_ref):
    acc = jnp.dot(x_ref[...], w_ref[...], preferred_element_type=F32)
    o_ref[...] = (acc * s_ref[...]).astype(o_ref.dtype)


def _proj_gate_kernel(x_ref, w_ref, b_ref, o_ref):
    acc = jnp.dot(x_ref[...], w_ref[...], preferred_element_type=F32) + b_ref[...]
    o_ref[...] = 1.0 / (1.0 + jnp.exp(-acc))


def _proj_mlaq_kernel(x_ref, wn_ref, wr_ref, cos_ref, sa_ref, sb_ref, o_ref, *, scale):
    tm = x_ref.shape[0]
    parts = 3 if tm % 48 == 0 else 1
    rows = tm // parts
    low = lax.broadcasted_iota(jnp.int32, (rows, LANES), 1) < MLA_ROPE_DIM
    for part in range(parts):
        rs = slice(part * rows, (part + 1) * rows)
        x = x_ref[rs, :]
        acc_n = jnp.dot(x, wn_ref[...], preferred_element_type=F32)
        acc_r = jnp.dot(x, wr_ref[...], preferred_element_type=F32)
        cos_t, sin_a, sin_b = cos_ref[rs, :], sa_ref[rs, :], sb_ref[rs, :]
        for hh in range(acc_n.shape[1] // MLA_NOPE_DIM):
            pair = acc_r[:, (hh // 2) * LANES:(hh // 2 + 1) * LANES]
            if hh % 2:
                pair = pltpu.roll(pair, MLA_ROPE_DIM, 1)
            rope = _rope_rows(jnp.where(low, pair, 0.0), cos_t, sin_a, sin_b) * scale
            c0 = hh * MLA_QK_PAD
            o_ref[rs, c0:c0 + LANES] = (acc_n[:, hh * LANES:(hh + 1) * LANES] * scale).astype(o_ref.dtype)
            o_ref[rs, c0 + LANES:c0 + 2 * LANES] = rope.astype(o_ref.dtype)


def _proj_mlaq(hn2d, wn, wr, tables, tm, tiles_per_batch, group, scale):
    m, k = hn2d.shape
    heads = wn.shape[1] // MLA_NOPE_DIM
    tab_spec = pl.BlockSpec((tm, LANES), lambda j, i: (i % tiles_per_batch, 0))
    return pl.pallas_call(
        functools.partial(_proj_mlaq_kernel, scale=scale),
        out_shape=jax.ShapeDtypeStruct((m, heads * MLA_QK_PAD), BF16),
        grid=(heads // group, m // tm),
        in_specs=[
            pl.BlockSpec((tm, k), lambda j, i: (i, 0)),
            pl.BlockSpec((k, group * MLA_NOPE_DIM), lambda j, i: (0, j)),
            pl.BlockSpec((k, group * MLA_ROPE_DIM), lambda j, i: (0, j)),
            tab_spec, tab_spec, tab_spec,
        ],
        out_specs=pl.BlockSpec((tm, group * MLA_QK_PAD), lambda j, i: (i, j)),
        compiler_params=_cparams(("parallel", "parallel")),
        name="proj_mlaq",
    )(hn2d, wn, wr, *tables)


def _store_lane_tiles(o_ref, val_t):
    for c in range(val_t.shape[1] // TILE):
        o_ref[c] = val_t[:, c * TILE:(c + 1) * TILE].astype(o_ref.dtype)


def _proj_t_kernel(x_ref, wt_ref, o_ref):
    acc_t = lax.dot_general(wt_ref[...], x_ref[...], (((1,), (1,)), ((), ())), preferred_element_type=F32)
    _store_lane_tiles(o_ref, acc_t)


def _proj_t(hn2d, wt, tm):
    m, k = hn2d.shape
    n = wt.shape[0]
    return pl.pallas_call(
        _proj_t_kernel,
        out_shape=jax.ShapeDtypeStruct((m // TILE, n, TILE), BF16),
        grid=(m // tm,),
        in_specs=[pl.BlockSpec((tm, k), lambda i: (i, 0)), pl.BlockSpec((n, k), lambda i: (0, 0))],
        out_specs=pl.BlockSpec((tm // TILE, n, TILE), lambda i: (i, 0, 0)),
        compiler_params=_cparams(("parallel",)),
        name="proj_t",
    )(hn2d, wt)


def _proj(kernel, hn2d, w, extras, extra_specs, out_dtype, tm, tn):
    m, k = hn2d.shape
    n = w.shape[1]
    return pl.pallas_call(
        kernel,
        out_shape=jax.ShapeDtypeStruct((m, n), out_dtype),
        grid=(n // tn, m // tm),
        in_specs=[
            pl.BlockSpec((tm, k), lambda j, i: (i, 0)),
            pl.BlockSpec((k, tn), lambda j, i: (0, j)),
        ] + extra_specs,
        out_specs=pl.BlockSpec((tm, tn), lambda j, i: (i, j)),
        compiler_params=_cparams(("parallel", "parallel")),
        name=getattr(kernel, "__name__", None) or kernel.func.__name__,
    )(hn2d, w, *extras)


def _kvup_kernel(c_ref, g_ref, wk_ref, wvt_ref, cos_ref, sa_ref, sb_ref, k_ref, vt_ref, *, rank):
    ckr = c_ref[...]
    cn = _rms(ckr[:, :rank], g_ref[...]).astype(BF16)
    kn = jnp.dot(cn, wk_ref[...], preferred_element_type=F32)
    vv_t = lax.dot_general(wvt_ref[...], cn, (((1,), (1,)), ((), ())), preferred_element_type=F32)
    rope = _rope_rows(ckr[:, rank:rank + LANES], cos_ref[...], sa_ref[...], sb_ref[...]).astype(BF16)
    for h in range(kn.shape[1] // MLA_NOPE_DIM):
        k_ref[:, h * MLA_QK_PAD:h * MLA_QK_PAD + LANES] = kn[:, h * LANES:(h + 1) * LANES].astype(BF16)
        k_ref[:, h * MLA_QK_PAD + LANES:(h + 1) * MLA_QK_PAD] = rope
    _store_lane_tiles(vt_ref, vv_t)


def _kvup(ckr, g, wk, wvt, tables, tm, tiles_per_batch):
    m, cw = ckr.shape
    rank = wk.shape[0]
    nk = wk.shape[1]
    nv = wvt.shape[0]
    heads = nk // MLA_NOPE_DIM
    tab_spec = pl.BlockSpec((tm, LANES), lambda i: (i % tiles_per_batch, 0))
    return pl.pallas_call(
        functools.partial(_kvup_kernel, rank=rank),
        out_shape=(jax.ShapeDtypeStruct((m, heads * MLA_QK_PAD), BF16),
                   jax.ShapeDtypeStruct((m // TILE, nv, TILE), BF16)),
        grid=(m // tm,),
        in_specs=[
            pl.BlockSpec((tm, cw), lambda i: (i, 0)),
            pl.BlockSpec((1, rank), lambda i: (0, 0)),
            pl.BlockSpec(wk.shape, lambda i: (0, 0)),
            pl.BlockSpec(wvt.shape, lambda i: (0, 0)),
            tab_spec, tab_spec, tab_spec,
        ],
        out_specs=(pl.BlockSpec((tm, heads * MLA_QK_PAD), lambda i: (i, 0)),
                   pl.BlockSpec((tm // TILE, nv, TILE), lambda i: (i, 0, 0))),
        compiler_params=_cparams(("parallel",)),
        name="kvup",
    )(ckr, g.reshape(1, rank), wk, wvt, *tables)


def _softplus2(z):
    neg_abs = pltpu.bitcast(pltpu.bitcast(z, jnp.uint32) | jnp.uint32(0x80000000), F32)
    return jnp.maximum(z, 0.0) + jnp.log2(1.0 + jnp.exp2(neg_abs))


def _sb_kernel(*refs, nsub, first_valid):
    q_refs = refs[:nsub]
    k_ref, vt_ref, o_ref, carry_ref, acc_ref = refs[nsub:]
    qs = pl.program_id(2)
    first_tile = 1 + nsub * qs
    carry_ref[...] = jnp.zeros_like(carry_ref)
    acc_ref[...] = jnp.zeros_like(acc_ref)
    trow = lax.broadcasted_iota(jnp.int32, (TILE, 2 * TILE), 0)
    tcol = lax.broadcasted_iota(jnp.int32, (TILE, 2 * TILE), 1) & (TILE - 1)
    tri2 = jnp.where(tcol >= trow, 1.0, 0.0).astype(BF16)

    def update(c0, c1, j, mask):
        lanes = slice(c0 * TILE, c1 * TILE)
        q = jnp.concatenate([q_refs[c][0] for c in range(c0, c1)], axis=0)
        kb = k_ref[0, pl.ds(pl.multiple_of(j * TILE, TILE), TILE), :]
        z = lax.dot_general(kb, q, (((1,), (1,)), ((), ())), preferred_element_type=F32)
        sp = _softplus2(z)
        if mask is not None:
            sp = jnp.where(mask, sp, 0.0)
        hi32 = pltpu.bitcast(pltpu.bitcast(sp, jnp.uint32) & jnp.uint32(0xFFFF0000), F32)
        parts = jnp.concatenate([hi32.astype(BF16), (sp - hi32).astype(BF16)], axis=0)
        cs = jnp.dot(tri2, parts, preferred_element_type=F32)
        carry = carry_ref[:, lanes]
        a = jnp.exp2(z - cs - carry)
        if mask is not None:
            a = jnp.where(mask, a, 0.0)
        acc_ref[:, lanes] = acc_ref[:, lanes] + jnp.dot(vt_ref[j], a.astype(BF16), preferred_element_type=F32)
        carry_ref[:, lanes] = carry + cs[0:1, :]

    def stick_left(c0, c1):
        return jnp.min(carry_ref[:, c0 * TILE:c1 * TILE]) < STICK_GONE_LOG2

    def diag_mask(nq):
        key = lax.broadcasted_iota(jnp.int32, (TILE, nq * TILE), 0)
        qry = lax.broadcasted_iota(jnp.int32, (TILE, nq * TILE), 1)
        return (qry >= TILE) | (key < qry)

    half = nsub // 2
    for c in reversed(range(nsub)):
        near = min(c + half, nsub)
        update(c, near, first_tile + c, diag_mask(near - c))
        if near < nsub:
            @pl.when(stick_left(near, nsub))
            def _():
                update(near, nsub, first_tile + c, None)

    for c0, c1 in ((0, half), (half, nsub)):
        def body(state):
            j, _ = state
            update(c0, c1, j, None)
            return j - 1, stick_left(c0, c1)

        _, alive = lax.while_loop(lambda st: (st[0] > 0) & st[1], body, (first_tile - 1, stick_left(c0, c1)))

        @pl.when(alive)
        def _():
            key0 = lax.broadcasted_iota(jnp.int32, (TILE, (c1 - c0) * TILE), 0)
            update(c0, c1, 0, key0 >= first_valid)

    o_ref[0] = acc_ref[...].T.astype(o_ref.dtype)


def _q_specs(nsub, width, col0):
    return [pl.BlockSpec((1, TILE, width),
                         functools.partial(lambda bi, h, i, c: (bi, nsub * i + 1 + c, col0 + h), c=c))
            for c in range(nsub)]


def _sb_attention(qk, vt, seq):
    b, lp, w2 = qk.shape
    heads = w2 // (2 * SB_HEAD_DIM)
    nsub = ATTN_Q_TILES
    rows = nsub * TILE
    return pl.pallas_call(
        functools.partial(_sb_kernel, nsub=nsub, first_valid=TILE - N_META),
        out_shape=jax.ShapeDtypeStruct((b, seq, heads * SB_HEAD_DIM), BF16),
        grid=(b, heads, seq // rows),
        in_specs=_q_specs(nsub, SB_HEAD_DIM, 0) + [
            pl.BlockSpec((1, lp, SB_HEAD_DIM), lambda bi, h, i: (bi, 0, heads + h)),
            pl.BlockSpec((lp // TILE, SB_HEAD_DIM, TILE), lambda bi, h, i: (bi, h, 0)),
        ],
        out_specs=pl.BlockSpec((1, rows, SB_HEAD_DIM), lambda bi, h, i: (bi, i, h)),
        scratch_shapes=[pltpu.VMEM((1, rows), F32), pltpu.VMEM((SB_HEAD_DIM, rows), F32)],
        compiler_params=_cparams(("parallel", "parallel", "arbitrary")),
        name="sb_attention",
    )(*([qk] * (nsub + 1)), vt)


def _mla_kernel(*refs, nsub, nheads, first_valid):
    nq = nsub * nheads
    q_refs = refs[:nq]
    k_ref, vt_ref, bias_ref, o_ref = refs[nq:nq + 4]
    scratch = refs[nq + 4:]
    qs = pl.program_id(2)
    rows = nsub * TILE
    blk_keys = 2 * TILE
    n_full = (nsub * qs) // 2

    lower = slice(0, rows // 2)
    upper = slice(rows // 2, rows)

    def with_ones(vtb):
        return jnp.concatenate([vtb, jnp.ones((DENOM_ROWS, vtb.shape[1]), BF16)], axis=0)

    class Head:
        def __init__(self, g):
            self.qcols = slice(g * MLA_QK_PAD, (g + 1) * MLA_QK_PAD)
            self.vrows = slice(g * MLA_V_DIM, (g + 1) * MLA_V_DIM)
            self.q_refs = q_refs[g * nsub:(g + 1) * nsub]
            (self.m_ref, self.acc_ref, s0, s1, p0, p1, x0, x1) = scratch[g * 8:(g + 1) * 8]
            self.even = (s0, x0, p0)
            self.odd = (s1, x1, p1)

        def queries(self, lanes=slice(None)):
            tiles = self.q_refs[lanes.start // TILE:] if lanes.start else self.q_refs
            return jnp.concatenate([r[0] for r in tiles], axis=0)

        def vt(self, j):
            return vt_ref[j, self.vrows, :]

        def scores(self, blk, s_ref, smax_ref, lanes=slice(None)):
            kb = k_ref[0, pl.ds(pl.multiple_of((1 + 2 * blk) * TILE, TILE), blk_keys), self.qcols]
            s = lax.dot_general(kb, self.queries(lanes), (((1,), (1,)), ((), ())), preferred_element_type=F32)
            s_ref[:, lanes] = s
            smax_ref[:, lanes] = jnp.max(s, axis=0, keepdims=True)

        def softmax(self, s_ref, smax_ref, p_ref, bias, lanes=slice(None)):
            if bias is None:
                s = s_ref[:, lanes]
                smax = smax_ref[:, lanes]
            else:
                s = s_ref[:, lanes] + bias
                smax = jnp.max(s, axis=0, keepdims=True)
            m_old = self.m_ref[:, lanes]
            m_new = jnp.maximum(m_old, smax)
            alpha = jnp.exp2(m_old - m_new)
            p = jnp.exp2(s - m_new)
            self.m_ref[:, lanes] = m_new
            p_ref[:, lanes] = p.astype(BF16)
            return alpha

        def values(self, blk, p_ref, lanes=slice(None)):
            j = 1 + 2 * blk
            vtb = with_ones(jnp.concatenate([self.vt(j), self.vt(j + 1)], axis=1))
            return jnp.dot(vtb, p_ref[:, lanes], preferred_element_type=F32)

        def stage(self, blk, cur, nxt, bias=None, ahead=slice(None)):
            (s_cur, x_cur, p_cur), (s_nxt, x_nxt, p_prev) = cur, nxt
            self.scores(blk + 1, s_nxt, x_nxt, ahead)
            alpha = self.softmax(s_cur, x_cur, p_cur, bias)
            self.acc_ref[...] = alpha * (self.acc_ref[...] + self.values(jnp.maximum(blk - 1, 0), p_prev))

        def last_stage(self, blk, cur, nxt, bias):
            (s_cur, x_cur, p_cur), (_, _, p_prev) = cur, nxt
            alpha = self.softmax(s_cur, x_cur, p_cur, bias[:, upper], upper)
            pv = self.values(blk - 1, p_prev)
            self.acc_ref[:, lower] = self.acc_ref[:, lower] + pv[:, lower]
            self.acc_ref[:, upper] = alpha * (self.acc_ref[:, upper] + pv[:, upper])

        def start(self):
            s = lax.dot_general(k_ref[0, first_valid:TILE, self.qcols], self.queries(),
                                (((1,), (1,)), ((), ())), preferred_element_type=F32)
            m0 = jnp.max(s, axis=0, keepdims=True)
            p = jnp.exp2(s - m0)
            self.m_ref[...] = m0
            p_tile = jnp.concatenate([jnp.zeros((first_valid, rows), BF16), p.astype(BF16)], axis=0)
            self.acc_ref[...] = jnp.dot(with_ones(self.vt(0)), p_tile, preferred_element_type=F32)
            self.scores(0, self.even[0], self.even[1])
            self.odd[2][...] = jnp.zeros_like(self.odd[2])

        def finish(self):
            acc = jnp.concatenate([self.acc_ref[:, lower],
                                   self.acc_ref[:, upper] + self.values(n_full + 1, self.odd[2], upper)], axis=1)
            out = (acc[:MLA_V_DIM] / acc[MLA_V_DIM:MLA_V_DIM + 1]).T
            o_ref[0, :, self.vrows] = out.astype(o_ref.dtype)

    heads = [Head(g) for g in range(nheads)]
    for hd in heads:
        hd.start()

    def body(u, carry):
        for hd in heads:
            hd.stage(2 * u, hd.even, hd.odd)
        for hd in heads:
            hd.stage(2 * u + 1, hd.odd, hd.even)
        return carry

    lax.fori_loop(0, n_full // 2, body, 0)
    for hd in heads:
        hd.stage(n_full, hd.even, hd.odd, bias=bias_ref[:blk_keys, :], ahead=upper)
    for hd in heads:
        hd.last_stage(n_full + 1, hd.odd, hd.even, bias_ref[blk_keys:, :])
    for hd in heads:
        hd.finish()


def _mla_attention(q, k, vt, seq):
    b, lp, _ = q.shape
    heads = vt.shape[1] // MLA_V_DIM
    nsub = ATTN_Q_TILES
    nheads = min(MLA_HEADS_PER_STEP, heads)
    assert nsub == 4, "the kernel visits the query-overlapping keys as exactly two 2-tile blocks"
    assert heads % nheads == 0
    rows = nsub * TILE
    shift = CHUNK.bit_length() - 1
    key = lax.broadcasted_iota(jnp.int32, (rows, rows), 0)
    qry = lax.broadcasted_iota(jnp.int32, (rows, rows), 1)
    bias = jnp.where((key >> shift) <= (qry >> shift), 0.0, NEG_BIG).astype(F32)
    q_specs = [pl.BlockSpec((1, TILE, MLA_QK_PAD),
                            functools.partial(lambda bi, h, i, g, c: (bi, nsub * i + 1 + c, nheads * h + g), g=g, c=c))
               for g in range(nheads) for c in range(nsub)]
    per_head_scratch = [pltpu.VMEM((1, rows), F32),
                        pltpu.VMEM((MLA_V_DIM + DENOM_ROWS, rows), F32),
                        pltpu.VMEM((2 * TILE, rows), F32), pltpu.VMEM((2 * TILE, rows), F32),
                        pltpu.VMEM((2 * TILE, rows), BF16), pltpu.VMEM((2 * TILE, rows), BF16),
                        pltpu.VMEM((1, rows), F32), pltpu.VMEM((1, rows), F32)]
    return pl.pallas_call(
        functools.partial(_mla_kernel, nsub=nsub, nheads=nheads, first_valid=TILE - N_META),
        out_shape=jax.ShapeDtypeStruct((b, seq, heads * MLA_V_DIM), BF16),
        grid=(b, heads // nheads, seq // rows),
        in_specs=q_specs + [
            pl.BlockSpec((1, lp, nheads * MLA_QK_PAD), lambda bi, h, i: (bi, 0, h)),
            pl.BlockSpec((lp // TILE, nheads * MLA_V_DIM, TILE), lambda bi, h, i: (bi, h, 0)),
            pl.BlockSpec((rows, rows), lambda bi, h, i: (0, 0), pipeline_mode=pl.Buffered(1)),
        ],
        out_specs=pl.BlockSpec((1, rows, nheads * MLA_V_DIM), lambda bi, h, i: (bi, i, h)),
        scratch_shapes=per_head_scratch * nheads,
        compiler_params=_cparams(("parallel", "parallel", "arbitrary")),
        name="mla_attention",
    )(*([q] * (nsub * nheads)), k, vt, bias)


def _merge_kernel(oa_ref, ob_ref, ga_ref, gb_ref, x_ref, wpa_ref, wpb_ref, wo_ref, o_ref):
    pa = jnp.dot(oa_ref[0], wpa_ref[...], preferred_element_type=F32)
    pb = jnp.dot(ob_ref[0], wpb_ref[...], preferred_element_type=F32)
    y = ga_ref[0] * pa + gb_ref[0] * pb
    o_ref[0] = x_ref[0] + jnp.dot(y.astype(BF16), wo_ref[...], preferred_element_type=F32)


def _merge(oa, ob, gates, x, wpa, wpb, wo):
    b, s, d = x.shape
    nt = s // TILE
    resident = lambda w: pl.BlockSpec(w.shape, lambda bi, i: (0, 0), pipeline_mode=pl.Buffered(1))
    return pl.pallas_call(
        _merge_kernel,
        out_shape=jax.ShapeDtypeStruct((b, s, d), F32),
        grid=(b, nt),
        in_specs=[
            pl.BlockSpec((1, TILE, oa.shape[2]), lambda bi, i: (bi, i, 0)),
            pl.BlockSpec((1, TILE, ob.shape[2]), lambda bi, i: (bi, i, 0)),
            pl.BlockSpec((1, TILE, d), lambda bi, i: (bi, i + 1, 0)),
            pl.BlockSpec((1, TILE, d), lambda bi, i: (bi, i + 1, 1)),
            pl.BlockSpec((1, TILE, d), lambda bi, i: (bi, i, 0)),
            resident(wpa), resident(wpb), resident(wo),
        ],
        out_specs=pl.BlockSpec((1, TILE, d), lambda bi, i: (bi, i, 0)),
        compiler_params=_cparams(("parallel", "parallel")),
        name="merge",
    )(oa, ob, gates, gates, x, wpa, wpb, wo)


def _split2(a):
    hi = a.astype(BF16)
    return hi, (a - hi.astype(F32)).astype(BF16)


def _route_kernel(h_ref, g_ref, wr_ref, br_ref, hp_ref, r_ref, cnt_ref, carry_ref, *, tm, half):
    i = pl.program_id(0)

    @pl.when(i == 0)
    def _():
        carry_ref[...] = jnp.zeros_like(carry_ref)

    hn = _rms(h_ref[...], g_ref[...])

    lo_bits = pltpu.bitcast(hn[:, :half].astype(BF16).astype(F32), jnp.uint32)
    hi_bits = pltpu.bitcast(hn[:, half:].astype(BF16).astype(F32), jnp.uint32)
    hp_ref[...] = (hi_bits & jnp.uint32(0xFFFF0000)) | (lo_bits >> 16)

    a_hi, a_mid = _split2(hn)
    w_hi, w_mid = wr_ref[0], wr_ref[1]
    dot = lambda a, w: jnp.dot(a, w, preferred_element_type=F32)
    lg = (dot(a_mid, w_hi) + dot(a_hi, w_mid) + dot(a_hi, w_hi)) + br_ref[...]

    lane = lax.broadcasted_iota(jnp.int32, lg.shape, 1)
    rmax = lambda v: jnp.max(v, axis=1, keepdims=True)
    rmin = lambda v: jnp.min(v, axis=1, keepdims=True)
    rsum = lambda v: jnp.sum(v, axis=1, keepdims=True)

    gmask = lane < N_GROUPS
    gl = jnp.where(gmask, lg, -jnp.inf)
    gmax = rmax(gl)
    gsel = rmin(jnp.where(gl == gmax, lane, LANES))
    p_g = 1.0 / rsum(jnp.where(gmask, jnp.exp(lg - gmax), 0.0))

    e_lo = EXPERT_LANE0 + gsel * EXPERTS_PER_GROUP
    emask = (lane >= e_lo) & (lane < e_lo + EXPERTS_PER_GROUP)
    emax = rmax(jnp.where(emask, lg, -jnp.inf))
    ex = jnp.where(emask, jnp.exp(lg - emax), 0.0)
    prob = jnp.where(emask, ex / rsum(ex), -1.0)
    top1 = rmax(prob)
    i1 = rmin(jnp.where(prob == top1, lane, LANES))
    prob2 = jnp.where(lane == i1, -1.0, prob)
    top2 = rmax(prob2)
    i2 = rmin(jnp.where(prob2 == top2, lane, LANES))
    denom = top1 + top2
    w1 = p_g * top1 / denom
    w2 = p_g * top2 / denom

    sel = ((lane == i1) | (lane == i2))
    row = lax.broadcasted_iota(jnp.int32, (tm, tm), 0)
    col = lax.broadcasted_iota(jnp.int32, (tm, tm), 1)
    before = (col < row).astype(BF16)
    prefix = dot(before, jnp.where(sel, 1.0, 0.0).astype(BF16)) + carry_ref[...]
    rank1 = rsum(jnp.where(lane == i1, prefix, 0.0))
    rank2 = rsum(jnp.where(lane == i2, prefix, 0.0))
    carry_ref[...] = carry_ref[...] + jnp.sum(jnp.where(sel, 1.0, 0.0), axis=0, keepdims=True)
    cnt_ref[...] = carry_ref[...]

    e1 = (i1 - EXPERT_LANE0).astype(F32)
    e2 = (i2 - EXPERT_LANE0).astype(F32)
    out = jnp.zeros(lg.shape, F32)
    for k, val in enumerate((e1, e2, w1, w2, rank1, rank2)):
        out = jnp.where(lane == k, val, out)
    r_ref[...] = out


def _route(h1, g, wr3, br):
    n, d = h1.shape
    tm = TILE
    return pl.pallas_call(
        functools.partial(_route_kernel, tm=tm, half=d // 2),
        out_shape=(jax.ShapeDtypeStruct((n, d // 2), jnp.uint32),
                   jax.ShapeDtypeStruct((n, LANES), F32),
                   jax.ShapeDtypeStruct((1, LANES), F32)),
        grid=(n // tm,),
        in_specs=[
            pl.BlockSpec((tm, d), lambda i: (i, 0)),
            pl.BlockSpec((1, d), lambda i: (0, 0)),
            pl.BlockSpec(wr3.shape, lambda i: (0, 0, 0)),
            pl.BlockSpec((1, LANES), lambda i: (0, 0)),
        ],
        out_specs=(pl.BlockSpec((tm, d // 2), lambda i: (i, 0)),
                   pl.BlockSpec((tm, LANES), lambda i: (i, 0)),
                   pl.BlockSpec((1, LANES), lambda i: (0, 0))),
        scratch_shapes=[pltpu.VMEM((1, LANES), F32)],
        compiler_params=_cparams(("arbitrary",)),
        name="route",
    )(h1, g.reshape(1, d), wr3, br)


def _dispatch_kernel(d0_ref, d1_ref, src_ref, init_ref, dst_ref, sem, *, tm):
    del init_ref
    base = pl.program_id(0) * tm

    def copy(r, dest_ref, s):
        return pltpu.make_async_copy(src_ref.at[pl.ds(r, 1)], dst_ref.at[pl.ds(dest_ref[base + r], 1)], sem.at[s])

    def start(r, c):
        copy(r, d0_ref, 0).start()
        copy(r, d1_ref, 1).start()
        return c

    lax.fori_loop(0, tm, start, 0, unroll=DMA_ISSUE_UNROLL)
    for s in range(2):
        pltpu.make_async_copy(src_ref, dst_ref.at[pl.ds(0, tm)], sem.at[s]).wait()


def _dispatch(dest0, dest1, hp, p_rows):
    n, w = hp.shape
    tm = TILE
    init = jnp.zeros((p_rows, w), hp.dtype)
    return pl.pallas_call(
        functools.partial(_dispatch_kernel, tm=tm),
        out_shape=jax.ShapeDtypeStruct((p_rows, w), hp.dtype),
        grid_spec=pltpu.PrefetchScalarGridSpec(
            num_scalar_prefetch=2,
            grid=(n // tm,),
            in_specs=[pl.BlockSpec((tm, w), lambda i, a, b: (i, 0)), pl.BlockSpec(memory_space=pl.ANY)],
            out_specs=pl.BlockSpec(memory_space=pl.ANY),
            scratch_shapes=[pltpu.SemaphoreType.DMA((2,))],
        ),
        input_output_aliases={3: 0},
        compiler_params=_cparams(("arbitrary",)),
        name="dispatch",
    )(dest0, dest1, hp, init)


def _expert_kernel(nu_ref, fe_ref, fh_ref, new_ref, cslot_ref, uslot_ref, full_ref,
                   x_ref, w1_ref, w3_ref, w2_ref, y_ref, c1_ref, c3_ref, c2_ref):
    del fe_ref
    j = pl.program_id(0)
    h = pl.program_id(1)
    t = j * EXPERT_FF_SPLIT + h
    live = j < nu_ref[0]

    @pl.when(new_ref[t] == 1)
    def _():
        dst = cslot_ref[t] * EXPERT_FF_SPLIT + fh_ref[t]
        c1_ref[dst] = w1_ref[0].astype(BF16)
        c3_ref[dst] = w3_ref[0].astype(BF16)
        c2_ref[dst] = w2_ref[0].astype(BF16)

    def rows():
        xw = x_ref[...]
        lo = pltpu.bitcast(xw << 16, F32).astype(BF16)
        hi = pltpu.bitcast(xw & jnp.uint32(0xFFFF0000), F32).astype(BF16)
        return jnp.concatenate([lo, hi], axis=1)

    def part(xb, hh):
        src = uslot_ref[j] * EXPERT_FF_SPLIT + hh
        a = jnp.dot(xb, c1_ref[src], preferred_element_type=F32)
        g = jnp.dot(xb, c3_ref[src], preferred_element_type=F32)
        hid = (a * (1.0 / (1.0 + jnp.exp(-a))) * g).astype(BF16)
        return jnp.dot(hid, c2_ref[src], preferred_element_type=F32)

    whole = full_ref[j] == 1

    @pl.when(live & whole & (h == 0))
    def _():
        xb = rows()
        acc = part(xb, 0)
        for hh in range(1, EXPERT_FF_SPLIT):
            acc = acc + part(xb, hh)
        y_ref[...] = acc

    @pl.when(live & jnp.logical_not(whole))
    def _():
        contrib = part(rows(), h)

        @pl.when(h == 0)
        def _():
            y_ref[...] = contrib

        @pl.when(h > 0)
        def _():
            y_ref[...] = y_ref[...] + contrib

    @pl.when(jnp.logical_not(live) & (h == 0))
    def _():
        y_ref[...] = jnp.zeros_like(y_ref)


def _expert_stream_plan(block_e, n_used, n_exp):
    nb = block_e.shape[0]
    nh = EXPERT_FF_SPLIT
    live = jnp.arange(nb) < n_used[0]
    first = jnp.concatenate([jnp.ones((1,), bool), block_e[1:] != block_e[:-1]]) & live
    seg = jnp.cumsum(first.astype(jnp.int32)) - 1
    n_seg = jnp.sum(first.astype(jnp.int32))
    seg_ids = jnp.arange(n_exp, dtype=jnp.int32)
    seg_expert = jnp.sum(jnp.where(first[None, :] & (seg[None, :] == seg_ids[:, None]), block_e[None, :], 0), axis=1)
    cap = nh * n_seg - 1

    def step(carry, t):
        pos, prev_busy, whole = carry
        j, h = t // nh, t % nh
        lv, k = live[j], seg[j]
        limit = jnp.where(lv, jnp.minimum(nh * k + 2 * nh - 1, cap), cap)
        must = lv & jnp.where(h == 0, pos < nh * k, jnp.logical_not(whole) & (pos < nh * k + h))
        inc = (pos < limit) & (prev_busy | must)
        pos = pos + inc.astype(jnp.int32)
        whole = jnp.where(h == 0, lv & (pos >= nh * k + nh - 1), whole)
        busy = lv & ((h == 0) | jnp.logical_not(whole))
        return (pos, busy, whole), (pos, inc.astype(jnp.int32), whole.astype(jnp.int32))

    init = (jnp.int32(-1), jnp.bool_(False), jnp.bool_(False))
    _, (pos, new, whole_t) = lax.scan(step, init, jnp.arange(nb * nh, dtype=jnp.int32))
    item_seg = pos // nh
    fetch_e = jnp.sum(jnp.where(item_seg[:, None] == seg_ids[None, :], seg_expert[None, :], 0), axis=1)
    return (fetch_e.astype(jnp.int32), (pos % nh).astype(jnp.int32), new,
            (item_seg % 2).astype(jnp.int32), (seg % 2).astype(jnp.int32), whole_t[::nh])


def _experts(block_e, n_used, xs, w1, w3, w2):
    p_rows, half = xs.shape
    n_exp, d, ff = w1.shape
    nb = p_rows // ROUTE_BLOCK
    nh = EXPERT_FF_SPLIT
    fh = ff // nh
    plan = _expert_stream_plan(block_e, n_used, n_exp)
    last = lambda j, nu: jnp.minimum(j, nu[0] - 1)
    step = lambda j, h: j * nh + h
    return pl.pallas_call(
        _expert_kernel,
        out_shape=jax.ShapeDtypeStruct((p_rows, d), F32),
        grid_spec=pltpu.PrefetchScalarGridSpec(
            num_scalar_prefetch=1 + len(plan),
            grid=(nb, nh),
            in_specs=[
                pl.BlockSpec((ROUTE_BLOCK, half), lambda j, h, nu, fe, fhh, *_: (last(j, nu), 0)),
                pl.BlockSpec((1, d, fh), lambda j, h, nu, fe, fhh, *_: (fe[step(j, h)], 0, fhh[step(j, h)])),
                pl.BlockSpec((1, d, fh), lambda j, h, nu, fe, fhh, *_: (fe[step(j, h)], 0, fhh[step(j, h)])),
                pl.BlockSpec((1, fh, d), lambda j, h, nu, fe, fhh, *_: (fe[step(j, h)], fhh[step(j, h)], 0)),
            ],
            out_specs=pl.BlockSpec((ROUTE_BLOCK, d), lambda j, h, *_: (j, 0)),
            scratch_shapes=[pltpu.VMEM((2 * nh, d, fh), BF16), pltpu.VMEM((2 * nh, d, fh), BF16),
                            pltpu.VMEM((2 * nh, fh, d), BF16)],
        ),
        compiler_params=pltpu.CompilerParams(dimension_semantics=("arbitrary", "arbitrary"),
                                             vmem_limit_bytes=EXPERT_VMEM_LIMIT),
        name="experts",
    )(n_used, *plan, xs, w1, w3, w2)


def _combine_kernel(d0_ref, d1_ref, h_ref, r_ref, g_ref, y_ref, o_ref, ya, yb, sem, *, tm):
    i = pl.program_id(0)
    last = pl.num_programs(0) - 1
    slot = i % 2
    ahead = jnp.minimum(i + 1, last)

    def issue(tile, sl, r):
        row = tile * tm + r
        pltpu.make_async_copy(y_ref.at[pl.ds(d0_ref[row], 1)], ya.at[sl, pl.ds(r, 1)], sem.at[sl, 0]).start()
        pltpu.make_async_copy(y_ref.at[pl.ds(d1_ref[row], 1)], yb.at[sl, pl.ds(r, 1)], sem.at[sl, 1]).start()

    def wait(sl):
        pltpu.make_async_copy(y_ref.at[pl.ds(0, tm)], ya.at[sl], sem.at[sl, 0]).wait()
        pltpu.make_async_copy(y_ref.at[pl.ds(0, tm)], yb.at[sl], sem.at[sl, 1]).wait()

    @pl.when(i == 0)
    def _():
        def first(r, c):
            issue(0, 0, r)
            return c
        lax.fori_loop(0, tm, first, 0, unroll=DMA_ISSUE_UNROLL)

    wait(slot)
    lane = lax.broadcasted_iota(jnp.int32, (COMBINE_CHUNK, LANES), 1)

    def chunk(c, carry):
        r0 = pl.multiple_of(c * COMBINE_CHUNK, COMBINE_CHUNK)
        for k in range(COMBINE_CHUNK):
            issue(ahead, 1 - slot, r0 + k)
        rows = pl.ds(r0, COMBINE_CHUNK)
        route = r_ref[rows, :]
        w1 = jnp.sum(jnp.where(lane == 2, route, 0.0), axis=1, keepdims=True)
        w2 = jnp.sum(jnp.where(lane == 3, route, 0.0), axis=1, keepdims=True)
        h2 = h_ref[rows, :] + (ya[slot, rows, :] * w1 + yb[slot, rows, :] * w2)
        o_ref[rows, :] = _rms(h2, g_ref[...])
        return carry

    lax.fori_loop(0, tm // COMBINE_CHUNK, chunk, 0)

    @pl.when(i == last)
    def _():
        wait(1 - slot)


def _combine(dest0, dest1, h1, route, g, y):
    n, d = h1.shape
    tm = TILE
    return pl.pallas_call(
        functools.partial(_combine_kernel, tm=tm),
        out_shape=jax.ShapeDtypeStruct((n, d), F32),
        grid_spec=pltpu.PrefetchScalarGridSpec(
            num_scalar_prefetch=2,
            grid=(n // tm,),
            in_specs=[
                pl.BlockSpec((tm, d), lambda i, a, b: (i, 0)),
                pl.BlockSpec((tm, LANES), lambda i, a, b: (i, 0)),
                pl.BlockSpec((1, d), lambda i, a, b: (0, 0)),
                pl.BlockSpec(memory_space=pl.ANY),
            ],
            out_specs=pl.BlockSpec((tm, d), lambda i, a, b: (i, 0)),
            scratch_shapes=[pltpu.VMEM((2, tm, d), F32), pltpu.VMEM((2, tm, d), F32),
                            pltpu.SemaphoreType.DMA((2, 2))],
        ),
        compiler_params=_cparams(("arbitrary",)),
        name="combine",
    )(dest0, dest1, h1, route, g.reshape(1, d), y)


def _rope_tables(lp):
    half = MLA_ROPE_DIM // 2
    inv = ROPE_THETA ** (-jnp.arange(half, dtype=F32) / half)
    pos = (jnp.arange(lp) - (TILE - N_META)).astype(F32)
    ang = pos[:, None] * inv[None, :]
    cos, sin = jnp.cos(ang), jnp.sin(ang)
    z32 = jnp.zeros((lp, half), F32)
    z64 = jnp.zeros((lp, LANES - MLA_ROPE_DIM), F32)
    return (jnp.concatenate([cos, cos, z64], axis=1),
            jnp.concatenate([-sin, z32, z64], axis=1),
            jnp.concatenate([z32, sin, z64], axis=1))


def kernel(x, meta_tokens, norm1_g, w_in, b_gate, kv_norm_g, w_uk, w_uv, w_proj_a, w_proj_b, w_out,
           norm2_g, w_route_group, b_route_group, w_route_expert, b_route_expert, w1, w3, w2, final_g):
    b, seq, d = x.shape
    assert seq % TILE == 0 and TILE % CHUNK == 0 and N_META <= TILE
    lp = TILE + seq
    n_tok = b * lp
    n_real = b * seq
    n_exp = N_GROUPS * EXPERTS_PER_GROUP
    sb_w = SB_HEADS * SB_HEAD_DIM
    qk_dim = MLA_NOPE_DIM + MLA_ROPE_DIM
    mq_w = MLA_HEADS * qk_dim
    sb_scale = SB_HEAD_DIM ** -0.5
    mla_scale = qk_dim ** -0.5

    wi = w_in[0]
    o_q = 3 * sb_w
    o_c = o_q + mq_w
    o_r = o_c + MLA_KV_RANK
    o_g = o_r + MLA_ROPE_DIM
    w_sb = wi[:, :2 * sb_w].astype(BF16)
    w_sbv_t = wi[:, 2 * sb_w:o_q].T.astype(BF16)
    w_mq = wi[:, o_q:o_c].reshape(d, MLA_HEADS, qk_dim)
    w_mq_nope = w_mq[:, :, :MLA_NOPE_DIM].reshape(d, MLA_HEADS * MLA_NOPE_DIM).astype(BF16)
    w_mq_rope = w_mq[:, :, MLA_NOPE_DIM:].reshape(d, MLA_HEADS * MLA_ROPE_DIM).astype(BF16)
    w_c = jnp.pad(wi[:, o_c:o_g], ((0, 0), (0, LANES - MLA_ROPE_DIM))).astype(BF16)
    w_g = wi[:, o_g:].astype(BF16)
    sb_colscale = jnp.concatenate([jnp.full((1, sb_w), sb_scale * LOG2E, F32), jnp.ones((1, sb_w), F32)], axis=1)
    wr = jnp.zeros((d, LANES), F32)
    wr = wr.at[:, :N_GROUPS].set(w_route_group[0]).at[:, EXPERT_LANE0:EXPERT_LANE0 + n_exp].set(w_route_expert[0])
    wr_hi = wr.astype(BF16)
    wr_mid = (wr - wr_hi.astype(F32)).astype(BF16)
    wr3 = jnp.stack([wr_hi, wr_mid])
    br = jnp.zeros((1, LANES), F32)
    br = br.at[0, :N_GROUPS].set(b_route_group[0]).at[0, EXPERT_LANE0:EXPERT_LANE0 + n_exp].set(b_route_expert[0])

    head = jnp.concatenate([jnp.zeros((TILE - N_META, d), F32), meta_tokens.astype(F32)], axis=0)
    hn = _norm1(x, head, norm1_g[0]).reshape(n_tok, d)
    tm = _row_tile(lp)
    tpb = lp // tm
    tables = _rope_tables(lp)
    row_spec = lambda tn: pl.BlockSpec((1, tn), lambda j, i: (0, j))

    sb_qk = _proj(_proj_scale_kernel, hn, w_sb, [sb_colscale], [row_spec(sb_w)], BF16, tm, sb_w)
    sb_vt = _proj_t(hn, w_sbv_t, tm)
    q_mla = _proj_mlaq(hn, w_mq_nope, w_mq_rope, tables, tm, tpb, min(4, MLA_HEADS), mla_scale * LOG2E)
    ckr = _proj(_proj_scale_kernel, hn, w_c, [jnp.ones((1, w_c.shape[1]), F32)],
                [row_spec(w_c.shape[1])], F32, tm, w_c.shape[1])
    gates = _proj(_proj_gate_kernel, hn, w_g, [b_gate[0].reshape(1, 2 * d)], [row_spec(d)], F32, tm, d)
    k_mla, vt_mla = _kvup(ckr, kv_norm_g[0], w_uk[0].astype(BF16), w_uv[0].T.astype(BF16), tables, tm, tpb)

    o_a = _sb_attention(sb_qk.reshape(b, lp, 2 * sb_w), sb_vt, seq)
    o_b = _mla_attention(q_mla.reshape(b, lp, -1), k_mla.reshape(b, lp, -1), vt_mla, seq)
    h1 = _merge(o_a, o_b, gates.reshape(b, lp, 2 * d), x,
                w_proj_a[0].astype(BF16), w_proj_b[0].astype(BF16), w_out[0].astype(BF16)).reshape(n_real, d)

    hp, route, counts = _route(h1, norm2_g[0], wr3, br)
    cnt = counts[0, EXPERT_LANE0:EXPERT_LANE0 + n_exp].astype(jnp.int32)
    padded = (cnt + ROUTE_BLOCK - 1) // ROUTE_BLOCK * ROUTE_BLOCK
    pends = jnp.cumsum(padded)
    pstarts = pends - padded
    n_blocks = (n_real * TOP_K + n_exp * (ROUTE_BLOCK - 1) + ROUTE_BLOCK - 1) // ROUTE_BLOCK
    n_used = (pends[-1] // ROUTE_BLOCK).astype(jnp.int32).reshape(1)
    blk_start = jnp.minimum(jnp.arange(n_blocks), n_used[0] - 1) * ROUTE_BLOCK
    block_e = jnp.minimum(jnp.sum(pends[None, :] <= blk_start[:, None], axis=1), n_exp - 1).astype(jnp.int32)
    ids = route[:, :6].astype(jnp.int32)
    expert_iota = jnp.arange(n_exp, dtype=jnp.int32)[None, :]
    start_of = lambda e: jnp.sum(jnp.where(e[:, None] == expert_iota, pstarts[None, :], 0), axis=1)
    dest0 = (start_of(ids[:, 0]) + ids[:, 4]).astype(jnp.int32)
    dest1 = (start_of(ids[:, 1]) + ids[:, 5]).astype(jnp.int32)

    xs = _dispatch(dest0, dest1, hp, n_blocks * ROUTE_BLOCK)
    y = _experts(block_e, n_used, xs, w1[0], w3[0], w2[0])
    out = _combine(dest0, dest1, h1, route, final_g, y)
    return out.reshape(b, seq, d)
```

```python
import functools

import jax
import jax.numpy as jnp
from jax import lax
from jax.experimental import pallas as pl
from jax.experimental.pallas import tpu as pltpu

N_META = 16
CHUNK = 64
SB_HEADS = 8
SB_HEAD_DIM = 128
MLA_HEADS = 16
MLA_NOPE_DIM = 128
MLA_ROPE_DIM = 64
MLA_V_DIM = 128
MLA_KV_RANK = 512
ROPE_THETA = 10000.0
N_GROUPS = 4
EXPERTS_PER_GROUP = 8
TOP_K = 2
ROUTE_BLOCK = 256
RMS_EPS = 1e-6

TILE = 256
LANES = 128
MLA_QK_PAD = 256
ATTN_Q_TILES = 4
EXPERT_LANE0 = 8
VMEM_LIMIT = 56 * 1024 * 1024
EXPERT_VMEM_LIMIT = 62 * 1024 * 1024
NEG_BIG = -1e30
DMA_ISSUE_UNROLL = 8
COMBINE_CHUNK = 64
DENOM_ROWS = 16
EXPERT_FF_SPLIT = 2
MLA_HEADS_PER_STEP = 2
SB_HEADS_PER_STEP = 2
STICK_GONE_LOG2 = 152.0

F32 = jnp.float32
BF16 = jnp.bfloat16
LOG2E = 1.4426950408889634


def _cparams(sem):
    return pltpu.CompilerParams(dimension_semantics=sem, vmem_limit_bytes=VMEM_LIMIT)


def _rms(v, g):
    ms = jnp.mean(v * v, axis=-1, keepdims=True)
    return v * lax.rsqrt(ms + RMS_EPS) * g


def _row_tile(lp):
    for t in (768, 512, 256):
        if lp % t == 0:
            return t
    raise ValueError(lp)


def _norm1_kernel(x_ref, head_ref, g_ref, o_ref):
    i = pl.program_id(1)

    @pl.when(i == 0)
    def _():
        o_ref[0] = _rms(head_ref[...], g_ref[...]).astype(BF16)

    @pl.when(i > 0)
    def _():
        o_ref[0] = _rms(x_ref[0], g_ref[...]).astype(BF16)


def _norm1(x, head, g):
    b, s, d = x.shape
    nt = s // TILE + 1
    return pl.pallas_call(
        _norm1_kernel,
        out_shape=jax.ShapeDtypeStruct((b, nt * TILE, d), BF16),
        grid=(b, nt),
        in_specs=[
            pl.BlockSpec((1, TILE, d), lambda bi, i: (bi, jnp.maximum(i - 1, 0), 0)),
            pl.BlockSpec((TILE, d), lambda bi, i: (0, 0)),
            pl.BlockSpec((1, d), lambda bi, i: (0, 0)),
        ],
        out_specs=pl.BlockSpec((1, TILE, d), lambda bi, i: (bi, i, 0)),
        compiler_params=_cparams(("parallel", "parallel")),
        name="norm1",
    )(x, head, g.reshape(1, d))


def _rope_rows(r, cos_t, sin_a, sin_b):
    return r * cos_t + pltpu.roll(r, 96, 1) * sin_a + pltpu.roll(r, 32, 1) * sin_b


def _proj_scale_kernel(x_ref, w_ref, s_ref, o_ref):
    acc = jnp.dot(x_ref[...], w_ref[...], preferred_element_type=F32)
    o_ref[...] = (acc * s_ref[...]).astype(o_ref.dtype)


def _proj_gate_kernel(x_ref, w_ref, b_ref, o_ref):
    acc = jnp.dot(x_ref[...], w_ref[...], preferred_element_type=F32) + b_ref[...]
    o_ref[...] = 1.0 / (1.0 + jnp.exp(-acc))


def _proj_mlaq_kernel(x_ref, wn_ref, wr_ref, cos_ref, sa_ref, sb_ref, o_ref, *, scale):
    tm = x_ref.shape[0]
    parts = 3 if tm % 48 == 0 else 1
    rows = tm // parts
    low = lax.broadcasted_iota(jnp.int32, (rows, LANES), 1) < MLA_ROPE_DIM
    for part in range(parts):
        rs = slice(part * rows, (part + 1) * rows)
        x = x_ref[rs, :]
        acc_n = jnp.dot(x, wn_ref[...], preferred_element_type=F32)
        acc_r = jnp.dot(x, wr_ref[...], preferred_element_type=F32)
        cos_t, sin_a, sin_b = cos_ref[rs, :], sa_ref[rs, :], sb_ref[rs, :]
        for hh in range(acc_n.shape[1] // MLA_NOPE_DIM):
            pair = acc_r[:, (hh // 2) * LANES:(hh // 2 + 1) * LANES]
            if hh % 2:
                pair = pltpu.roll(pair, MLA_ROPE_DIM, 1)
            rope = _rope_rows(jnp.where(low, pair, 0.0), cos_t, sin_a, sin_b) * scale
            c0 = hh * MLA_QK_PAD
            o_ref[rs, c0:c0 + LANES] = (acc_n[:, hh * LANES:(hh + 1) * LANES] * scale).astype(o_ref.dtype)
            o_ref[rs, c0 + LANES:c0 + 2 * LANES] = rope.astype(o_ref.dtype)


def _proj_mlaq(hn2d, wn, wr, tables, tm, tiles_per_batch, group, scale):
    m, k = hn2d.shape
    heads = wn.shape[1] // MLA_NOPE_DIM
    tab_spec = pl.BlockSpec((tm, LANES), lambda j, i: (i % tiles_per_batch, 0))
    return pl.pallas_call(
        functools.partial(_proj_mlaq_kernel, scale=scale),
        out_shape=jax.ShapeDtypeStruct((m, heads * MLA_QK_PAD), BF16),
        grid=(heads // group, m // tm),
        in_specs=[
            pl.BlockSpec((tm, k), lambda j, i: (i, 0)),
            pl.BlockSpec((k, group * MLA_NOPE_DIM), lambda j, i: (0, j)),
            pl.BlockSpec((k, group * MLA_ROPE_DIM), lambda j, i: (0, j)),
            tab_spec, tab_spec, tab_spec,
        ],
        out_specs=pl.BlockSpec((tm, group * MLA_QK_PAD), lambda j, i: (i, j)),
        compiler_params=_cparams(("parallel", "parallel")),
        name="proj_mlaq",
    )(hn2d, wn, wr, *tables)


def _store_lane_tiles(o_ref, val_t):
    for c in range(val_t.shape[1] // TILE):
        o_ref[c] = val_t[:, c * TILE:(c + 1) * TILE].astype(o_ref.dtype)


def _proj_t_kernel(x_ref, wt_ref, o_ref):
    acc_t = lax.dot_general(wt_ref[...], x_ref[...], (((1,), (1,)), ((), ())), preferred_element_type=F32)
    _store_lane_tiles(o_ref, acc_t)


def _proj_t(hn2d, wt, tm):
    m, k = hn2d.shape
    n = wt.shape[0]
    return pl.pallas_call(
        _proj_t_kernel,
        out_shape=jax.ShapeDtypeStruct((m // TILE, n, TILE), BF16),
        grid=(m // tm,),
        in_specs=[pl.BlockSpec((tm, k), lambda i: (i, 0)), pl.BlockSpec((n, k), lambda i: (0, 0))],
        out_specs=pl.BlockSpec((tm // TILE, n, TILE), lambda i: (i, 0, 0)),
        compiler_params=_cparams(("parallel",)),
        name="proj_t",
    )(hn2d, wt)


def _proj(kernel, hn2d, w, extras, extra_specs, out_dtype, tm, tn):
    m, k = hn2d.shape
    n = w.shape[1]
    return pl.pallas_call(
        kernel,
        out_shape=jax.ShapeDtypeStruct((m, n), out_dtype),
        grid=(n // tn, m // tm),
        in_specs=[
            pl.BlockSpec((tm, k), lambda j, i: (i, 0)),
            pl.BlockSpec((k, tn), lambda j, i: (0, j)),
        ] + extra_specs,
        out_specs=pl.BlockSpec((tm, tn), lambda j, i: (i, j)),
        compiler_params=_cparams(("parallel", "parallel")),
        name=getattr(kernel, "__name__", None) or kernel.func.__name__,
    )(hn2d, w, *extras)


def _kvup_kernel(c_ref, g_ref, wk_ref, wvt_ref, cos_ref, sa_ref, sb_ref, k_ref, vt_ref, *, rank):
    ckr = c_ref[...]
    cn = _rms(ckr[:, :rank], g_ref[...]).astype(BF16)
    kn = jnp.dot(cn, wk_ref[...], preferred_element_type=F32)
    vv_t = lax.dot_general(wvt_ref[...], cn, (((1,), (1,)), ((), ())), preferred_element_type=F32)
    rope = _rope_rows(ckr[:, rank:rank + LANES], cos_ref[...], sa_ref[...], sb_ref[...]).astype(BF16)
    for h in range(kn.shape[1] // MLA_NOPE_DIM):
        k_ref[:, h * MLA_QK_PAD:h * MLA_QK_PAD + LANES] = kn[:, h * LANES:(h + 1) * LANES].astype(BF16)
        k_ref[:, h * MLA_QK_PAD + LANES:(h + 1) * MLA_QK_PAD] = rope
    _store_lane_tiles(vt_ref, vv_t)


def _kvup(ckr, g, wk, wvt, tables, tm, tiles_per_batch):
    m, cw = ckr.shape
    rank = wk.shape[0]
    nk = wk.shape[1]
    nv = wvt.shape[0]
    heads = nk // MLA_NOPE_DIM
    tab_spec = pl.BlockSpec((tm, LANES), lambda i: (i % tiles_per_batch, 0))
    return pl.pallas_call(
        functools.partial(_kvup_kernel, rank=rank),
        out_shape=(jax.ShapeDtypeStruct((m, heads * MLA_QK_PAD), BF16),
                   jax.ShapeDtypeStruct((m // TILE, nv, TILE), BF16)),
        grid=(m // tm,),
        in_specs=[
            pl.BlockSpec((tm, cw), lambda i: (i, 0)),
            pl.BlockSpec((1, rank), lambda i: (0, 0)),
            pl.BlockSpec(wk.shape, lambda i: (0, 0)),
            pl.BlockSpec(wvt.shape, lambda i: (0, 0)),
            tab_spec, tab_spec, tab_spec,
        ],
        out_specs=(pl.BlockSpec((tm, heads * MLA_QK_PAD), lambda i: (i, 0)),
                   pl.BlockSpec((tm // TILE, nv, TILE), lambda i: (i, 0, 0))),
        compiler_params=_cparams(("parallel",)),
        name="kvup",
    )(ckr, g.reshape(1, rank), wk, wvt, *tables)


def _softplus2(z):
    neg_abs = pltpu.bitcast(pltpu.bitcast(z, jnp.uint32) | jnp.uint32(0x80000000), F32)
    return jnp.maximum(z, 0.0) + jnp.log2(1.0 + jnp.exp2(neg_abs))


def _sb_kernel(*refs, nsub, nheads, first_valid):
    nq = nsub * nheads
    q_refs = refs[:nq]
    k_ref, vt_ref, o_ref = refs[nq:nq + 3]
    scratch = refs[nq + 3:]
    qs = pl.program_id(2)
    first_tile = 1 + nsub * qs
    for ref in scratch:
        ref[...] = jnp.zeros_like(ref)
    trow = lax.broadcasted_iota(jnp.int32, (TILE, 2 * TILE), 0)
    tcol = lax.broadcasted_iota(jnp.int32, (TILE, 2 * TILE), 1) & (TILE - 1)
    tri2 = jnp.where(tcol >= trow, 1.0, 0.0).astype(BF16)

    def update(g, c0, c1, j, mask):
        carry_ref, acc_ref = scratch[2 * g], scratch[2 * g + 1]
        cols = slice(g * SB_HEAD_DIM, (g + 1) * SB_HEAD_DIM)
        lanes = slice(c0 * TILE, c1 * TILE)
        q = jnp.concatenate([q_refs[g * nsub + c][0] for c in range(c0, c1)], axis=0)
        kb = k_ref[0, pl.ds(pl.multiple_of(j * TILE, TILE), TILE), cols]
        z = lax.dot_general(kb, q, (((1,), (1,)), ((), ())), preferred_element_type=F32)
        sp = _softplus2(z)
        if mask is not None:
            sp = jnp.where(mask, sp, 0.0)
        hi32 = pltpu.bitcast(pltpu.bitcast(sp, jnp.uint32) & jnp.uint32(0xFFFF0000), F32)
        parts = jnp.concatenate([hi32.astype(BF16), (sp - hi32).astype(BF16)], axis=0)
        cs = jnp.dot(tri2, parts, preferred_element_type=F32)
        carry = carry_ref[:, lanes]
        a = jnp.exp2(z - cs - carry)
        if mask is not None:
            a = jnp.where(mask, a, 0.0)
        acc_ref[:, lanes] = acc_ref[:, lanes] + jnp.dot(vt_ref[j, cols, :], a.astype(BF16),
                                                        preferred_element_type=F32)
        carry_ref[:, lanes] = carry + cs[0:1, :]

    def stick_left(g, c0, c1):
        return jnp.min(scratch[2 * g][:, c0 * TILE:c1 * TILE]) < STICK_GONE_LOG2

    def diag_mask(n):
        key = lax.broadcasted_iota(jnp.int32, (TILE, n * TILE), 0)
        qry = lax.broadcasted_iota(jnp.int32, (TILE, n * TILE), 1)
        return (qry >= TILE) | (key < qry)

    half = nsub // 2
    for c in reversed(range(nsub)):
        near = min(c + half, nsub)
        for g in range(nheads):
            update(g, c, near, first_tile + c, diag_mask(near - c))
        if near < nsub:
            for g in range(nheads):
                @pl.when(stick_left(g, near, nsub))
                def _():
                    update(g, near, nsub, first_tile + c, None)

    for g in range(nheads):
        for c0, c1 in ((0, half), (half, nsub)):
            def body(state):
                j, _ = state
                update(g, c0, c1, j, None)
                return j - 1, stick_left(g, c0, c1)

            _, alive = lax.while_loop(lambda st: (st[0] > 0) & st[1], body,
                                      (first_tile - 1, stick_left(g, c0, c1)))

            @pl.when(alive)
            def _():
                key0 = lax.broadcasted_iota(jnp.int32, (TILE, (c1 - c0) * TILE), 0)
                update(g, c0, c1, 0, key0 >= first_valid)

    for g in range(nheads):
        o_ref[0, :, g * SB_HEAD_DIM:(g + 1) * SB_HEAD_DIM] = scratch[2 * g + 1][...].T.astype(o_ref.dtype)


def _sb_attention(qk, vt, seq):
    b, lp, w2 = qk.shape
    heads = w2 // (2 * SB_HEAD_DIM)
    nsub = ATTN_Q_TILES
    nheads = min(SB_HEADS_PER_STEP, heads)
    assert heads % nheads == 0
    rows = nsub * TILE
    wide = nheads * SB_HEAD_DIM
    q_specs = [pl.BlockSpec((1, TILE, SB_HEAD_DIM),
                            functools.partial(lambda bi, h, i, g, c: (bi, nsub * i + 1 + c, nheads * h + g), g=g, c=c))
               for g in range(nheads) for c in range(nsub)]
    return pl.pallas_call(
        functools.partial(_sb_kernel, nsub=nsub, nheads=nheads, first_valid=TILE - N_META),
        out_shape=jax.ShapeDtypeStruct((b, seq, heads * SB_HEAD_DIM), BF16),
        grid=(b, heads // nheads, seq // rows),
        in_specs=q_specs + [
            pl.BlockSpec((1, lp, wide), lambda bi, h, i: (bi, 0, heads // nheads + h)),
            pl.BlockSpec((lp // TILE, wide, TILE), lambda bi, h, i: (bi, h, 0)),
        ],
        out_specs=pl.BlockSpec((1, rows, wide), lambda bi, h, i: (bi, i, h)),
        scratch_shapes=[pltpu.VMEM((1, rows), F32), pltpu.VMEM((SB_HEAD_DIM, rows), F32)] * nheads,
        compiler_params=_cparams(("parallel", "parallel", "arbitrary")),
        name="sb_attention",
    )(*([qk] * (nsub * nheads + 1)), vt)


def _mla_kernel(*refs, nsub, nheads, first_valid):
    nq = nsub * nheads
    q_refs = refs[:nq]
    k_ref, vt_ref, bias_ref, o_ref = refs[nq:nq + 4]
    scratch = refs[nq + 4:]
    qs = pl.program_id(2)
    rows = nsub * TILE
    blk_keys = 2 * TILE
    n_full = (nsub * qs) // 2

    lower = slice(0, rows // 2)
    upper = slice(rows // 2, rows)

    def with_ones(vtb):
        return jnp.concatenate([vtb, jnp.ones((DENOM_ROWS, vtb.shape[1]), BF16)], axis=0)

    class Head:
        def __init__(self, g):
            self.qcols = slice(g * MLA_QK_PAD, (g + 1) * MLA_QK_PAD)
            self.vrows = slice(g * MLA_V_DIM, (g + 1) * MLA_V_DIM)
            self.q_refs = q_refs[g * nsub:(g + 1) * nsub]
            (self.m_ref, self.acc_ref, s0, s1, p0, p1, x0, x1) = scratch[g * 8:(g + 1) * 8]
            self.even = (s0, x0, p0)
            self.odd = (s1, x1, p1)

        def queries(self, lanes=slice(None)):
            tiles = self.q_refs[lanes.start // TILE:] if lanes.start else self.q_refs
            return jnp.concatenate([r[0] for r in tiles], axis=0)

        def vt(self, j):
            return vt_ref[j, self.vrows, :]

        def scores(self, blk, s_ref, smax_ref, lanes=slice(None)):
            kb = k_ref[0, pl.ds(pl.multiple_of((1 + 2 * blk) * TILE, TILE), blk_keys), self.qcols]
            s = lax.dot_general(kb, self.queries(lanes), (((1,), (1,)), ((), ())), preferred_element_type=F32)
            s_ref[:, lanes] = s
            smax_ref[:, lanes] = jnp.max(s, axis=0, keepdims=True)

        def softmax(self, s_ref, smax_ref, p_ref, bias, lanes=slice(None)):
            if bias is None:
                s = s_ref[:, lanes]
                smax = smax_ref[:, lanes]
            else:
                s = s_ref[:, lanes] + bias
                smax = jnp.max(s, axis=0, keepdims=True)
            m_old = self.m_ref[:, lanes]
            m_new = jnp.maximum(m_old, smax)
            alpha = jnp.exp2(m_old - m_new)
            p = jnp.exp2(s - m_new)
            self.m_ref[:, lanes] = m_new
            p_ref[:, lanes] = p.astype(BF16)
            return alpha

        def values(self, blk, p_ref, lanes=slice(None)):
            j = 1 + 2 * blk
            vtb = with_ones(jnp.concatenate([self.vt(j), self.vt(j + 1)], axis=1))
            return jnp.dot(vtb, p_ref[:, lanes], preferred_element_type=F32)

        def stage(self, blk, cur, nxt, bias=None, ahead=slice(None)):
            (s_cur, x_cur, p_cur), (s_nxt, x_nxt, p_prev) = cur, nxt
            self.scores(blk + 1, s_nxt, x_nxt, ahead)
            alpha = self.softmax(s_cur, x_cur, p_cur, bias)
            self.acc_ref[...] = alpha * (self.acc_ref[...] + self.values(jnp.maximum(blk - 1, 0), p_prev))

        def last_stage(self, blk, cur, nxt, bias):
            (s_cur, x_cur, p_cur), (_, _, p_prev) = cur, nxt
            alpha = self.softmax(s_cur, x_cur, p_cur, bias[:, upper], upper)
            pv = self.values(blk - 1, p_prev)
            self.acc_ref[:, lower] = self.acc_ref[:, lower] + pv[:, lower]
            self.acc_ref[:, upper] = alpha * (self.acc_ref[:, upper] + pv[:, upper])

        def start(self):
            s = lax.dot_general(k_ref[0, first_valid:TILE, self.qcols], self.queries(),
                                (((1,), (1,)), ((), ())), preferred_element_type=F32)
            m0 = jnp.max(s, axis=0, keepdims=True)
            p = jnp.exp2(s - m0)
            self.m_ref[...] = m0
            p_tile = jnp.concatenate([jnp.zeros((first_valid, rows), BF16), p.astype(BF16)], axis=0)
            self.acc_ref[...] = jnp.dot(with_ones(self.vt(0)), p_tile, preferred_element_type=F32)
            self.scores(0, self.even[0], self.even[1])
            self.odd[2][...] = jnp.zeros_like(self.odd[2])

        def finish(self):
            acc = jnp.concatenate([self.acc_ref[:, lower],
                                   self.acc_ref[:, upper] + self.values(n_full + 1, self.odd[2], upper)], axis=1)
            out = (acc[:MLA_V_DIM] / acc[MLA_V_DIM:MLA_V_DIM + 1]).T
            o_ref[0, :, self.vrows] = out.astype(o_ref.dtype)

    heads = [Head(g) for g in range(nheads)]
    for hd in heads:
        hd.start()

    def body(u, carry):
        for hd in heads:
            hd.stage(2 * u, hd.even, hd.odd)
        for hd in heads:
            hd.stage(2 * u + 1, hd.odd, hd.even)
        return carry

    lax.fori_loop(0, n_full // 2, body, 0)
    for hd in heads:
        hd.stage(n_full, hd.even, hd.odd, bias=bias_ref[:blk_keys, :], ahead=upper)
    for hd in heads:
        hd.last_stage(n_full + 1, hd.odd, hd.even, bias_ref[blk_keys:, :])
    for hd in heads:
        hd.finish()


def _mla_attention(q, k, vt, seq):
    b, lp, _ = q.shape
    heads = vt.shape[1] // MLA_V_DIM
    nsub = ATTN_Q_TILES
    nheads = min(MLA_HEADS_PER_STEP, heads)
    assert nsub == 4, "the kernel visits the query-overlapping keys as exactly two 2-tile blocks"
    assert heads % nheads == 0
    rows = nsub * TILE
    shift = CHUNK.bit_length() - 1
    key = lax.broadcasted_iota(jnp.int32, (rows, rows), 0)
    qry = lax.broadcasted_iota(jnp.int32, (rows, rows), 1)
    bias = jnp.where((key >> shift) <= (qry >> shift), 0.0, NEG_BIG).astype(F32)
    q_specs = [pl.BlockSpec((1, TILE, MLA_QK_PAD),
                            functools.partial(lambda bi, h, i, g, c: (bi, nsub * i + 1 + c, nheads * h + g), g=g, c=c))
               for g in range(nheads) for c in range(nsub)]
    per_head_scratch = [pltpu.VMEM((1, rows), F32),
                        pltpu.VMEM((MLA_V_DIM + DENOM_ROWS, rows), F32),
                        pltpu.VMEM((2 * TILE, rows), F32), pltpu.VMEM((2 * TILE, rows), F32),
                        pltpu.VMEM((2 * TILE, rows), BF16), pltpu.VMEM((2 * TILE, rows), BF16),
                        pltpu.VMEM((1, rows), F32), pltpu.VMEM((1, rows), F32)]
    return pl.pallas_call(
        functools.partial(_mla_kernel, nsub=nsub, nheads=nheads, first_valid=TILE - N_META),
        out_shape=jax.ShapeDtypeStruct((b, seq, heads * MLA_V_DIM), BF16),
        grid=(b, heads // nheads, seq // rows),
        in_specs=q_specs + [
            pl.BlockSpec((1, lp, nheads * MLA_QK_PAD), lambda bi, h, i: (bi, 0, h)),
            pl.BlockSpec((lp // TILE, nheads * MLA_V_DIM, TILE), lambda bi, h, i: (bi, h, 0)),
            pl.BlockSpec((rows, rows), lambda bi, h, i: (0, 0), pipeline_mode=pl.Buffered(1)),
        ],
        out_specs=pl.BlockSpec((1, rows, nheads * MLA_V_DIM), lambda bi, h, i: (bi, i, h)),
        scratch_shapes=per_head_scratch * nheads,
        compiler_params=_cparams(("parallel", "parallel", "arbitrary")),
        name="mla_attention",
    )(*([q] * (nsub * nheads)), k, vt, bias)


def _merge_kernel(oa_ref, ob_ref, ga_ref, gb_ref, x_ref, wpa_ref, wpb_ref, wo_ref, o_ref):
    pa = jnp.dot(oa_ref[0], wpa_ref[...], preferred_element_type=F32)
    pb = jnp.dot(ob_ref[0], wpb_ref[...], preferred_element_type=F32)
    y = ga_ref[0] * pa + gb_ref[0] * pb
    o_ref[0] = x_ref[0] + jnp.dot(y.astype(BF16), wo_ref[...], preferred_element_type=F32)


def _merge(oa, ob, gates, x, wpa, wpb, wo):
    b, s, d = x.shape
    nt = s // TILE
    resident = lambda w: pl.BlockSpec(w.shape, lambda bi, i: (0, 0), pipeline_mode=pl.Buffered(1))
    return pl.pallas_call(
        _merge_kernel,
        out_shape=jax.ShapeDtypeStruct((b, s, d), F32),
        grid=(b, nt),
        in_specs=[
            pl.BlockSpec((1, TILE, oa.shape[2]), lambda bi, i: (bi, i, 0)),
            pl.BlockSpec((1, TILE, ob.shape[2]), lambda bi, i: (bi, i, 0)),
            pl.BlockSpec((1, TILE, d), lambda bi, i: (bi, i + 1, 0)),
            pl.BlockSpec((1, TILE, d), lambda bi, i: (bi, i + 1, 1)),
            pl.BlockSpec((1, TILE, d), lambda bi, i: (bi, i, 0)),
            resident(wpa), resident(wpb), resident(wo),
        ],
        out_specs=pl.BlockSpec((1, TILE, d), lambda bi, i: (bi, i, 0)),
        compiler_params=_cparams(("parallel", "parallel")),
        name="merge",
    )(oa, ob, gates, gates, x, wpa, wpb, wo)


def _split2(a):
    hi = a.astype(BF16)
    return hi, (a - hi.astype(F32)).astype(BF16)


def _route_kernel(h_ref, g_ref, wr_ref, br_ref, hp_ref, r_ref, cnt_ref, carry_ref, *, tm, half):
    i = pl.program_id(0)

    @pl.when(i == 0)
    def _():
        carry_ref[...] = jnp.zeros_like(carry_ref)

    hn = _rms(h_ref[...], g_ref[...])

    lo_bits = pltpu.bitcast(hn[:, :half].astype(BF16).astype(F32), jnp.uint32)
    hi_bits = pltpu.bitcast(hn[:, half:].astype(BF16).astype(F32), jnp.uint32)
    hp_ref[...] = (hi_bits & jnp.uint32(0xFFFF0000)) | (lo_bits >> 16)

    a_hi, a_mid = _split2(hn)
    w_hi, w_mid = wr_ref[0], wr_ref[1]
    dot = lambda a, w: jnp.dot(a, w, preferred_element_type=F32)
    lg = (dot(a_mid, w_hi) + dot(a_hi, w_mid) + dot(a_hi, w_hi)) + br_ref[...]

    lane = lax.broadcasted_iota(jnp.int32, lg.shape, 1)
    rmax = lambda v: jnp.max(v, axis=1, keepdims=True)
    rmin = lambda v: jnp.min(v, axis=1, keepdims=True)
    rsum = lambda v: jnp.sum(v, axis=1, keepdims=True)

    gmask = lane < N_GROUPS
    gl = jnp.where(gmask, lg, -jnp.inf)
    gmax = rmax(gl)
    gsel = rmin(jnp.where(gl == gmax, lane, LANES))
    p_g = 1.0 / rsum(jnp.where(gmask, jnp.exp(lg - gmax), 0.0))

    e_lo = EXPERT_LANE0 + gsel * EXPERTS_PER_GROUP
    emask = (lane >= e_lo) & (lane < e_lo + EXPERTS_PER_GROUP)
    emax = rmax(jnp.where(emask, lg, -jnp.inf))
    ex = jnp.where(emask, jnp.exp(lg - emax), 0.0)
    prob = jnp.where(emask, ex / rsum(ex), -1.0)
    top1 = rmax(prob)
    i1 = rmin(jnp.where(prob == top1, lane, LANES))
    prob2 = jnp.where(lane == i1, -1.0, prob)
    top2 = rmax(prob2)
    i2 = rmin(jnp.where(prob2 == top2, lane, LANES))
    denom = top1 + top2
    w1 = p_g * top1 / denom
    w2 = p_g * top2 / denom

    sel = ((lane == i1) | (lane == i2))
    row = lax.broadcasted_iota(jnp.int32, (tm, tm), 0)
    col = lax.broadcasted_iota(jnp.int32, (tm, tm), 1)
    before = (col < row).astype(BF16)
    prefix = dot(before, jnp.where(sel, 1.0, 0.0).astype(BF16)) + carry_ref[...]
    rank1 = rsum(jnp.where(lane == i1, prefix, 0.0))
    rank2 = rsum(jnp.where(lane == i2, prefix, 0.0))
    carry_ref[...] = carry_ref[...] + jnp.sum(jnp.where(sel, 1.0, 0.0), axis=0, keepdims=True)
    cnt_ref[...] = carry_ref[...]

    e1 = (i1 - EXPERT_LANE0).astype(F32)
    e2 = (i2 - EXPERT_LANE0).astype(F32)
    out = jnp.zeros(lg.shape, F32)
    for k, val in enumerate((e1, e2, w1, w2, rank1, rank2)):
        out = jnp.where(lane == k, val, out)
    r_ref[...] = out


def _route(h1, g, wr3, br):
    n, d = h1.shape
    tm = TILE
    return pl.pallas_call(
        functools.partial(_route_kernel, tm=tm, half=d // 2),
        out_shape=(jax.ShapeDtypeStruct((n, d // 2), jnp.uint32),
                   jax.ShapeDtypeStruct((n, LANES), F32),
                   jax.ShapeDtypeStruct((1, LANES), F32)),
        grid=(n // tm,),
        in_specs=[
            pl.BlockSpec((tm, d), lambda i: (i, 0)),
            pl.BlockSpec((1, d), lambda i: (0, 0)),
            pl.BlockSpec(wr3.shape, lambda i: (0, 0, 0)),
            pl.BlockSpec((1, LANES), lambda i: (0, 0)),
        ],
        out_specs=(pl.BlockSpec((tm, d // 2), lambda i: (i, 0)),
                   pl.BlockSpec((tm, LANES), lambda i: (i, 0)),
                   pl.BlockSpec((1, LANES), lambda i: (0, 0))),
        scratch_shapes=[pltpu.VMEM((1, LANES), F32)],
        compiler_params=_cparams(("arbitrary",)),
        name="route",
    )(h1, g.reshape(1, d), wr3, br)


def _dispatch_kernel(d0_ref, d1_ref, src_ref, init_ref, dst_ref, sem, *, tm):
    del init_ref
    base = pl.program_id(0) * tm

    def copy(r, dest_ref, s):
        return pltpu.make_async_copy(src_ref.at[pl.ds(r, 1)], dst_ref.at[pl.ds(dest_ref[base + r], 1)], sem.at[s])

    def start(r, c):
        copy(r, d0_ref, 0).start()
        copy(r, d1_ref, 1).start()
        return c

    lax.fori_loop(0, tm, start, 0, unroll=DMA_ISSUE_UNROLL)
    for s in range(2):
        pltpu.make_async_copy(src_ref, dst_ref.at[pl.ds(0, tm)], sem.at[s]).wait()


def _dispatch(dest0, dest1, hp, p_rows):
    n, w = hp.shape
    tm = TILE
    init = jnp.zeros((p_rows, w), hp.dtype)
    return pl.pallas_call(
        functools.partial(_dispatch_kernel, tm=tm),
        out_shape=jax.ShapeDtypeStruct((p_rows, w), hp.dtype),
        grid_spec=pltpu.PrefetchScalarGridSpec(
            num_scalar_prefetch=2,
            grid=(n // tm,),
            in_specs=[pl.BlockSpec((tm, w), lambda i, a, b: (i, 0)), pl.BlockSpec(memory_space=pl.ANY)],
            out_specs=pl.BlockSpec(memory_space=pl.ANY),
            scratch_shapes=[pltpu.SemaphoreType.DMA((2,))],
        ),
        input_output_aliases={3: 0},
        compiler_params=_cparams(("arbitrary",)),
        name="dispatch",
    )(dest0, dest1, hp, init)


def _expert_kernel(nu_ref, fe_ref, fh_ref, new_ref, cslot_ref, uslot_ref,
                   x_ref, w1_ref, w3_ref, w2_ref, y_ref, c13_ref, c2_ref):
    del fe_ref
    j = pl.program_id(0)
    h = pl.program_id(1)
    t = j * EXPERT_FF_SPLIT + h
    live = j < nu_ref[0]

    @pl.when(new_ref[t] == 1)
    def _():
        dst = cslot_ref[t] * EXPERT_FF_SPLIT + fh_ref[t]
        fh = w1_ref.shape[2]
        c13_ref[dst, :, :fh] = w1_ref[0].astype(BF16)
        c13_ref[dst, :, fh:] = w3_ref[0].astype(BF16)
        c2_ref[dst] = w2_ref[0].astype(BF16)

    @pl.when(live)
    def _():
        src = uslot_ref[j] * EXPERT_FF_SPLIT + h
        xw = x_ref[...]
        lo = pltpu.bitcast(xw << 16, F32).astype(BF16)
        hi = pltpu.bitcast(xw & jnp.uint32(0xFFFF0000), F32).astype(BF16)
        xb = jnp.concatenate([lo, hi], axis=1)
        ag = jnp.dot(xb, c13_ref[src], preferred_element_type=F32)
        fh = ag.shape[1] // 2
        a, g = ag[:, :fh], ag[:, fh:]
        hid = (a * (1.0 / (1.0 + jnp.exp(-a))) * g).astype(BF16)
        part = jnp.dot(hid, c2_ref[src], preferred_element_type=F32)

        @pl.when(h == 0)
        def _():
            y_ref[...] = part

        @pl.when(h > 0)
        def _():
            y_ref[...] = y_ref[...] + part

    @pl.when(jnp.logical_not(live) & (h == 0))
    def _():
        y_ref[...] = jnp.zeros_like(y_ref)


def _expert_stream_plan(block_e, n_used, n_exp):
    nb = block_e.shape[0]
    nh = EXPERT_FF_SPLIT
    live = jnp.arange(nb) < n_used[0]
    first = jnp.concatenate([jnp.ones((1,), bool), block_e[1:] != block_e[:-1]]) & live
    seg = jnp.cumsum(first.astype(jnp.int32)) - 1
    n_seg = jnp.sum(first.astype(jnp.int32))
    seg_ids = jnp.arange(n_exp, dtype=jnp.int32)
    seg_expert = jnp.sum(jnp.where(first[None, :] & (seg[None, :] == seg_ids[:, None]), block_e[None, :], 0), axis=1)
    t = jnp.arange(nb * nh, dtype=jnp.int32)
    limit = jnp.where(live[t // nh], nh * seg[t // nh] + 2 * nh - 1, nh * n_seg - 1)
    pos = jnp.minimum(t + jnp.minimum(lax.cummin(limit - t), 0), nh * n_seg - 1)
    new = jnp.concatenate([jnp.ones((1,), jnp.int32), (pos[1:] != pos[:-1]).astype(jnp.int32)])
    item_seg = pos // nh
    fetch_e = jnp.sum(jnp.where(item_seg[:, None] == seg_ids[None, :], seg_expert[None, :], 0), axis=1)
    return (fetch_e.astype(jnp.int32), (pos % nh).astype(jnp.int32), new,
            (item_seg % 2).astype(jnp.int32), (seg % 2).astype(jnp.int32))


def _experts(block_e, n_used, xs, w1, w3, w2):
    p_rows, half = xs.shape
    n_exp, d, ff = w1.shape
    nb = p_rows // ROUTE_BLOCK
    nh = EXPERT_FF_SPLIT
    fh = ff // nh
    fetch_e, fetch_h, new, cslot, uslot = _expert_stream_plan(block_e, n_used, n_exp)
    last = lambda j, nu: jnp.minimum(j, nu[0] - 1)
    step = lambda j, h: j * nh + h
    return pl.pallas_call(
        _expert_kernel,
        out_shape=jax.ShapeDtypeStruct((p_rows, d), F32),
        grid_spec=pltpu.PrefetchScalarGridSpec(
            num_scalar_prefetch=6,
            grid=(nb, nh),
            in_specs=[
                pl.BlockSpec((ROUTE_BLOCK, half), lambda j, h, nu, fe, fhh, *_: (last(j, nu), 0)),
                pl.BlockSpec((1, d, fh), lambda j, h, nu, fe, fhh, *_: (fe[step(j, h)], 0, fhh[step(j, h)])),
                pl.BlockSpec((1, d, fh), lambda j, h, nu, fe, fhh, *_: (fe[step(j, h)], 0, fhh[step(j, h)])),
                pl.BlockSpec((1, fh, d), lambda j, h, nu, fe, fhh, *_: (fe[step(j, h)], fhh[step(j, h)], 0)),
            ],
            out_specs=pl.BlockSpec((ROUTE_BLOCK, d), lambda j, h, *_: (j, 0)),
            scratch_shapes=[pltpu.VMEM((2 * nh, d, 2 * fh), BF16), pltpu.VMEM((2 * nh, fh, d), BF16)],
        ),
        compiler_params=pltpu.CompilerParams(dimension_semantics=("arbitrary", "arbitrary"),
                                             vmem_limit_bytes=EXPERT_VMEM_LIMIT),
        name="experts",
    )(n_used, fetch_e, fetch_h, new, cslot, uslot, xs, w1, w3, w2)


def _combine_kernel(d0_ref, d1_ref, h_ref, r_ref, g_ref, y_ref, o_ref, ya, yb, sem, *, tm):
    i = pl.program_id(0)
    last = pl.num_programs(0) - 1
    slot = i % 2
    ahead = jnp.minimum(i + 1, last)

    def issue(tile, sl, r):
        row = tile * tm + r
        pltpu.make_async_copy(y_ref.at[pl.ds(d0_ref[row], 1)], ya.at[sl, pl.ds(r, 1)], sem.at[sl, 0]).start()
        pltpu.make_async_copy(y_ref.at[pl.ds(d1_ref[row], 1)], yb.at[sl, pl.ds(r, 1)], sem.at[sl, 1]).start()

    def wait(sl):
        pltpu.make_async_copy(y_ref.at[pl.ds(0, tm)], ya.at[sl], sem.at[sl, 0]).wait()
        pltpu.make_async_copy(y_ref.at[pl.ds(0, tm)], yb.at[sl], sem.at[sl, 1]).wait()

    @pl.when(i == 0)
    def _():
        def first(r, c):
            issue(0, 0, r)
            return c
        lax.fori_loop(0, tm, first, 0, unroll=DMA_ISSUE_UNROLL)

    wait(slot)
    lane = lax.broadcasted_iota(jnp.int32, (COMBINE_CHUNK, LANES), 1)

    def chunk(c, carry):
        r0 = pl.multiple_of(c * COMBINE_CHUNK, COMBINE_CHUNK)
        for k in range(COMBINE_CHUNK):
            issue(ahead, 1 - slot, r0 + k)
        rows = pl.ds(r0, COMBINE_CHUNK)
        route = r_ref[rows, :]
        w1 = jnp.sum(jnp.where(lane == 2, route, 0.0), axis=1, keepdims=True)
        w2 = jnp.sum(jnp.where(lane == 3, route, 0.0), axis=1, keepdims=True)
        h2 = h_ref[rows, :] + (ya[slot, rows, :] * w1 + yb[slot, rows, :] * w2)
        o_ref[rows, :] = _rms(h2, g_ref[...])
        return carry

    lax.fori_loop(0, tm // COMBINE_CHUNK, chunk, 0)

    @pl.when(i == last)
    def _():
        wait(1 - slot)


def _combine(dest0, dest1, h1, route, g, y):
    n, d = h1.shape
    tm = TILE
    return pl.pallas_call(
        functools.partial(_combine_kernel, tm=tm),
        out_shape=jax.ShapeDtypeStruct((n, d), F32),
        grid_spec=pltpu.PrefetchScalarGridSpec(
            num_scalar_prefetch=2,
            grid=(n // tm,),
            in_specs=[
                pl.BlockSpec((tm, d), lambda i, a, b: (i, 0)),
                pl.BlockSpec((tm, LANES), lambda i, a, b: (i, 0)),
                pl.BlockSpec((1, d), lambda i, a, b: (0, 0)),
                pl.BlockSpec(memory_space=pl.ANY),
            ],
            out_specs=pl.BlockSpec((tm, d), lambda i, a, b: (i, 0)),
            scratch_shapes=[pltpu.VMEM((2, tm, d), F32), pltpu.VMEM((2, tm, d), F32),
                            pltpu.SemaphoreType.DMA((2, 2))],
        ),
        compiler_params=_cparams(("arbitrary",)),
        name="combine",
    )(dest0, dest1, h1, route, g.reshape(1, d), y)


def _rope_tables(lp):
    half = MLA_ROPE_DIM // 2
    inv = ROPE_THETA ** (-jnp.arange(half, dtype=F32) / half)
    pos = (jnp.arange(lp) - (TILE - N_META)).astype(F32)
    ang = pos[:, None] * inv[None, :]
    cos, sin = jnp.cos(ang), jnp.sin(ang)
    z32 = jnp.zeros((lp, half), F32)
    z64 = jnp.zeros((lp, LANES - MLA_ROPE_DIM), F32)
    return (jnp.concatenate([cos, cos, z64], axis=1),
            jnp.concatenate([-sin, z32, z64], axis=1),
            jnp.concatenate([z32, sin, z64], axis=1))


def kernel(x, meta_tokens, norm1_g, w_in, b_gate, kv_norm_g, w_uk, w_uv, w_proj_a, w_proj_b, w_out,
           norm2_g, w_route_group, b_route_group, w_route_expert, b_route_expert, w1, w3, w2, final_g):
    b, seq, d = x.shape
    assert seq % TILE == 0 and TILE % CHUNK == 0 and N_META <= TILE
    lp = TILE + seq
    n_tok = b * lp
    n_real = b * seq
    n_exp = N_GROUPS * EXPERTS_PER_GROUP
    sb_w = SB_HEADS * SB_HEAD_DIM
    qk_dim = MLA_NOPE_DIM + MLA_ROPE_DIM
    mq_w = MLA_HEADS * qk_dim
    sb_scale = SB_HEAD_DIM ** -0.5
    mla_scale = qk_dim ** -0.5

    wi = w_in[0]
    o_q = 3 * sb_w
    o_c = o_q + mq_w
    o_r = o_c + MLA_KV_RANK
    o_g = o_r + MLA_ROPE_DIM
    w_sb = wi[:, :2 * sb_w].astype(BF16)
    w_sbv_t = wi[:, 2 * sb_w:o_q].T.astype(BF16)
    w_mq = wi[:, o_q:o_c].reshape(d, MLA_HEADS, qk_dim)
    w_mq_nope = w_mq[:, :, :MLA_NOPE_DIM].reshape(d, MLA_HEADS * MLA_NOPE_DIM).astype(BF16)
    w_mq_rope = w_mq[:, :, MLA_NOPE_DIM:].reshape(d, MLA_HEADS * MLA_ROPE_DIM).astype(BF16)
    w_c = jnp.pad(wi[:, o_c:o_g], ((0, 0), (0, LANES - MLA_ROPE_DIM))).astype(BF16)
    w_g = wi[:, o_g:].astype(BF16)
    sb_colscale = jnp.concatenate([jnp.full((1, sb_w), sb_scale * LOG2E, F32), jnp.ones((1, sb_w), F32)], axis=1)
    wr = jnp.zeros((d, LANES), F32)
    wr = wr.at[:, :N_GROUPS].set(w_route_group[0]).at[:, EXPERT_LANE0:EXPERT_LANE0 + n_exp].set(w_route_expert[0])
    wr_hi = wr.astype(BF16)
    wr_mid = (wr - wr_hi.astype(F32)).astype(BF16)
    wr3 = jnp.stack([wr_hi, wr_mid])
    br = jnp.zeros((1, LANES), F32)
    br = br.at[0, :N_GROUPS].set(b_route_group[0]).at[0, EXPERT_LANE0:EXPERT_LANE0 + n_exp].set(b_route_expert[0])

    head = jnp.concatenate([jnp.zeros((TILE - N_META, d), F32), meta_tokens.astype(F32)], axis=0)
    hn = _norm1(x, head, norm1_g[0]).reshape(n_tok, d)
    tm = _row_tile(lp)
    tpb = lp // tm
    tables = _rope_tables(lp)
    row_spec = lambda tn: pl.BlockSpec((1, tn), lambda j, i: (0, j))

    sb_qk = _proj(_proj_scale_kernel, hn, w_sb, [sb_colscale], [row_spec(sb_w)], BF16, tm, sb_w)
    sb_vt = _proj_t(hn, w_sbv_t, tm)
    q_mla = _proj_mlaq(hn, w_mq_nope, w_mq_rope, tables, tm, tpb, min(8, MLA_HEADS), mla_scale * LOG2E)
    ckr = _proj(_proj_scale_kernel, hn, w_c, [jnp.ones((1, w_c.shape[1]), F32)],
                [row_spec(w_c.shape[1])], F32, tm, w_c.shape[1])
    gates = _proj(_proj_gate_kernel, hn, w_g, [b_gate[0].reshape(1, 2 * d)], [row_spec(d)], F32, tm, d)
    k_mla, vt_mla = _kvup(ckr, kv_norm_g[0], w_uk[0].astype(BF16), w_uv[0].T.astype(BF16), tables, tm, tpb)

    o_a = _sb_attention(sb_qk.reshape(b, lp, 2 * sb_w), sb_vt, seq)
    o_b = _mla_attention(q_mla.reshape(b, lp, -1), k_mla.reshape(b, lp, -1), vt_mla, seq)
    h1 = _merge(o_a, o_b, gates.reshape(b, lp, 2 * d), x,
                w_proj_a[0].astype(BF16), w_proj_b[0].astype(BF16), w_out[0].astype(BF16)).reshape(n_real, d)

    hp, route, counts = _route(h1, norm2_g[0], wr3, br)
    cnt = counts[0, EXPERT_LANE0:EXPERT_LANE0 + n_exp].astype(jnp.int32)
    padded = (cnt + ROUTE_BLOCK - 1) // ROUTE_BLOCK * ROUTE_BLOCK
    pends = jnp.cumsum(padded)
    pstarts = pends - padded
    n_blocks = (n_real * TOP_K + n_exp * (ROUTE_BLOCK - 1) + ROUTE_BLOCK - 1) // ROUTE_BLOCK
    n_used = (pends[-1] // ROUTE_BLOCK).astype(jnp.int32).reshape(1)
    blk_start = jnp.minimum(jnp.arange(n_blocks), n_used[0] - 1) * ROUTE_BLOCK
    block_e = jnp.minimum(jnp.sum(pends[None, :] <= blk_start[:, None], axis=1), n_exp - 1).astype(jnp.int32)
    ids = route[:, :6].astype(jnp.int32)
    expert_iota = jnp.arange(n_exp, dtype=jnp.int32)[None, :]
    start_of = lambda e: jnp.sum(jnp.where(e[:, None] == expert_iota, pstarts[None, :], 0), axis=1)
    dest0 = (start_of(ids[:, 0]) + ids[:, 4]).astype(jnp.int32)
    dest1 = (start_of(ids[:, 1]) + ids[:, 5]).astype(jnp.int32)

    xs = _dispatch(dest0, dest1, hp, n_blocks * ROUTE_BLOCK)
    y = _experts(block_e, n_used, xs, w1[0], w3[0], w2[0])
    out = _combine(dest0, dest1, h1, route, final_g, y)
    return out.reshape(b, seq, d)
```

```python
import functools

import jax
import jax.numpy as jnp
from jax import lax
from jax.experimental import pallas as pl
from jax.experimental.pallas import tpu as pltpu

N_META = 16
CHUNK = 64
SB_HEADS = 8
SB_HEAD_DIM = 128
MLA_HEADS = 16
MLA_NOPE_DIM = 128
MLA_ROPE_DIM = 64
MLA_V_DIM = 128
MLA_KV_RANK = 512
ROPE_THETA = 10000.0
N_GROUPS = 4
EXPERTS_PER_GROUP = 8
TOP_K = 2
ROUTE_BLOCK = 256
RMS_EPS = 1e-6

TILE = 256
LANES = 128
MLA_QK_PAD = 256
ATTN_Q_TILES = 4
EXPERT_LANE0 = 8
VMEM_LIMIT = 56 * 1024 * 1024
EXPERT_VMEM_LIMIT = 62 * 1024 * 1024
NEG_BIG = -1e30
DMA_ISSUE_UNROLL = 8
COMBINE_CHUNK = 64
DENOM_ROWS = 16
EXPERT_FF_SPLIT = 2
MLA_HEADS_PER_STEP = 2
SB_HEADS_PER_STEP = 2
STICK_GONE_LOG2 = 152.0

F32 = jnp.float32
BF16 = jnp.bfloat16
LOG2E = 1.4426950408889634


def _cparams(sem):
    return pltpu.CompilerParams(dimension_semantics=sem, vmem_limit_bytes=VMEM_LIMIT)


def _rms(v, g):
    ms = jnp.mean(v * v, axis=-1, keepdims=True)
    return v * lax.rsqrt(ms + RMS_EPS) * g


def _row_tile(lp):
    for t in (768, 512, 256):
        if lp % t == 0:
            return t
    raise ValueError(lp)


def _norm1_kernel(x_ref, head_ref, g_ref, o_ref):
    i = pl.program_id(1)

    @pl.when(i == 0)
    def _():
        o_ref[0] = _rms(head_ref[...], g_ref[...]).astype(BF16)

    @pl.when(i > 0)
    def _():
        o_ref[0] = _rms(x_ref[0], g_ref[...]).astype(BF16)


def _norm1(x, head, g):
    b, s, d = x.shape
    nt = s // TILE + 1
    return pl.pallas_call(
        _norm1_kernel,
        out_shape=jax.ShapeDtypeStruct((b, nt * TILE, d), BF16),
        grid=(b, nt),
        in_specs=[
            pl.BlockSpec((1, TILE, d), lambda bi, i: (bi, jnp.maximum(i - 1, 0), 0)),
            pl.BlockSpec((TILE, d), lambda bi, i: (0, 0)),
            pl.BlockSpec((1, d), lambda bi, i: (0, 0)),
        ],
        out_specs=pl.BlockSpec((1, TILE, d), lambda bi, i: (bi, i, 0)),
        compiler_params=_cparams(("parallel", "parallel")),
        name="norm1",
    )(x, head, g.reshape(1, d))


def _rope_rows(r, cos_t, sin_a, sin_b):
    return r * cos_t + pltpu.roll(r, 96, 1) * sin_a + pltpu.roll(r, 32, 1) * sin_b


def _proj_scale_kernel(x_ref, w_ref, s_ref, o_ref):
    acc = jnp.dot(x_ref[...], w_ref[...], preferred_element_type=F32)
    o_ref[...] = (acc * s_ref[...]).astype(o_ref.dtype)


def _proj_gate_kernel(x_ref, w_ref, b_ref, o_ref):
    acc = jnp.dot(x_ref[...], w_ref[...], preferred_element_type=F32) + b_ref[...]
    o_ref[...] = 1.0 / (1.0 + jnp.exp(-acc))


def _proj_mlaq_kernel(x_ref, wn_ref, wr_ref, cos_ref, sa_ref, sb_ref, o_ref, *, scale):
    tm = x_ref.shape[0]
    parts = 3 if tm % 48 == 0 else 1
    rows = tm // parts
    low = lax.broadcasted_iota(jnp.int32, (rows, LANES), 1) < MLA_ROPE_DIM
    for part in range(parts):
        rs = slice(part * rows, (part + 1) * rows)
        x = x_ref[rs, :]
        acc_n = jnp.dot(x, wn_ref[...], preferred_element_type=F32)
        acc_r = jnp.dot(x, wr_ref[...], preferred_element_type=F32)
        cos_t, sin_a, sin_b = cos_ref[rs, :], sa_ref[rs, :], sb_ref[rs, :]
        for hh in range(acc_n.shape[1] // MLA_NOPE_DIM):
            pair = acc_r[:, (hh // 2) * LANES:(hh // 2 + 1) * LANES]
            if hh % 2:
                pair = pltpu.roll(pair, MLA_ROPE_DIM, 1)
            rope = _rope_rows(jnp.where(low, pair, 0.0), cos_t, sin_a, sin_b) * scale
            c0 = hh * MLA_QK_PAD
            o_ref[rs, c0:c0 + LANES] = (acc_n[:, hh * LANES:(hh + 1) * LANES] * scale).astype(o_ref.dtype)
            o_ref[rs, c0 + LANES:c0 + 2 * LANES] = rope.astype(o_ref.dtype)


def _proj_mlaq(hn2d, wn, wr, tables, tm, tiles_per_batch, group, scale):
    m, k = hn2d.shape
    heads = wn.shape[1] // MLA_NOPE_DIM
    tab_spec = pl.BlockSpec((tm, LANES), lambda j, i: (i % tiles_per_batch, 0))
    return pl.pallas_call(
        functools.partial(_proj_mlaq_kernel, scale=scale),
        out_shape=jax.ShapeDtypeStruct((m, heads * MLA_QK_PAD), BF16),
        grid=(heads // group, m // tm),
        in_specs=[
            pl.BlockSpec((tm, k), lambda j, i: (i, 0)),
            pl.BlockSpec((k, group * MLA_NOPE_DIM), lambda j, i: (0, j)),
            pl.BlockSpec((k, group * MLA_ROPE_DIM), lambda j, i: (0, j)),
            tab_spec, tab_spec, tab_spec,
        ],
        out_specs=pl.BlockSpec((tm, group * MLA_QK_PAD), lambda j, i: (i, j)),
        compiler_params=_cparams(("parallel", "parallel")),
        name="proj_mlaq",
    )(hn2d, wn, wr, *tables)


def _store_lane_tiles(o_ref, val_t):
    for c in range(val_t.shape[1] // TILE):
        o_ref[c] = val_t[:, c * TILE:(c + 1) * TILE].astype(o_ref.dtype)


def _proj_t_kernel(x_ref, wt_ref, o_ref):
    acc_t = lax.dot_general(wt_ref[...], x_ref[...], (((1,), (1,)), ((), ())), preferred_element_type=F32)
    _store_lane_tiles(o_ref, acc_t)


def _proj_t(hn2d, wt, tm):
    m, k = hn2d.shape
    n = wt.shape[0]
    return pl.pallas_call(
        _proj_t_kernel,
        out_shape=jax.ShapeDtypeStruct((m // TILE, n, TILE), BF16),
        grid=(m // tm,),
        in_specs=[pl.BlockSpec((tm, k), lambda i: (i, 0)), pl.BlockSpec((n, k), lambda i: (0, 0))],
        out_specs=pl.BlockSpec((tm // TILE, n, TILE), lambda i: (i, 0, 0)),
        compiler_params=_cparams(("parallel",)),
        name="proj_t",
    )(hn2d, wt)


def _proj(kernel, hn2d, w, extras, extra_specs, out_dtype, tm, tn):
    m, k = hn2d.shape
    n = w.shape[1]
    return pl.pallas_call(
        kernel,
        out_shape=jax.ShapeDtypeStruct((m, n), out_dtype),
        grid=(n // tn, m // tm),
        in_specs=[
            pl.BlockSpec((tm, k), lambda j, i: (i, 0)),
            pl.BlockSpec((k, tn), lambda j, i: (0, j)),
        ] + extra_specs,
        out_specs=pl.BlockSpec((tm, tn), lambda j, i: (i, j)),
        compiler_params=_cparams(("parallel", "parallel")),
        name=getattr(kernel, "__name__", None) or kernel.func.__name__,
    )(hn2d, w, *extras)


def _kvup_kernel(c_ref, g_ref, wk_ref, wvt_ref, cos_ref, sa_ref, sb_ref, k_ref, vt_ref, *, rank):
    ckr = c_ref[...]
    cn = _rms(ckr[:, :rank], g_ref[...]).astype(BF16)
    kn = jnp.dot(cn, wk_ref[...], preferred_element_type=F32)
    vv_t = lax.dot_general(wvt_ref[...], cn, (((1,), (1,)), ((), ())), preferred_element_type=F32)
    rope = _rope_rows(ckr[:, rank:rank + LANES], cos_ref[...], sa_ref[...], sb_ref[...]).astype(BF16)
    for h in range(kn.shape[1] // MLA_NOPE_DIM):
        k_ref[:, h * MLA_QK_PAD:h * MLA_QK_PAD + LANES] = kn[:, h * LANES:(h + 1) * LANES].astype(BF16)
        k_ref[:, h * MLA_QK_PAD + LANES:(h + 1) * MLA_QK_PAD] = rope
    _store_lane_tiles(vt_ref, vv_t)


def _kvup(ckr, g, wk, wvt, tables, tm, tiles_per_batch):
    m, cw = ckr.shape
    rank = wk.shape[0]
    nk = wk.shape[1]
    nv = wvt.shape[0]
    heads = nk // MLA_NOPE_DIM
    tab_spec = pl.BlockSpec((tm, LANES), lambda i: (i % tiles_per_batch, 0))
    return pl.pallas_call(
        functools.partial(_kvup_kernel, rank=rank),
        out_shape=(jax.ShapeDtypeStruct((m, heads * MLA_QK_PAD), BF16),
                   jax.ShapeDtypeStruct((m // TILE, nv, TILE), BF16)),
        grid=(m // tm,),
        in_specs=[
            pl.BlockSpec((tm, cw), lambda i: (i, 0)),
            pl.BlockSpec((1, rank), lambda i: (0, 0)),
            pl.BlockSpec(wk.shape, lambda i: (0, 0)),
            pl.BlockSpec(wvt.shape, lambda i: (0, 0)),
            tab_spec, tab_spec, tab_spec,
        ],
        out_specs=(pl.BlockSpec((tm, heads * MLA_QK_PAD), lambda i: (i, 0)),
                   pl.BlockSpec((tm // TILE, nv, TILE), lambda i: (i, 0, 0))),
        compiler_params=_cparams(("parallel",)),
        name="kvup",
    )(ckr, g.reshape(1, rank), wk, wvt, *tables)


def _softplus2(z):
    neg_abs = pltpu.bitcast(pltpu.bitcast(z, jnp.uint32) | jnp.uint32(0x80000000), F32)
    return jnp.maximum(z, 0.0) + jnp.log2(1.0 + jnp.exp2(neg_abs))


def _sb_kernel(*refs, nsub, nheads, first_valid):
    nq = nsub * nheads
    q_refs = refs[:nq]
    k_ref, vt_ref, o_ref = refs[nq:nq + 3]
    scratch = refs[nq + 3:]
    qs = pl.program_id(2)
    first_tile = 1 + nsub * qs
    for ref in scratch:
        ref[...] = jnp.zeros_like(ref)
    trow = lax.broadcasted_iota(jnp.int32, (TILE, 2 * TILE), 0)
    tcol = lax.broadcasted_iota(jnp.int32, (TILE, 2 * TILE), 1) & (TILE - 1)
    tri2 = jnp.where(tcol >= trow, 1.0, 0.0).astype(BF16)

    def update(g, c0, c1, j, mask):
        carry_ref, acc_ref = scratch[2 * g], scratch[2 * g + 1]
        cols = slice(g * SB_HEAD_DIM, (g + 1) * SB_HEAD_DIM)
        lanes = slice(c0 * TILE, c1 * TILE)
        q = jnp.concatenate([q_refs[g * nsub + c][0] for c in range(c0, c1)], axis=0)
        kb = k_ref[0, pl.ds(pl.multiple_of(j * TILE, TILE), TILE), cols]
        z = lax.dot_general(kb, q, (((1,), (1,)), ((), ())), preferred_element_type=F32)
        sp = _softplus2(z)
        if mask is not None:
            sp = jnp.where(mask, sp, 0.0)
        hi32 = pltpu.bitcast(pltpu.bitcast(sp, jnp.uint32) & jnp.uint32(0xFFFF0000), F32)
        parts = jnp.concatenate([hi32.astype(BF16), (sp - hi32).astype(BF16)], axis=0)
        cs = jnp.dot(tri2, parts, preferred_element_type=F32)
        carry = carry_ref[:, lanes]
        a = jnp.exp2(z - cs - carry)
        if mask is not None:
            a = jnp.where(mask, a, 0.0)
        acc_ref[:, lanes] = acc_ref[:, lanes] + jnp.dot(vt_ref[j, cols, :], a.astype(BF16),
                                                        preferred_element_type=F32)
        carry_ref[:, lanes] = carry + cs[0:1, :]

    def stick_left(g, c0, c1):
        return jnp.min(scratch[2 * g][:, c0 * TILE:c1 * TILE]) < STICK_GONE_LOG2

    def diag_mask(n):
        key = lax.broadcasted_iota(jnp.int32, (TILE, n * TILE), 0)
        qry = lax.broadcasted_iota(jnp.int32, (TILE, n * TILE), 1)
        return (qry >= TILE) | (key < qry)

    def update_all(c0, c1, j, mask):
        lanes = slice(c0 * TILE, c1 * TILE)
        n = (c1 - c0) * TILE
        kstart = pl.multiple_of(j * TILE, TILE)
        zs = []
        for g in range(nheads):
            q = jnp.concatenate([q_refs[g * nsub + c][0] for c in range(c0, c1)], axis=0)
            kb = k_ref[0, pl.ds(kstart, TILE), g * SB_HEAD_DIM:(g + 1) * SB_HEAD_DIM]
            zs.append(lax.dot_general(kb, q, (((1,), (1,)), ((), ())), preferred_element_type=F32))
        z = jnp.concatenate(zs, axis=1)
        wide = jnp.concatenate([mask] * nheads, axis=1)
        sp = jnp.where(wide, _softplus2(z), 0.0)
        hi32 = pltpu.bitcast(pltpu.bitcast(sp, jnp.uint32) & jnp.uint32(0xFFFF0000), F32)
        parts = jnp.concatenate([hi32.astype(BF16), (sp - hi32).astype(BF16)], axis=0)
        cs = jnp.dot(tri2, parts, preferred_element_type=F32)
        carry = jnp.concatenate([scratch[2 * g][:, lanes] for g in range(nheads)], axis=1)
        a = jnp.where(wide, jnp.exp2(z - cs - carry), 0.0).astype(BF16)
        carry = carry + cs[0:1, :]
        for g in range(nheads):
            part = slice(g * n, (g + 1) * n)
            acc_ref = scratch[2 * g + 1]
            acc_ref[:, lanes] = acc_ref[:, lanes] + jnp.dot(
                vt_ref[j, g * SB_HEAD_DIM:(g + 1) * SB_HEAD_DIM, :], a[:, part], preferred_element_type=F32)
            scratch[2 * g][:, lanes] = carry[:, part]

    half = nsub // 2
    for c in reversed(range(nsub)):
        near = min(c + half, nsub)
        update_all(c, near, first_tile + c, diag_mask(near - c))
        if near < nsub:
            for g in range(nheads):
                @pl.when(stick_left(g, near, nsub))
                def _():
                    update(g, near, nsub, first_tile + c, None)

    for g in range(nheads):
        for c0, c1 in ((0, half), (half, nsub)):
            def body(state):
                j, _ = state
                update(g, c0, c1, j, None)
                return j - 1, stick_left(g, c0, c1)

            _, alive = lax.while_loop(lambda st: (st[0] > 0) & st[1], body,
                                      (first_tile - 1, stick_left(g, c0, c1)))

            @pl.when(alive)
            def _():
                key0 = lax.broadcasted_iota(jnp.int32, (TILE, (c1 - c0) * TILE), 0)
                update(g, c0, c1, 0, key0 >= first_valid)

    for g in range(nheads):
        o_ref[0, :, g * SB_HEAD_DIM:(g + 1) * SB_HEAD_DIM] = scratch[2 * g + 1][...].T.astype(o_ref.dtype)


def _sb_attention(qk, vt, seq):
    b, lp, w2 = qk.shape
    heads = w2 // (2 * SB_HEAD_DIM)
    nsub = ATTN_Q_TILES
    nheads = min(SB_HEADS_PER_STEP, heads)
    assert heads % nheads == 0
    rows = nsub * TILE
    wide = nheads * SB_HEAD_DIM
    q_specs = [pl.BlockSpec((1, TILE, SB_HEAD_DIM),
                            functools.partial(lambda bi, h, i, g, c: (bi, nsub * i + 1 + c, nheads * h + g), g=g, c=c))
               for g in range(nheads) for c in range(nsub)]
    return pl.pallas_call(
        functools.partial(_sb_kernel, nsub=nsub, nheads=nheads, first_valid=TILE - N_META),
        out_shape=jax.ShapeDtypeStruct((b, seq, heads * SB_HEAD_DIM), BF16),
        grid=(b, heads // nheads, seq // rows),
        in_specs=q_specs + [
            pl.BlockSpec((1, lp, wide), lambda bi, h, i: (bi, 0, heads // nheads + h)),
            pl.BlockSpec((lp // TILE, wide, TILE), lambda bi, h, i: (bi, h, 0)),
        ],
        out_specs=pl.BlockSpec((1, rows, wide), lambda bi, h, i: (bi, i, h)),
        scratch_shapes=[pltpu.VMEM((1, rows), F32), pltpu.VMEM((SB_HEAD_DIM, rows), F32)] * nheads,
        compiler_params=_cparams(("parallel", "parallel", "arbitrary")),
        name="sb_attention",
    )(*([qk] * (nsub * nheads + 1)), vt)


def _mla_kernel(*refs, nsub, nheads, first_valid):
    nq = nsub * nheads
    q_refs = refs[:nq]
    k_ref, vt_ref, bias_ref, o_ref = refs[nq:nq + 4]
    scratch = refs[nq + 4:]
    qs = pl.program_id(2)
    rows = nsub * TILE
    blk_keys = 2 * TILE
    n_full = (nsub * qs) // 2

    lower = slice(0, rows // 2)
    upper = slice(rows // 2, rows)

    def with_ones(vtb):
        return jnp.concatenate([vtb, jnp.ones((DENOM_ROWS, vtb.shape[1]), BF16)], axis=0)

    class Head:
        def __init__(self, g):
            self.qcols = slice(g * MLA_QK_PAD, (g + 1) * MLA_QK_PAD)
            self.vrows = slice(g * MLA_V_DIM, (g + 1) * MLA_V_DIM)
            self.q_refs = q_refs[g * nsub:(g + 1) * nsub]
            (self.m_ref, self.acc_ref, s0, s1, p0, p1, x0, x1) = scratch[g * 8:(g + 1) * 8]
            self.even = (s0, x0, p0)
            self.odd = (s1, x1, p1)

        def queries(self, lanes=slice(None)):
            tiles = self.q_refs[lanes.start // TILE:] if lanes.start else self.q_refs
            return jnp.concatenate([r[0] for r in tiles], axis=0)

        def vt(self, j):
            return vt_ref[j, self.vrows, :]

        def scores(self, blk, s_ref, smax_ref, lanes=slice(None)):
            kb = k_ref[0, pl.ds(pl.multiple_of((1 + 2 * blk) * TILE, TILE), blk_keys), self.qcols]
            s = lax.dot_general(kb, self.queries(lanes), (((1,), (1,)), ((), ())), preferred_element_type=F32)
            s_ref[:, lanes] = s
            smax_ref[:, lanes] = jnp.max(s, axis=0, keepdims=True)

        def softmax(self, s_ref, smax_ref, p_ref, bias, lanes=slice(None)):
            if bias is None:
                s = s_ref[:, lanes]
                smax = smax_ref[:, lanes]
            else:
                s = s_ref[:, lanes] + bias
                smax = jnp.max(s, axis=0, keepdims=True)
            m_old = self.m_ref[:, lanes]
            m_new = jnp.maximum(m_old, smax)
            alpha = jnp.exp2(m_old - m_new)
            p = jnp.exp2(s - m_new)
            self.m_ref[:, lanes] = m_new
            p_ref[:, lanes] = p.astype(BF16)
            return alpha

        def values(self, blk, p_ref, lanes=slice(None)):
            j = 1 + 2 * blk
            vtb = with_ones(jnp.concatenate([self.vt(j), self.vt(j + 1)], axis=1))
            return jnp.dot(vtb, p_ref[:, lanes], preferred_element_type=F32)

        def stage(self, blk, cur, nxt, bias=None, ahead=slice(None)):
            (s_cur, x_cur, p_cur), (s_nxt, x_nxt, p_prev) = cur, nxt
            self.scores(blk + 1, s_nxt, x_nxt, ahead)
            alpha = self.softmax(s_cur, x_cur, p_cur, bias)
            self.acc_ref[...] = alpha * (self.acc_ref[...] + self.values(jnp.maximum(blk - 1, 0), p_prev))

        def last_stage(self, blk, cur, nxt, bias):
            (s_cur, x_cur, p_cur), (_, _, p_prev) = cur, nxt
            alpha = self.softmax(s_cur, x_cur, p_cur, bias[:, upper], upper)
            pv = self.values(blk - 1, p_prev)
            self.acc_ref[:, lower] = self.acc_ref[:, lower] + pv[:, lower]
            self.acc_ref[:, upper] = alpha * (self.acc_ref[:, upper] + pv[:, upper])

        def start(self):
            s = lax.dot_general(k_ref[0, first_valid:TILE, self.qcols], self.queries(),
                                (((1,), (1,)), ((), ())), preferred_element_type=F32)
            m0 = jnp.max(s, axis=0, keepdims=True)
            p = jnp.exp2(s - m0)
            self.m_ref[...] = m0
            p_tile = jnp.concatenate([jnp.zeros((first_valid, rows), BF16), p.astype(BF16)], axis=0)
            self.acc_ref[...] = jnp.dot(with_ones(self.vt(0)), p_tile, preferred_element_type=F32)
            self.scores(0, self.even[0], self.even[1])
            self.odd[2][...] = jnp.zeros_like(self.odd[2])

        def finish(self):
            acc = jnp.concatenate([self.acc_ref[:, lower],
                                   self.acc_ref[:, upper] + self.values(n_full + 1, self.odd[2], upper)], axis=1)
            out = (acc[:MLA_V_DIM] / acc[MLA_V_DIM:MLA_V_DIM + 1]).T
            o_ref[0, :, self.vrows] = out.astype(o_ref.dtype)

    heads = [Head(g) for g in range(nheads)]
    for hd in heads:
        hd.start()

    def body(u, carry):
        for hd in heads:
            hd.stage(2 * u, hd.even, hd.odd)
        for hd in heads:
            hd.stage(2 * u + 1, hd.odd, hd.even)
        return carry

    lax.fori_loop(0, n_full // 2, body, 0)
    for hd in heads:
        hd.stage(n_full, hd.even, hd.odd, bias=bias_ref[:blk_keys, :], ahead=upper)
    for hd in heads:
        hd.last_stage(n_full + 1, hd.odd, hd.even, bias_ref[blk_keys:, :])
    for hd in heads:
        hd.finish()


def _mla_attention(q, k, vt, seq):
    b, lp, _ = q.shape
    heads = vt.shape[1] // MLA_V_DIM
    nsub = ATTN_Q_TILES
    nheads = min(MLA_HEADS_PER_STEP, heads)
    assert nsub == 4, "the kernel visits the query-overlapping keys as exactly two 2-tile blocks"
    assert heads % nheads == 0
    rows = nsub * TILE
    shift = CHUNK.bit_length() - 1
    key = lax.broadcasted_iota(jnp.int32, (rows, rows), 0)
    qry = lax.broadcasted_iota(jnp.int32, (rows, rows), 1)
    bias = jnp.where((key >> shift) <= (qry >> shift), 0.0, NEG_BIG).astype(F32)
    q_specs = [pl.BlockSpec((1, TILE, MLA_QK_PAD),
                            functools.partial(lambda bi, h, i, g, c: (bi, nsub * i + 1 + c, nheads * h + g), g=g, c=c))
               for g in range(nheads) for c in range(nsub)]
    per_head_scratch = [pltpu.VMEM((1, rows), F32),
                        pltpu.VMEM((MLA_V_DIM + DENOM_ROWS, rows), F32),
                        pltpu.VMEM((2 * TILE, rows), F32), pltpu.VMEM((2 * TILE, rows), F32),
                        pltpu.VMEM((2 * TILE, rows), BF16), pltpu.VMEM((2 * TILE, rows), BF16),
                        pltpu.VMEM((1, rows), F32), pltpu.VMEM((1, rows), F32)]
    return pl.pallas_call(
        functools.partial(_mla_kernel, nsub=nsub, nheads=nheads, first_valid=TILE - N_META),
        out_shape=jax.ShapeDtypeStruct((b, seq, heads * MLA_V_DIM), BF16),
        grid=(b, heads // nheads, seq // rows),
        in_specs=q_specs + [
            pl.BlockSpec((1, lp, nheads * MLA_QK_PAD), lambda bi, h, i: (bi, 0, h)),
            pl.BlockSpec((lp // TILE, nheads * MLA_V_DIM, TILE), lambda bi, h, i: (bi, h, 0)),
            pl.BlockSpec((rows, rows), lambda bi, h, i: (0, 0), pipeline_mode=pl.Buffered(1)),
        ],
        out_specs=pl.BlockSpec((1, rows, nheads * MLA_V_DIM), lambda bi, h, i: (bi, i, h)),
        scratch_shapes=per_head_scratch * nheads,
        compiler_params=_cparams(("parallel", "parallel", "arbitrary")),
        name="mla_attention",
    )(*([q] * (nsub * nheads)), k, vt, bias)


def _merge_kernel(oa_ref, ob_ref, ga_ref, gb_ref, x_ref, wpa_ref, wpb_ref, wo_ref, o_ref):
    pa = jnp.dot(oa_ref[0], wpa_ref[...], preferred_element_type=F32)
    pb = jnp.dot(ob_ref[0], wpb_ref[...], preferred_element_type=F32)
    y = ga_ref[0] * pa + gb_ref[0] * pb
    o_ref[0] = x_ref[0] + jnp.dot(y.astype(BF16), wo_ref[...], preferred_element_type=F32)


def _merge(oa, ob, gates, x, wpa, wpb, wo):
    b, s, d = x.shape
    nt = s // TILE
    resident = lambda w: pl.BlockSpec(w.shape, lambda bi, i: (0, 0), pipeline_mode=pl.Buffered(1))
    return pl.pallas_call(
        _merge_kernel,
        out_shape=jax.ShapeDtypeStruct((b, s, d), F32),
        grid=(b, nt),
        in_specs=[
            pl.BlockSpec((1, TILE, oa.shape[2]), lambda bi, i: (bi, i, 0)),
            pl.BlockSpec((1, TILE, ob.shape[2]), lambda bi, i: (bi, i, 0)),
            pl.BlockSpec((1, TILE, d), lambda bi, i: (bi, i + 1, 0)),
            pl.BlockSpec((1, TILE, d), lambda bi, i: (bi, i + 1, 1)),
            pl.BlockSpec((1, TILE, d), lambda bi, i: (bi, i, 0)),
            resident(wpa), resident(wpb), resident(wo),
        ],
        out_specs=pl.BlockSpec((1, TILE, d), lambda bi, i: (bi, i, 0)),
        compiler_params=_cparams(("parallel", "parallel")),
        name="merge",
    )(oa, ob, gates, gates, x, wpa, wpb, wo)


def _split2(a):
    hi = a.astype(BF16)
    return hi, (a - hi.astype(F32)).astype(BF16)


def _route_kernel(h_ref, g_ref, wr_ref, br_ref, hp_ref, r_ref, cnt_ref, carry_ref, *, tm, half):
    i = pl.program_id(0)

    @pl.when(i == 0)
    def _():
        carry_ref[...] = jnp.zeros_like(carry_ref)

    hn = _rms(h_ref[...], g_ref[...])

    lo_bits = pltpu.bitcast(hn[:, :half].astype(BF16).astype(F32), jnp.uint32)
    hi_bits = pltpu.bitcast(hn[:, half:].astype(BF16).astype(F32), jnp.uint32)
    hp_ref[...] = (hi_bits & jnp.uint32(0xFFFF0000)) | (lo_bits >> 16)

    a_hi, a_mid = _split2(hn)
    w_hi, w_mid = wr_ref[0], wr_ref[1]
    dot = lambda a, w: jnp.dot(a, w, preferred_element_type=F32)
    lg = (dot(a_mid, w_hi) + dot(a_hi, w_mid) + dot(a_hi, w_hi)) + br_ref[...]

    lane = lax.broadcasted_iota(jnp.int32, lg.shape, 1)
    rmax = lambda v: jnp.max(v, axis=1, keepdims=True)
    rmin = lambda v: jnp.min(v, axis=1, keepdims=True)
    rsum = lambda v: jnp.sum(v, axis=1, keepdims=True)

    gmask = lane < N_GROUPS
    gl = jnp.where(gmask, lg, -jnp.inf)
    gmax = rmax(gl)
    gsel = rmin(jnp.where(gl == gmax, lane, LANES))
    p_g = 1.0 / rsum(jnp.where(gmask, jnp.exp(lg - gmax), 0.0))

    e_lo = EXPERT_LANE0 + gsel * EXPERTS_PER_GROUP
    emask = (lane >= e_lo) & (lane < e_lo + EXPERTS_PER_GROUP)
    emax = rmax(jnp.where(emask, lg, -jnp.inf))
    ex = jnp.where(emask, jnp.exp(lg - emax), 0.0)
    prob = jnp.where(emask, ex / rsum(ex), -1.0)
    top1 = rmax(prob)
    i1 = rmin(jnp.where(prob == top1, lane, LANES))
    prob2 = jnp.where(lane == i1, -1.0, prob)
    top2 = rmax(prob2)
    i2 = rmin(jnp.where(prob2 == top2, lane, LANES))
    denom = top1 + top2
    w1 = p_g * top1 / denom
    w2 = p_g * top2 / denom

    sel = ((lane == i1) | (lane == i2))
    row = lax.broadcasted_iota(jnp.int32, (tm, tm), 0)
    col = lax.broadcasted_iota(jnp.int32, (tm, tm), 1)
    before = (col < row).astype(BF16)
    prefix = dot(before, jnp.where(sel, 1.0, 0.0).astype(BF16)) + carry_ref[...]
    rank1 = rsum(jnp.where(lane == i1, prefix, 0.0))
    rank2 = rsum(jnp.where(lane == i2, prefix, 0.0))
    carry_ref[...] = carry_ref[...] + jnp.sum(jnp.where(sel, 1.0, 0.0), axis=0, keepdims=True)
    cnt_ref[...] = carry_ref[...]

    e1 = (i1 - EXPERT_LANE0).astype(F32)
    e2 = (i2 - EXPERT_LANE0).astype(F32)
    out = jnp.zeros(lg.shape, F32)
    for k, val in enumerate((e1, e2, w1, w2, rank1, rank2)):
        out = jnp.where(lane == k, val, out)
    r_ref[...] = out


def _route(h1, g, wr3, br):
    n, d = h1.shape
    tm = TILE
    return pl.pallas_call(
        functools.partial(_route_kernel, tm=tm, half=d // 2),
        out_shape=(jax.ShapeDtypeStruct((n, d // 2), jnp.uint32),
                   jax.ShapeDtypeStruct((n, LANES), F32),
                   jax.ShapeDtypeStruct((1, LANES), F32)),
        grid=(n // tm,),
        in_specs=[
            pl.BlockSpec((tm, d), lambda i: (i, 0)),
            pl.BlockSpec((1, d), lambda i: (0, 0)),
            pl.BlockSpec(wr3.shape, lambda i: (0, 0, 0)),
            pl.BlockSpec((1, LANES), lambda i: (0, 0)),
        ],
        out_specs=(pl.BlockSpec((tm, d // 2), lambda i: (i, 0)),
                   pl.BlockSpec((tm, LANES), lambda i: (i, 0)),
                   pl.BlockSpec((1, LANES), lambda i: (0, 0))),
        scratch_shapes=[pltpu.VMEM((1, LANES), F32)],
        compiler_params=_cparams(("arbitrary",)),
        name="route",
    )(h1, g.reshape(1, d), wr3, br)


def _dispatch_kernel(d0_ref, d1_ref, src_ref, init_ref, dst_ref, sem, *, tm):
    del init_ref
    base = pl.program_id(0) * tm

    def copy(r, dest_ref, s):
        return pltpu.make_async_copy(src_ref.at[pl.ds(r, 1)], dst_ref.at[pl.ds(dest_ref[base + r], 1)], sem.at[s])

    def start(r, c):
        copy(r, d0_ref, 0).start()
        copy(r, d1_ref, 1).start()
        return c

    lax.fori_loop(0, tm, start, 0, unroll=DMA_ISSUE_UNROLL)
    for s in range(2):
        pltpu.make_async_copy(src_ref, dst_ref.at[pl.ds(0, tm)], sem.at[s]).wait()


def _dispatch(dest0, dest1, hp, p_rows):
    n, w = hp.shape
    tm = TILE
    init = jnp.zeros((p_rows, w), hp.dtype)
    return pl.pallas_call(
        functools.partial(_dispatch_kernel, tm=tm),
        out_shape=jax.ShapeDtypeStruct((p_rows, w), hp.dtype),
        grid_spec=pltpu.PrefetchScalarGridSpec(
            num_scalar_prefetch=2,
            grid=(n // tm,),
            in_specs=[pl.BlockSpec((tm, w), lambda i, a, b: (i, 0)), pl.BlockSpec(memory_space=pl.ANY)],
            out_specs=pl.BlockSpec(memory_space=pl.ANY),
            scratch_shapes=[pltpu.SemaphoreType.DMA((2,))],
        ),
        input_output_aliases={3: 0},
        compiler_params=_cparams(("arbitrary",)),
        name="dispatch",
    )(dest0, dest1, hp, init)


def _expert_kernel(nu_ref, fe_ref, fh_ref, new_ref, cslot_ref, uslot_ref,
                   x_ref, w1_ref, w3_ref, w2_ref, y_ref, c13_ref, c2_ref):
    del fe_ref
    j = pl.program_id(0)
    h = pl.program_id(1)
    t = j * EXPERT_FF_SPLIT + h
    live = j < nu_ref[0]

    @pl.when(new_ref[t] == 1)
    def _():
        dst = cslot_ref[t] * EXPERT_FF_SPLIT + fh_ref[t]
        fh = w1_ref.shape[2]
        c13_ref[dst, :, :fh] = w1_ref[0].astype(BF16)
        c13_ref[dst, :, fh:] = w3_ref[0].astype(BF16)
        c2_ref[dst] = w2_ref[0].astype(BF16)

    @pl.when(live)
    def _():
        src = uslot_ref[j] * EXPERT_FF_SPLIT + h
        xw = x_ref[...]
        lo = pltpu.bitcast(xw << 16, F32).astype(BF16)
        hi = pltpu.bitcast(xw & jnp.uint32(0xFFFF0000), F32).astype(BF16)
        xb = jnp.concatenate([lo, hi], axis=1)
        ag = jnp.dot(xb, c13_ref[src], preferred_element_type=F32)
        fh = ag.shape[1] // 2
        a, g = ag[:, :fh], ag[:, fh:]
        hid = (a * (1.0 / (1.0 + jnp.exp(-a))) * g).astype(BF16)
        part = jnp.dot(hid, c2_ref[src], preferred_element_type=F32)

        @pl.when(h == 0)
        def _():
            y_ref[...] = part

        @pl.when(h > 0)
        def _():
            y_ref[...] = y_ref[...] + part

    @pl.when(jnp.logical_not(live) & (h == 0))
    def _():
        y_ref[...] = jnp.zeros_like(y_ref)


def _expert_stream_plan(block_e, n_used, n_exp):
    nb = block_e.shape[0]
    nh = EXPERT_FF_SPLIT
    live = jnp.arange(nb) < n_used[0]
    first = jnp.concatenate([jnp.ones((1,), bool), block_e[1:] != block_e[:-1]]) & live
    seg = jnp.cumsum(first.astype(jnp.int32)) - 1
    n_seg = jnp.sum(first.astype(jnp.int32))
    seg_ids = jnp.arange(n_exp, dtype=jnp.int32)
    seg_expert = jnp.sum(jnp.where(first[None, :] & (seg[None, :] == seg_ids[:, None]), block_e[None, :], 0), axis=1)
    t = jnp.arange(nb * nh, dtype=jnp.int32)
    limit = jnp.where(live[t // nh], nh * seg[t // nh] + 2 * nh - 1, nh * n_seg - 1)
    pos = jnp.minimum(t + jnp.minimum(lax.cummin(limit - t), 0), nh * n_seg - 1)
    new = jnp.concatenate([jnp.ones((1,), jnp.int32), (pos[1:] != pos[:-1]).astype(jnp.int32)])
    item_seg = pos // nh
    fetch_e = jnp.sum(jnp.where(item_seg[:, None] == seg_ids[None, :], seg_expert[None, :], 0), axis=1)
    return (fetch_e.astype(jnp.int32), (pos % nh).astype(jnp.int32), new,
            (item_seg % 2).astype(jnp.int32), (seg % 2).astype(jnp.int32))


def _experts(block_e, n_used, xs, w1, w3, w2):
    p_rows, half = xs.shape
    n_exp, d, ff = w1.shape
    nb = p_rows // ROUTE_BLOCK
    nh = EXPERT_FF_SPLIT
    fh = ff // nh
    fetch_e, fetch_h, new, cslot, uslot = _expert_stream_plan(block_e, n_used, n_exp)
    last = lambda j, nu: jnp.minimum(j, nu[0] - 1)
    step = lambda j, h: j * nh + h
    return pl.pallas_call(
        _expert_kernel,
        out_shape=jax.ShapeDtypeStruct((p_rows, d), F32),
        grid_spec=pltpu.PrefetchScalarGridSpec(
            num_scalar_prefetch=6,
            grid=(nb, nh),
            in_specs=[
                pl.BlockSpec((ROUTE_BLOCK, half), lambda j, h, nu, fe, fhh, *_: (last(j, nu), 0)),
                pl.BlockSpec((1, d, fh), lambda j, h, nu, fe, fhh, *_: (fe[step(j, h)], 0, fhh[step(j, h)])),
                pl.BlockSpec((1, d, fh), lambda j, h, nu, fe, fhh, *_: (fe[step(j, h)], 0, fhh[step(j, h)])),
                pl.BlockSpec((1, fh, d), lambda j, h, nu, fe, fhh, *_: (fe[step(j, h)], fhh[step(j, h)], 0)),
            ],
            out_specs=pl.BlockSpec((ROUTE_BLOCK, d), lambda j, h, *_: (j, 0)),
            scratch_shapes=[pltpu.VMEM((2 * nh, d, 2 * fh), BF16), pltpu.VMEM((2 * nh, fh, d), BF16)],
        ),
        compiler_params=pltpu.CompilerParams(dimension_semantics=("arbitrary", "arbitrary"),
                                             vmem_limit_bytes=EXPERT_VMEM_LIMIT),
        name="experts",
    )(n_used, fetch_e, fetch_h, new, cslot, uslot, xs, w1, w3, w2)


def _combine_kernel(d0_ref, d1_ref, h_ref, r_ref, g_ref, y_ref, o_ref, ya, yb, sem, *, tm):
    i = pl.program_id(0)
    last = pl.num_programs(0) - 1
    slot = i % 2
    ahead = jnp.minimum(i + 1, last)

    def issue(tile, sl, r):
        row = tile * tm + r
        pltpu.make_async_copy(y_ref.at[pl.ds(d0_ref[row], 1)], ya.at[sl, pl.ds(r, 1)], sem.at[sl, 0]).start()
        pltpu.make_async_copy(y_ref.at[pl.ds(d1_ref[row], 1)], yb.at[sl, pl.ds(r, 1)], sem.at[sl, 1]).start()

    def wait(sl):
        pltpu.make_async_copy(y_ref.at[pl.ds(0, tm)], ya.at[sl], sem.at[sl, 0]).wait()
        pltpu.make_async_copy(y_ref.at[pl.ds(0, tm)], yb.at[sl], sem.at[sl, 1]).wait()

    @pl.when(i == 0)
    def _():
        def first(r, c):
            issue(0, 0, r)
            return c
        lax.fori_loop(0, tm, first, 0, unroll=DMA_ISSUE_UNROLL)

    wait(slot)
    lane = lax.broadcasted_iota(jnp.int32, (COMBINE_CHUNK, LANES), 1)

    def chunk(c, carry):
        r0 = pl.multiple_of(c * COMBINE_CHUNK, COMBINE_CHUNK)
        for k in range(COMBINE_CHUNK):
            issue(ahead, 1 - slot, r0 + k)
        rows = pl.ds(r0, COMBINE_CHUNK)
        route = r_ref[rows, :]
        w1 = jnp.sum(jnp.where(lane == 2, route, 0.0), axis=1, keepdims=True)
        w2 = jnp.sum(jnp.where(lane == 3, route, 0.0), axis=1, keepdims=True)
        h2 = h_ref[rows, :] + (ya[slot, rows, :] * w1 + yb[slot, rows, :] * w2)
        o_ref[rows, :] = _rms(h2, g_ref[...])
        return carry

    lax.fori_loop(0, tm // COMBINE_CHUNK, chunk, 0)

    @pl.when(i == last)
    def _():
        wait(1 - slot)


def _combine(dest0, dest1, h1, route, g, y):
    n, d = h1.shape
    tm = TILE
    return pl.pallas_call(
        functools.partial(_combine_kernel, tm=tm),
        out_shape=jax.ShapeDtypeStruct((n, d), F32),
        grid_spec=pltpu.PrefetchScalarGridSpec(
            num_scalar_prefetch=2,
            grid=(n // tm,),
            in_specs=[
                pl.BlockSpec((tm, d), lambda i, a, b: (i, 0)),
                pl.BlockSpec((tm, LANES), lambda i, a, b: (i, 0)),
                pl.BlockSpec((1, d), lambda i, a, b: (0, 0)),
                pl.BlockSpec(memory_space=pl.ANY),
            ],
            out_specs=pl.BlockSpec((tm, d), lambda i, a, b: (i, 0)),
            scratch_shapes=[pltpu.VMEM((2, tm, d), F32), pltpu.VMEM((2, tm, d), F32),
                            pltpu.SemaphoreType.DMA((2, 2))],
        ),
        compiler_params=_cparams(("arbitrary",)),
        name="combine",
    )(dest0, dest1, h1, route, g.reshape(1, d), y)


def _rope_tables(lp):
    half = MLA_ROPE_DIM // 2
    inv = ROPE_THETA ** (-jnp.arange(half, dtype=F32) / half)
    pos = (jnp.arange(lp) - (TILE - N_META)).astype(F32)
    ang = pos[:, None] * inv[None, :]
    cos, sin = jnp.cos(ang), jnp.sin(ang)
    z32 = jnp.zeros((lp, half), F32)
    z64 = jnp.zeros((lp, LANES - MLA_ROPE_DIM), F32)
    return (jnp.concatenate([cos, cos, z64], axis=1),
            jnp.concatenate([-sin, z32, z64], axis=1),
            jnp.concatenate([z32, sin, z64], axis=1))


def kernel(x, meta_tokens, norm1_g, w_in, b_gate, kv_norm_g, w_uk, w_uv, w_proj_a, w_proj_b, w_out,
           norm2_g, w_route_group, b_route_group, w_route_expert, b_route_expert, w1, w3, w2, final_g):
    b, seq, d = x.shape
    assert seq % TILE == 0 and TILE % CHUNK == 0 and N_META <= TILE
    lp = TILE + seq
    n_tok = b * lp
    n_real = b * seq
    n_exp = N_GROUPS * EXPERTS_PER_GROUP
    sb_w = SB_HEADS * SB_HEAD_DIM
    qk_dim = MLA_NOPE_DIM + MLA_ROPE_DIM
    mq_w = MLA_HEADS * qk_dim
    sb_scale = SB_HEAD_DIM ** -0.5
    mla_scale = qk_dim ** -0.5

    wi = w_in[0]
    o_q = 3 * sb_w
    o_c = o_q + mq_w
    o_r = o_c + MLA_KV_RANK
    o_g = o_r + MLA_ROPE_DIM
    w_sb = wi[:, :2 * sb_w].astype(BF16)
    w_sbv_t = wi[:, 2 * sb_w:o_q].T.astype(BF16)
    w_mq = wi[:, o_q:o_c].reshape(d, MLA_HEADS, qk_dim)
    w_mq_nope = w_mq[:, :, :MLA_NOPE_DIM].reshape(d, MLA_HEADS * MLA_NOPE_DIM).astype(BF16)
    w_mq_rope = w_mq[:, :, MLA_NOPE_DIM:].reshape(d, MLA_HEADS * MLA_ROPE_DIM).astype(BF16)
    w_c = jnp.pad(wi[:, o_c:o_g], ((0, 0), (0, LANES - MLA_ROPE_DIM))).astype(BF16)
    w_g = wi[:, o_g:].astype(BF16)
    sb_colscale = jnp.concatenate([jnp.full((1, sb_w), sb_scale * LOG2E, F32), jnp.ones((1, sb_w), F32)], axis=1)
    wr = jnp.zeros((d, LANES), F32)
    wr = wr.at[:, :N_GROUPS].set(w_route_group[0]).at[:, EXPERT_LANE0:EXPERT_LANE0 + n_exp].set(w_route_expert[0])
    wr_hi = wr.astype(BF16)
    wr_mid = (wr - wr_hi.astype(F32)).astype(BF16)
    wr3 = jnp.stack([wr_hi, wr_mid])
    br = jnp.zeros((1, LANES), F32)
    br = br.at[0, :N_GROUPS].set(b_route_group[0]).at[0, EXPERT_LANE0:EXPERT_LANE0 + n_exp].set(b_route_expert[0])

    head = jnp.concatenate([jnp.zeros((TILE - N_META, d), F32), meta_tokens.astype(F32)], axis=0)
    hn = _norm1(x, head, norm1_g[0]).reshape(n_tok, d)
    tm = _row_tile(lp)
    tpb = lp // tm
    tables = _rope_tables(lp)
    row_spec = lambda tn: pl.BlockSpec((1, tn), lambda j, i: (0, j))

    sb_qk = _proj(_proj_scale_kernel, hn, w_sb, [sb_colscale], [row_spec(sb_w)], BF16, tm, sb_w)
    sb_vt = _proj_t(hn, w_sbv_t, tm)
    q_mla = _proj_mlaq(hn, w_mq_nope, w_mq_rope, tables, tm, tpb, min(8, MLA_HEADS), mla_scale * LOG2E)
    ckr = _proj(_proj_scale_kernel, hn, w_c, [jnp.ones((1, w_c.shape[1]), F32)],
                [row_spec(w_c.shape[1])], F32, tm, w_c.shape[1])
    gates = _proj(_proj_gate_kernel, hn, w_g, [b_gate[0].reshape(1, 2 * d)], [row_spec(d)], F32, tm, d)
    k_mla, vt_mla = _kvup(ckr, kv_norm_g[0], w_uk[0].astype(BF16), w_uv[0].T.astype(BF16), tables, tm, tpb)

    o_a = _sb_attention(sb_qk.reshape(b, lp, 2 * sb_w), sb_vt, seq)
    o_b = _mla_attention(q_mla.reshape(b, lp, -1), k_mla.reshape(b, lp, -1), vt_mla, seq)
    h1 = _merge(o_a, o_b, gates.reshape(b, lp, 2 * d), x,
                w_proj_a[0].astype(BF16), w_proj_b[0].astype(BF16), w_out[0].astype(BF16)).reshape(n_real, d)

    hp, route, counts = _route(h1, norm2_g[0], wr3, br)
    cnt = counts[0, EXPERT_LANE0:EXPERT_LANE0 + n_exp].astype(jnp.int32)
    padded = (cnt + ROUTE_BLOCK - 1) // ROUTE_BLOCK * ROUTE_BLOCK
    pends = jnp.cumsum(padded)
    pstarts = pends - padded
    n_blocks = (n_real * TOP_K + n_exp * (ROUTE_BLOCK - 1) + ROUTE_BLOCK - 1) // ROUTE_BLOCK
    n_used = (pends[-1] // ROUTE_BLOCK).astype(jnp.int32).reshape(1)
    blk_start = jnp.minimum(jnp.arange(n_blocks), n_used[0] - 1) * ROUTE_BLOCK
    block_e = jnp.minimum(jnp.sum(pends[None, :] <= blk_start[:, None], axis=1), n_exp - 1).astype(jnp.int32)
    ids = route[:, :6].astype(jnp.int32)
    expert_iota = jnp.arange(n_exp, dtype=jnp.int32)[None, :]
    start_of = lambda e: jnp.sum(jnp.where(e[:, None] == expert_iota, pstarts[None, :], 0), axis=1)
    dest0 = (start_of(ids[:, 0]) + ids[:, 4]).astype(jnp.int32)
    dest1 = (start_of(ids[:, 1]) + ids[:, 5]).astype(jnp.int32)

    xs = _dispatch(dest0, dest1, hp, n_blocks * ROUTE_BLOCK)
    y = _experts(block_e, n_used, xs, w1[0], w3[0], w2[0])
    out = _combine(dest0, dest1, h1, route, final_g, y)
    return out.reshape(b, seq, d)
```
